```python
import jax, jax.numpy as jnp
from jax import lax
import numpy as np

D_MODEL = 1024
BATCH = 2
SEQ = 8192
DEPTH = 1
DEC_BATCH = 128
DEC_SEQ = 4
PAST_LEN = 2048
PAGE_SIZE = 128

MIX_WIDTH = D_MODEL
MIX_A = MIX_WIDTH // 2
A_GROUPS = 8
A_DG = MIX_A // A_GROUPS
CHUNK = 128
MIX_B = MIX_WIDTH - MIX_A
HEAD_DIM = 64
N_HEADS = MIX_B // HEAD_DIM
N_KV = 4
Q_PER_KV = N_HEADS // N_KV
KV_W = N_KV * HEAD_DIM
L_CMP = 32
STRIDE = 16
CMP_HID = 256
L_SLC = 64
N_SEL = 16
WINDOW = 512
Q_BLOCK = 128
D_FF = -(-(8 * D_MODEL) // (3 * 256)) * 256
IN_COLS = 2 * MIX_A + MIX_B + 6 * KV_W + 3 * N_HEADS
ROPE_THETA = 10000.0
EPS = 1e-6
FORCED_SCORE = 1e4

kernel_name = 'hymba_gmlp_nsa_decoder_step'


def rmsnorm(x, g):
    xf = x.astype(jnp.float32)
    y = xf * lax.rsqrt(jnp.mean(xf * xf, axis=-1, keepdims=True) + EPS)
    return (y * g.astype(jnp.float32)).astype(x.dtype)


def modulate(x, g, shift, scale):
    return rmsnorm(x, g) * (1 + scale[:, None]) + shift[:, None]


def rope(x, pos):
    half = HEAD_DIM // 2
    inv = ROPE_THETA ** (-jnp.arange(half, dtype=jnp.float32) / half)
    ang = pos.astype(jnp.float32)[:, None] * inv[None, :]
    cos, sin = jnp.cos(ang)[:, None, :], jnp.sin(ang)[:, None, :]
    xf = x.astype(jnp.float32)
    x1, x2 = xf[..., :half], xf[..., half:]
    return jnp.concatenate([x1 * cos - x2 * sin, x2 * cos + x1 * sin], axis=-1).astype(x.dtype)


def masked_probs(s, mask):
    s = jnp.where(mask, s, -jnp.inf)
    m = jnp.max(s, axis=-1, keepdims=True)
    m = jnp.where(jnp.isfinite(m), m, 0.0)
    p = jnp.where(mask, jnp.exp(s - m), 0.0)
    return p / jnp.maximum(jnp.sum(p, axis=-1, keepdims=True), 1e-20)


def project(h, pos, w_in, g_sgu, g_q, g_k_cmp, g_k_slc, g_k_win):
    B, S = h.shape[:2]
    sizes = [MIX_A, MIX_A, MIX_B] + [KV_W] * 6
    cuts = [sum(sizes[:i + 1]) for i in range(len(sizes))]
    u, v, q, kc, vc, ks, vs, kw, vw, gl = jnp.split(h @ w_in, cuts, axis=-1)
    u = jax.nn.gelu(u).reshape(B, S, A_GROUPS, A_DG)
    v = rmsnorm(jax.nn.gelu(v).reshape(B, S, A_GROUPS, A_DG), g_sgu.reshape(A_GROUPS, A_DG))
    hd = lambda a, n: a.reshape(B, S, n, HEAD_DIM)
    q = rope(rmsnorm(hd(q, N_HEADS), g_q), pos)
    kc = rope(rmsnorm(hd(kc, N_KV), g_k_cmp), pos)
    ks = rope(rmsnorm(hd(ks, N_KV), g_k_slc), pos)
    kw = rope(rmsnorm(hd(kw, N_KV), g_k_win), pos)
    gates = jax.nn.sigmoid(gl.reshape(B, S, N_HEADS, 3))
    return u, v, q, kc, hd(vc, N_KV), ks, hd(vs, N_KV), kw, hd(vw, N_KV), gates


def chunk_mlp(u, v, w_s, b_s):
    B, S = u.shape[:2]
    n_c = -(-S // CHUNK)
    vp = jnp.pad(v, ((0, 0), (0, n_c * CHUNK - S), (0, 0), (0, 0)))
    vp = vp.reshape(B, n_c, CHUNK, A_GROUPS, A_DG)
    w = jnp.where(jnp.tril(jnp.ones((CHUNK, CHUNK), bool)), w_s, 0)
    mixed = jnp.einsum('gts,bcsgd->bctgd', w, vp) + b_s.T[None, None, :, :, None]
    mixed = mixed.reshape(B, n_c * CHUNK, A_GROUPS, A_DG)[:, :S]
    return (u * mixed).reshape(B, S, MIX_A)


def compress(k, pe, w1, w2):
    B, T = k.shape[:2]
    nc = (T - L_CMP) // STRIDE + 1
    idx = jnp.arange(nc)[:, None] * STRIDE + jnp.arange(L_CMP)[None, :]
    blk = k[:, idx] + pe[None, None, :, None, :]
    blk = blk.transpose(0, 1, 3, 2, 4).reshape(B, nc, N_KV, L_CMP * HEAD_DIM)
    return jax.nn.gelu(blk @ w1) @ w2


def to_blocks(k):
    B, T = k.shape[:2]
    n_s = -(-T // L_SLC)
    k = jnp.pad(k, ((0, 0), (0, n_s * L_SLC - T), (0, 0), (0, 0)))
    return k.reshape(B, n_s, L_SLC, N_KV, HEAD_DIM).transpose(0, 3, 1, 2, 4)


def nsa_context(kc, vc, ks, vs, pe_k, pe_v, w_ck1, w_ck2, w_cv1, w_cv2):
    kc_c = compress(kc, pe_k, w_ck1, w_ck2)
    vc_c = compress(vc, pe_v, w_cv1, w_cv2)
    c_end = jnp.arange(kc_c.shape[1]) * STRIDE + (L_CMP - 1)
    return kc_c, vc_c, c_end, to_blocks(ks), to_blocks(vs)


def nsa_core(q, gates, t, kc_c, vc_c, c_end, ks_blk, vs_blk, kw, vw, w_pos):
    B, Q = q.shape[:2]
    f32 = jnp.float32
    qg = q.astype(f32).reshape(B, Q, N_KV, Q_PER_KV, HEAD_DIM) * (HEAD_DIM ** -0.5)
    s_c = jnp.einsum('bqgrd,bcgd->bqgrc', qg, kc_c.astype(f32))
    p_c = masked_probs(s_c, (c_end[None, :] <= t[:, None])[None, :, None, None, :])
    o_c = jnp.einsum('bqgrc,bcgd->bqgrd', p_c, vc_c.astype(f32))
    n_c, n_s = c_end.shape[0], ks_blk.shape[2]
    c_start = jnp.arange(n_c) * STRIDE
    blk = jnp.arange(n_s)
    overlap = ((c_start[:, None] < (blk[None, :] + 1) * L_SLC)
               & (c_start[:, None] + L_CMP > blk[None, :] * L_SLC)).astype(f32)
    imp = jnp.einsum('bqgrc,cn->bqgn', p_c, overlap)
    cur = t // L_SLC
    valid = blk[None, :] <= cur[:, None]
    forced = valid & ((blk[None, :] == 0) | (blk[None, :] >= cur[:, None] - 1))
    score = jnp.where(forced[None, :, None, :], FORCED_SCORE,
                      jnp.where(valid[None, :, None, :], imp, -jnp.inf))
    _, idx = lax.top_k(score, min(N_SEL, n_s))
    bi = jnp.arange(B)[:, None, None, None]
    gi = jnp.arange(N_KV)[None, None, :, None]
    n_k = idx.shape[-1] * L_SLC
    kb = ks_blk[bi, gi, idx].reshape(B, Q, N_KV, n_k, HEAD_DIM).astype(f32)
    vb = vs_blk[bi, gi, idx].reshape(B, Q, N_KV, n_k, HEAD_DIM).astype(f32)
    tok = (idx[..., None] * L_SLC + jnp.arange(L_SLC)).reshape(B, Q, N_KV, n_k)
    s_s = jnp.einsum('bqgrd,bqgkd->bqgrk', qg, kb)
    p_s = masked_probs(s_s, (tok <= t[None, :, None, None])[:, :, :, None, :])
    o_s = jnp.einsum('bqgrk,bqgkd->bqgrd', p_s, vb)
    s_w = jnp.einsum('bqgrd,bkgd->bqgrk', qg, kw.astype(f32))
    rel = t[:, None] - w_pos[None, :]
    m_w = (rel >= 0) & (rel < WINDOW) & (w_pos[None, :] >= 0)
    p_w = masked_probs(s_w, m_w[None, :, None, None, :])
    o_w = jnp.einsum('bqgrk,bkgd->bqgrd', p_w, vw.astype(f32))
    g = gates.astype(f32).reshape(B, Q, N_KV, Q_PER_KV, 3)
    o = g[..., 0:1] * o_c + g[..., 1:2] * o_s + g[..., 2:3] * o_w
    return o.reshape(B, Q, MIX_B).astype(q.dtype)


def nsa_prompt(q, gates, kc_c, vc_c, c_end, ks_blk, vs_blk, kw, vw):
    B, S = q.shape[:2]
    pad = ((0, 0), (WINDOW, 0), (0, 0), (0, 0))
    kw_pad, vw_pad = jnp.pad(kw, pad), jnp.pad(vw, pad)

    def block(qs):
        t = qs + jnp.arange(Q_BLOCK)
        qb = lax.dynamic_slice_in_dim(q, qs, Q_BLOCK, axis=1)
        gb = lax.dynamic_slice_in_dim(gates, qs, Q_BLOCK, axis=1)
        kwb = lax.dynamic_slice_in_dim(kw_pad, qs, WINDOW + Q_BLOCK, axis=1)
        vwb = lax.dynamic_slice_in_dim(vw_pad, qs, WINDOW + Q_BLOCK, axis=1)
        w_pos = qs - WINDOW + jnp.arange(WINDOW + Q_BLOCK)
        return nsa_core(qb, gb, t, kc_c, vc_c, c_end, ks_blk, vs_blk, kwb, vwb, w_pos)

    out = lax.map(block, jnp.arange(S // Q_BLOCK) * Q_BLOCK)
    return out.transpose(1, 0, 2, 3).reshape(B, S, MIX_B)


def gather_pages(cache, page_table):
    rows = cache[page_table]
    return rows.reshape(page_table.shape[0], -1, N_KV, HEAD_DIM)


def finish(x, mix, gate1, shift2, scale2, gate2, w_out, g_ffn_norm, w_ffn_in, w_ffn_out):
    x = x + gate1[:, None] * (mix @ w_out)
    h = modulate(x, g_ffn_norm, shift2, scale2)
    a, b = jnp.split(h @ w_ffn_in, 2, axis=-1)
    return x + gate2[:, None] * ((jax.nn.silu(a) * b) @ w_ffn_out)


def setup_inputs(seed: int = 0) -> dict:
    key = jax.random.key(seed)
    k = jax.random.split(key, 32)
    n_pages = PAST_LEN // PAGE_SIZE
    n_pool = (DEC_BATCH * n_pages * 5) // 4
    wb = min(WINDOW, PAST_LEN)
    f32 = jnp.float32

    def nrm(i, shape, scale=1.0):
        return jax.random.normal(k[i], shape, f32) * scale

    def gain(i, shape):
        return 1.0 + nrm(i, shape, 0.05)

    page_table = jax.random.permutation(k[0], n_pool)[:DEC_BATCH * n_pages]
    page_table = page_table.reshape(DEC_BATCH, n_pages).astype(jnp.int32)
    cache_shape = (DEPTH, n_pool, PAGE_SIZE, N_KV, HEAD_DIM)
    win_shape = (DEPTH, DEC_BATCH, wb, N_KV, HEAD_DIM)
    return {
        'x_prompt': nrm(1, (BATCH, SEQ, D_MODEL)),
        'x_sample': nrm(2, (DEC_BATCH, DEC_SEQ, D_MODEL)),
        'cache_k_cmp': nrm(3, cache_shape),
        'cache_v_cmp': nrm(4, cache_shape),
        'cache_k_slc': nrm(5, cache_shape),
        'cache_v_slc': nrm(6, cache_shape),
        'state_k_win': nrm(7, win_shape),
        'state_v_win': nrm(8, win_shape),
        'page_table': page_table,
        'c_prompt': nrm(9, (BATCH, D_MODEL)),
        'c_sample': nrm(10, (DEC_BATCH, D_MODEL)),
        'w_ada': nrm(11, (DEPTH, D_MODEL, 6 * D_MODEL), 0.5 * D_MODEL ** -0.5),
        'b_ada': nrm(12, (DEPTH, 6 * D_MODEL), 0.1),
        'g_mix_norm': gain(13, (DEPTH, D_MODEL)),
        'g_ffn_norm': gain(14, (DEPTH, D_MODEL)),
        'w_in': nrm(15, (DEPTH, D_MODEL, IN_COLS), D_MODEL ** -0.5),
        'g_sgu': gain(16, (DEPTH, MIX_A)),
        'w_sgu': nrm(17, (DEPTH, A_GROUPS, CHUNK, CHUNK), CHUNK ** -0.5),
        'b_sgu': 1.0 + nrm(18, (DEPTH, A_GROUPS, CHUNK), 0.1),
        'g_q': gain(19, (DEPTH, HEAD_DIM)),
        'g_k_cmp': gain(20, (DEPTH, HEAD_DIM)),
        'g_k_slc': gain(21, (DEPTH, HEAD_DIM)),
        'g_k_win': gain(22, (DEPTH, HEAD_DIM)),
        'pe_k_cmp': nrm(23, (DEPTH, L_CMP, HEAD_DIM), 0.1),
        'pe_v_cmp': nrm(24, (DEPTH, L_CMP, HEAD_DIM), 0.1),
        'w_ck1': nrm(25, (DEPTH, L_CMP * HEAD_DIM, CMP_HID), (L_CMP * HEAD_DIM) ** -0.5),
        'w_ck2': nrm(26, (DEPTH, CMP_HID, HEAD_DIM), CMP_HID ** -0.5),
        'w_cv1': nrm(27, (DEPTH, L_CMP * HEAD_DIM, CMP_HID), (L_CMP * HEAD_DIM) ** -0.5),
        'w_cv2': nrm(28, (DEPTH, CMP_HID, HEAD_DIM), CMP_HID ** -0.5),
        'w_out': nrm(29, (DEPTH, MIX_WIDTH, D_MODEL), MIX_WIDTH ** -0.5),
        'w_ffn_in': nrm(30, (DEPTH, D_MODEL, 2 * D_FF), D_MODEL ** -0.5),
        'w_ffn_out': nrm(31, (DEPTH, D_FF, D_MODEL), D_FF ** -0.5),
    }


def reference(x_prompt, x_sample, cache_k_cmp, cache_v_cmp, cache_k_slc, cache_v_slc,
              state_k_win, state_v_win, page_table, c_prompt, c_sample,
              w_ada, b_ada, g_mix_norm, g_ffn_norm, w_in, g_sgu, w_sgu, b_sgu,
              g_q, g_k_cmp, g_k_slc, g_k_win, pe_k_cmp, pe_v_cmp,
              w_ck1, w_ck2, w_cv1, w_cv2, w_out, w_ffn_in, w_ffn_out):
    B, S = x_prompt.shape[:2]
    n_new = x_sample.shape[1]
    pos_p = jnp.arange(S)
    pos_s = PAST_LEN + jnp.arange(n_new)
    wb_p = min(WINDOW, S)
    wb_s = state_k_win.shape[2]
    open_p = S - ((S - 1) // CHUNK) * CHUNK
    xp, xs = x_prompt, x_sample
    sp = [[] for _ in range(7)]
    ss = [[] for _ in range(7)]
    for l in range(DEPTH):
        proj_w = (w_in[l], g_sgu[l], g_q[l], g_k_cmp[l], g_k_slc[l], g_k_win[l])
        cmp_w = (pe_k_cmp[l], pe_v_cmp[l], w_ck1[l], w_ck2[l], w_cv1[l], w_cv2[l])
        ffn_w = (w_out[l], g_ffn_norm[l], w_ffn_in[l], w_ffn_out[l])

        sh1, sc1, gt1, sh2, sc2, gt2 = jnp.split(jax.nn.silu(c_prompt) @ w_ada[l] + b_ada[l], 6, axis=-1)
        h = modulate(xp, g_mix_norm[l], sh1, sc1)
        u, v, q, kc, vc, ks, vs, kw, vw, gates = project(h, pos_p, *proj_w)
        a_out = chunk_mlp(u, v, w_sgu[l], b_sgu[l])
        ctx = nsa_context(kc, vc, ks, vs, *cmp_w)
        b_out = nsa_prompt(q, gates, *ctx, kw, vw)
        xp = finish(xp, jnp.concatenate([a_out, b_out], axis=-1), gt1, sh2, sc2, gt2, *ffn_w)
        for lst, val in zip(sp, (kc, vc, ks, vs, kw[:, S - wb_p:], vw[:, S - wb_p:],
                                 v.reshape(B, S, MIX_A)[:, S - open_p:])):
            lst.append(val)

        sh1, sc1, gt1, sh2, sc2, gt2 = jnp.split(jax.nn.silu(c_sample) @ w_ada[l] + b_ada[l], 6, axis=-1)
        h = modulate(xs, g_mix_norm[l], sh1, sc1)
        u, v, q, kc, vc, ks, vs, kw, vw, gates = project(h, pos_s, *proj_w)
        a_out = chunk_mlp(u, v, w_sgu[l], b_sgu[l])
        past = lambda cache: gather_pages(cache[l], page_table)
        ctx = nsa_context(jnp.concatenate([past(cache_k_cmp), kc], axis=1),
                          jnp.concatenate([past(cache_v_cmp), vc], axis=1),
                          jnp.concatenate([past(cache_k_slc), ks], axis=1),
                          jnp.concatenate([past(cache_v_slc), vs], axis=1), *cmp_w)
        kw_all = jnp.concatenate([state_k_win[l], kw], axis=1)
        vw_all = jnp.concatenate([state_v_win[l], vw], axis=1)
        w_pos = PAST_LEN - wb_s + jnp.arange(wb_s + n_new)
        b_out = nsa_core(q, gates, pos_s, *ctx, kw_all, vw_all, w_pos)
        xs = finish(xs, jnp.concatenate([a_out, b_out], axis=-1), gt1, sh2, sc2, gt2, *ffn_w)
        for lst, val in zip(ss, (kc, vc, ks, vs, kw_all[:, n_new:], vw_all[:, n_new:],
                                 v.reshape(xs.shape[0], n_new, MIX_A))):
            lst.append(val)

    p_k_cmp, p_v_cmp, p_k_slc, p_v_slc, p_k_win, p_v_win, p_chunk_v = [jnp.stack(a) for a in sp]
    s_k_cmp, s_v_cmp, s_k_slc, s_v_slc, s_k_win, s_v_win, s_chunk_v = [jnp.stack(a) for a in ss]
    return (xp, xs, p_k_cmp, p_v_cmp, p_k_slc, p_v_slc, p_k_win, p_v_win, p_chunk_v,
            s_k_cmp, s_v_cmp, s_k_slc, s_v_slc, s_k_win, s_v_win, s_chunk_v)
```

```python
import functools

import jax
import jax.numpy as jnp
from jax import lax
from jax.experimental import pallas as pl
from jax.experimental.pallas import tpu as pltpu

F32 = jnp.float32
BF16 = jnp.bfloat16

CHUNK = 128
A_GROUPS = 8
HEAD_DIM = 64
N_HEADS = 8
N_KV = 4
L_CMP = 32
STRIDE = 16
CMP_HID = 256
L_SLC = 64
N_SEL = 16
WINDOW = 512
ROPE_THETA = 10000.0
EPS = 1e-6
FORCED_SCORE = 1e4
NEG = -1e30
SEL_BIAS = -(2.0 ** 100)

LANES = 128
VMEM_LIMIT = 52 * 1024 * 1024

_NT = (((1,), (1,)), ((), ()))


def _dot(a, b):
    return jnp.dot(a, b, preferred_element_type=F32)


def _dot_nt(a, b):
    return lax.dot_general(a, b, _NT, preferred_element_type=F32)


def _split_dot_left(coef, x):
    hi = x.astype(BF16)
    lo = (x - hi.astype(F32)).astype(BF16)
    return _dot(coef, hi) + _dot(coef, lo)


def _split_dot_right(x, coef):
    hi = x.astype(BF16)
    lo = (x - hi.astype(F32)).astype(BF16)
    return _dot(hi, coef) + _dot(lo, coef)


def _params(sem):
    return pltpu.CompilerParams(dimension_semantics=sem, vmem_limit_bytes=VMEM_LIMIT)


def _ada_kernel(c_ref, w_ref, b_ref, o_ref):
    c = c_ref[...]
    o_ref[...] = _dot(jax.nn.silu(c).astype(BF16), w_ref[...]) + b_ref[...]


def _ada(c, w, b):
    m, k = c.shape
    n = w.shape[1]
    tn = 1024
    return pl.pallas_call(
        _ada_kernel,
        grid=(n // tn,),
        in_specs=[pl.BlockSpec((m, k), lambda j: (0, 0)),
                  pl.BlockSpec((k, tn), lambda j: (0, j)),
                  pl.BlockSpec((1, tn), lambda j: (0, j))],
        out_specs=pl.BlockSpec((m, tn), lambda j: (0, j)),
        out_shape=jax.ShapeDtypeStruct((m, n), F32),
        compiler_params=_params(("parallel",)),
        name="ada",
    )(c, w, b)


def _group_mean_sq(y, g_ref):
    y2 = y * y
    cols = []
    for c in range(y.shape[1] // 256):
        cols.append(_split_dot_right(y2[:, 256 * c:256 * (c + 1)], g_ref[...]))
    return cols[0] if len(cols) == 1 else jnp.concatenate(cols, axis=1)


def _rope(x, cos, sin):
    n = x.shape[1]
    reps = n // LANES
    cos_t = cos if reps == 1 else jnp.concatenate([cos] * reps, axis=1)
    sin_t = sin if reps == 1 else jnp.concatenate([sin] * reps, axis=1)
    lane = lax.broadcasted_iota(jnp.int32, x.shape, 1)
    first_half = (lane & (HEAD_DIM - 1)) < (HEAD_DIM // 2)
    partner = jnp.where(first_half, pltpu.roll(x, n - HEAD_DIM // 2, 1), pltpu.roll(x, HEAD_DIM // 2, 1))
    return x * cos_t + partner * sin_t


def _proj_kernel(x_ref, sh_ref, sc_ref, gmix_ref, w_ref, cos_ref, sin_ref, gsgu_ref, gq_ref, gkc_ref,
                 gks_ref, gkw_ref, wmix_ref, bmix_ref, gmat_ref,
                 a_ref, vn_ref, q_ref, kc_ref, vc_ref, ks_ref, vs_ref, kw_ref, vw_ref, gate_ref):
    x = x_ref[...]
    tm = x.shape[0]
    ms = jnp.mean(x * x, axis=-1, keepdims=True)
    h = x * lax.rsqrt(ms + EPS) * gmix_ref[...]
    h = h * (1.0 + sc_ref[...]) + sh_ref[...]
    hb = h.astype(BF16)
    cos = cos_ref[...]
    sin = sin_ref[...]

    def seg(lo, hi):
        return _dot(hb, w_ref[:, lo:hi])

    def head_norm(y, g_ref):
        return y * lax.rsqrt(_group_mean_sq(y, gmat_ref) + EPS) * g_ref[...]

    u = jax.nn.gelu(seg(0, 512))
    v = jax.nn.gelu(seg(512, 1024))
    vn = head_norm(v, gsgu_ref)
    vn_ref[...] = vn
    vb = vn.astype(BF16)
    lane = lax.broadcasted_iota(jnp.int32, (CHUNK, LANES), 1)
    low = lane < HEAD_DIM
    for ck in range(tm // CHUNK):
        rows = slice(ck * CHUNK, (ck + 1) * CHUNK)
        for pr in range(A_GROUPS // 2):
            cols = slice(pr * LANES, (pr + 1) * LANES)
            vp = vb[rows, cols]
            mixed = jnp.where(low, _dot(wmix_ref[2 * pr], vp), _dot(wmix_ref[2 * pr + 1], vp))
            mixed = mixed + bmix_ref[:, cols]
            a_ref[rows, cols] = (u[rows, cols] * mixed).astype(a_ref.dtype)

    q_ref[...] = _rope(head_norm(seg(1024, 1536), gq_ref), cos, sin)
    kc_ref[...] = _rope(head_norm(seg(1536, 1792), gkc_ref), cos, sin)
    vc_ref[...] = seg(1792, 2048)
    ks_ref[...] = _rope(head_norm(seg(2048, 2304), gks_ref), cos, sin)
    vs_ref[...] = seg(2304, 2560)
    kw_ref[...] = _rope(head_norm(seg(2560, 2816), gkw_ref), cos, sin)
    vw_ref[...] = seg(2816, 3072)
    gate_ref[...] = jax.nn.sigmoid(seg(3072, 3200))


def _proj(x, sh, sc, per_row_mod, rows_per_mod, gmix, w_pad, cos, sin, rope_rows, gsgu, gq, gkc, gks, gkw,
          wmix, bmix, gmat, tm):
    r, d = x.shape
    n_tiles = r // tm
    rope_tiles = rope_rows // tm
    if per_row_mod:
        mod_spec = pl.BlockSpec((tm, d), lambda i: (i, 0))
    else:
        tiles_per_mod = rows_per_mod // tm
        mod_spec = pl.BlockSpec((None, 1, d), lambda i: (i // tiles_per_mod, 0, 0))
    const = lambda shape: pl.BlockSpec(shape, lambda i: (0,) * len(shape))
    row = lambda n: pl.BlockSpec((tm, n), lambda i: (i, 0))
    out_widths = [512, 512, 512, 256, 256, 256, 256, 256, 256, 128]
    out_dtypes = [BF16] + [F32] * 9
    return pl.pallas_call(
        _proj_kernel,
        grid=(n_tiles,),
        in_specs=[row(d), mod_spec, mod_spec, const((1, d)), const(w_pad.shape),
                  pl.BlockSpec((tm, LANES), lambda i: (i % rope_tiles, 0)),
                  pl.BlockSpec((tm, LANES), lambda i: (i % rope_tiles, 0)),
                  const((1, 512)), const((1, 512)), const((1, 256)), const((1, 256)), const((1, 256)),
                  const(wmix.shape), const(bmix.shape), const(gmat.shape)],
        out_specs=[row(n) for n in out_widths],
        out_shape=[jax.ShapeDtypeStruct((r, n), dt) for n, dt in zip(out_widths, out_dtypes)],
        compiler_params=_params(("parallel",)),
        name="proj",
    )(x, sh, sc, gmix, w_pad, cos, sin, gsgu, gq, gkc, gks, gkw, wmix, bmix, gmat)


def _compress_kernel(*refs, n_prefetch, n_seg, seg_rows):
    refs = refs[n_prefetch:]
    n_pair = N_KV // 2
    seg_refs = refs[:n_seg * n_pair]
    w1_ref, w1ab_ref, pe_ref, w2_ref, o_ref, lhs_ref, ab_ref, hid_ref = refs[n_seg * n_pair:]
    cps = seg_rows // STRIDE
    m = n_seg * cps
    lane = lax.broadcasted_iota(jnp.int32, (cps, LANES), 1)
    low = lane < HEAD_DIM
    for s in range(n_seg):
        rows = slice(s * cps, (s + 1) * cps)
        for rr in range(STRIDE // 2):
            for pp in range(n_pair):
                src = seg_refs[s * n_pair + pp]
                p0 = src[pl.ds(2 * rr, cps, stride=STRIDE), :]
                p1 = src[pl.ds(2 * rr + 1, cps, stride=STRIDE), :]
                r0 = pltpu.roll(p0, HEAD_DIM, 1)
                r1 = pltpu.roll(p1, HEAD_DIM, 1)
                g0 = 2 * pp
                base = rr * LANES
                lhs_ref[rows, g0 * 1024 + base:g0 * 1024 + base + LANES] = jnp.where(low, p0, r1)
                lhs_ref[rows, (g0 + 1) * 1024 + base:(g0 + 1) * 1024 + base + LANES] = jnp.where(low, r0, p1)
    pe_term = _dot(jnp.broadcast_to(pe_ref[...], (8, pe_ref.shape[1])).astype(BF16), w1_ref[...])[0:1]
    ab_ref[m:m + 8, :] = jnp.zeros((8, 2 * CMP_HID), F32)
    for g in range(N_KV):
        ab_ref[0:m, :] = _dot(lhs_ref[:, g * 1024:(g + 1) * 1024].astype(BF16), w1ab_ref[...])
        hidden = ab_ref[0:m, 0:CMP_HID] + ab_ref[pl.ds(1, m), CMP_HID:2 * CMP_HID] + pe_term
        hid_ref[:, g * CMP_HID:(g + 1) * CMP_HID] = jax.nn.gelu(hidden).astype(BF16)
    o_ref[...] = _dot(hid_ref[...], w2_ref[...])


def _compress_call(seg_arrays, seg_specs, grid, n_out_blocks, w1, w1ab, pe, w2, seg_rows, prefetch=None):
    n_seg = len(seg_arrays) // (N_KV // 2)
    m = n_seg * seg_rows // STRIDE
    n_out = w2.shape[1]
    nsp = 0 if prefetch is None else 1
    const = lambda shape: pl.BlockSpec(shape, lambda *a: (0,) * len(shape))
    grid_spec = pltpu.PrefetchScalarGridSpec(
        num_scalar_prefetch=nsp,
        grid=grid,
        in_specs=list(seg_specs) + [const(w1.shape), const(w1ab.shape), const(pe.shape), const(w2.shape)],
        out_specs=pl.BlockSpec((None, m, n_out), lambda i, *a: (i, 0, 0)),
        scratch_shapes=[pltpu.VMEM((m, N_KV * 1024), F32),
                        pltpu.VMEM((m + 8, 2 * CMP_HID), F32),
                        pltpu.VMEM((m, N_KV * CMP_HID), BF16)],
    )
    args = ([] if prefetch is None else [prefetch]) + list(seg_arrays) + [w1, w1ab, pe, w2]
    return pl.pallas_call(
        functools.partial(_compress_kernel, n_prefetch=nsp, n_seg=n_seg, seg_rows=seg_rows),
        grid_spec=grid_spec,
        out_shape=jax.ShapeDtypeStruct((n_out_blocks, m, n_out), F32),
        compiler_params=_params(("parallel",)),
        name="compress",
    )(*args)


def _compress_weights(pe, w1, w2, head_stride):
    half = STRIDE * HEAD_DIM
    w1b = w1.astype(BF16)
    w1ab = jnp.concatenate([w1b[:half], w1b[half:]], axis=1)
    w2p = jnp.pad(w2, ((0, 0), (0, head_stride - HEAD_DIM)))
    w2blk = jnp.einsum("gh,kd->gkhd", jnp.eye(N_KV, dtype=F32), w2p)
    w2blk = w2blk.reshape(N_KV * CMP_HID, N_KV * head_stride).astype(BF16)
    return w1b, w1ab, pe.reshape(1, L_CMP * HEAD_DIM), w2blk


def _kth_threshold(score, axis):
    shape = list(score.shape)
    shape[axis] = 1

    def body(_, carry):
        thr, cnt = carry
        cand = jnp.where(score < thr, score, -jnp.inf)
        mx = jnp.max(cand, axis=axis, keepdims=True)
        c = jnp.sum(jnp.where(score >= mx, 1.0, 0.0), axis=axis, keepdims=True)
        upd = cnt < N_SEL
        return jnp.where(upd, mx, thr), jnp.where(upd, c, cnt)

    thr, _ = lax.fori_loop(0, N_SEL, body, (jnp.full(shape, jnp.inf, F32), jnp.zeros(shape, F32)))
    return thr


TK = 512
BIAS_ROWS = 16


def _attn_prompt_kernel(qt_ref, gt_ref, kcc_ref, vcct_ref, ks_ref, vst_ref, kw_ref, vwt_ref, ovt_ref, tri_ref,
                        o_ref, qaug_ref, sel_ref):
    i = pl.program_id(2)
    qs = i * CHUNK
    nq = CHUNK
    qt = qt_ref[...]
    qaug_ref[0:HEAD_DIM, :] = jnp.concatenate([qt[:HEAD_DIM], qt[HEAD_DIM:]], axis=1)
    qaug_ref[HEAD_DIM:, :] = jnp.zeros((LANES - HEAD_DIM, 2 * nq), BF16)

    nc = kcc_ref.shape[0]
    s = _dot(kcc_ref[...], qaug_ref[...])
    c_idx = lax.broadcasted_iota(jnp.int32, (nc, 2 * nq), 0)
    q_idx = lax.broadcasted_iota(jnp.int32, (nc, 2 * nq), 1) & (nq - 1)
    mask = c_idx * STRIDE + (L_CMP - 1) <= qs + q_idx
    s = jnp.where(mask, s, NEG)
    mx = jnp.max(s, axis=0, keepdims=True)
    p = jnp.where(mask, jnp.exp(s - mx), 0.0)
    p = p / jnp.maximum(jnp.sum(p, axis=0, keepdims=True), 1e-20)
    o_c = _dot(vcct_ref[...], p.astype(BF16))

    imp = _split_dot_left(ovt_ref[...], p[:, :nq] + p[:, nq:])
    ns = imp.shape[0]
    n_idx = lax.broadcasted_iota(jnp.int32, (ns, nq), 0)
    cur = (qs + lax.broadcasted_iota(jnp.int32, (ns, nq), 1)) // L_SLC
    valid = n_idx <= cur
    forced = valid & ((n_idx == 0) | (n_idx >= cur - 1))
    score = jnp.where(forced, FORCED_SCORE, jnp.where(valid, imp, -jnp.inf))
    thr = _kth_threshold(score, 0)
    above = score > thr
    tie = score == thr
    need = N_SEL - jnp.sum(jnp.where(above, 1.0, 0.0), axis=0, keepdims=True)
    rank = _dot(tri_ref[...], jnp.where(tie, 1.0, 0.0).astype(BF16))
    sel = above | (tie & (rank <= need))
    sel_ref[...] = jnp.where(sel, 0.0, SEL_BIAS)

    d_idx = (lax.broadcasted_iota(jnp.int32, (TK, 2 * nq), 0)
             - (lax.broadcasted_iota(jnp.int32, (TK, 2 * nq), 1) & (nq - 1)))

    def tile_body(kt, carry):
        m_run, l_run, acc = carry
        k0 = pl.multiple_of(kt * TK, TK)
        b0 = pl.multiple_of((kt // (BIAS_ROWS * L_SLC // TK)) * BIAS_ROWS, BIAS_ROWS)
        bias = sel_ref[pl.ds(b0, BIAS_ROWS), :]
        qaug_ref[HEAD_DIM:HEAD_DIM + BIAS_ROWS, :] = jnp.concatenate([bias, bias], axis=1).astype(BF16)
        st = _dot(ks_ref[pl.ds(k0, TK), :], qaug_ref[...])
        st = jnp.where(d_idx <= qs - k0, st, NEG)
        m_new = jnp.maximum(m_run, jnp.max(st, axis=0, keepdims=True))
        alpha = jnp.exp(m_run - m_new)
        pt = jnp.exp(st - m_new)
        l_new = alpha * l_run + jnp.sum(pt, axis=0, keepdims=True)
        acc_new = alpha * acc + _dot(vst_ref[:, pl.ds(k0, TK)], pt.astype(BF16))
        return m_new, l_new, acc_new

    n_tiles = (qs + nq + TK - 1) // TK
    init = (jnp.full((1, 2 * nq), NEG, F32), jnp.zeros((1, 2 * nq), F32), jnp.zeros((HEAD_DIM, 2 * nq), F32))
    _, l_s, acc_s = lax.fori_loop(0, n_tiles, tile_body, init)
    o_s = acc_s / l_s

    nw = WINDOW + nq
    w0 = pl.multiple_of(qs, CHUNK)
    sw = _dot(kw_ref[pl.ds(w0, nw), :], qaug_ref[...])
    j_idx = lax.broadcasted_iota(jnp.int32, (nw, 2 * nq), 0)
    jq = j_idx - (lax.broadcasted_iota(jnp.int32, (nw, 2 * nq), 1) & (nq - 1))
    wmask = (jq > 0) & (jq <= WINDOW) & (j_idx >= WINDOW - qs)
    sw = jnp.where(wmask, sw, NEG)
    mw = jnp.max(sw, axis=0, keepdims=True)
    pw = jnp.where(wmask, jnp.exp(sw - mw), 0.0)
    lw = jnp.maximum(jnp.sum(pw, axis=0, keepdims=True), 1e-20)
    o_w = _dot(vwt_ref[:, pl.ds(w0, nw)], pw.astype(BF16)) / lw

    g = gt_ref[...]
    for r in range(2):
        cols = slice(r * nq, (r + 1) * nq)
        o_r = (g[3 * r:3 * r + 1] * o_c[:, cols] + g[3 * r + 1:3 * r + 2] * o_s[:, cols]
               + g[3 * r + 2:3 * r + 3] * o_w[:, cols])
        o_ref[r * HEAD_DIM:(r + 1) * HEAD_DIM, :] = o_r.astype(o_ref.dtype)


def _attn_prompt(qt, gt, kcc, vcct, ks_aug, vst, kw_pad, vwt_pad, ovt, tri):
    b, _, _, s = qt.shape
    nqb = s // CHUNK
    nc = kcc.shape[1]
    blk4 = lambda shape: pl.BlockSpec((None, None) + shape, lambda bi, g, i: (bi, g, 0, 0))
    return pl.pallas_call(
        _attn_prompt_kernel,
        grid=(b, N_KV, nqb),
        in_specs=[pl.BlockSpec((None, None, 2 * HEAD_DIM, CHUNK), lambda bi, g, i: (bi, g, 0, i)),
                  pl.BlockSpec((None, None, 8, CHUNK), lambda bi, g, i: (bi, g, 0, i)),
                  pl.BlockSpec((None, nc, LANES), lambda bi, g, i: (bi, 0, g)),
                  blk4((HEAD_DIM, nc)),
                  blk4((s, LANES)),
                  blk4((HEAD_DIM, s)),
                  blk4((s + WINDOW, LANES)),
                  blk4((HEAD_DIM, s + WINDOW)),
                  pl.BlockSpec(ovt.shape, lambda bi, g, i: (0, 0)),
                  pl.BlockSpec(tri.shape, lambda bi, g, i: (0, 0))],
        out_specs=pl.BlockSpec((None, None, 2 * HEAD_DIM, CHUNK), lambda bi, g, i: (bi, g, 0, i)),
        out_shape=jax.ShapeDtypeStruct((b, N_KV, 2 * HEAD_DIM, s), BF16),
        scratch_shapes=[pltpu.VMEM((LANES, 2 * CHUNK), BF16),
                        pltpu.VMEM((s // L_SLC, CHUNK), F32)],
        compiler_params=_params(("parallel", "parallel", "arbitrary")),
        name="attn_prompt",
    )(qt, gt, kcc, vcct, ks_aug, vst, kw_pad, vwt_pad, ovt, tri)


NQ_PAD = 8


def _softmax_rows(s, mask):
    s = jnp.where(mask, s, NEG)
    mx = jnp.max(s, axis=1, keepdims=True)
    p = jnp.where(mask, jnp.exp(s - mx), 0.0)
    return p / jnp.maximum(jnp.sum(p, axis=1, keepdims=True), 1e-20)


def _attn_sample_kernel(pt_ref, qbd_ref, g_ref, kcc_ref, vcc_ref, *refs, n_pages, past_len):
    kpages = refs[:n_pages]
    vpages = refs[n_pages:2 * n_pages]
    (kst_ref, vst_ref, kws_ref, vws_ref, kwt_ref, vwt_ref, ov_ref, triu_ref, eexp_ref,
     o_ref, s_scr) = refs[2 * n_pages:]
    del pt_ref
    qbd = qbd_ref[...]
    nrow = qbd.shape[0]
    page = kpages[0].shape[0]

    def t_of(shape):
        return past_len + (lax.broadcasted_iota(jnp.int32, shape, 0) & (NQ_PAD - 1))

    nc = kcc_ref.shape[0]
    s = _dot_nt(qbd, kcc_ref[...])
    c_idx = lax.broadcasted_iota(jnp.int32, (nrow, nc), 1)
    p_c = _softmax_rows(s, c_idx * STRIDE + (L_CMP - 1) <= t_of((nrow, nc)))
    o_c = _dot(p_c.astype(BF16), vcc_ref[...])

    half = nrow // 2
    imp = _split_dot_right(p_c[:half] + p_c[half:], ov_ref[...])
    ns = imp.shape[1]
    n_idx = lax.broadcasted_iota(jnp.int32, (half, ns), 1)
    cur = t_of((half, ns)) // L_SLC
    valid = n_idx <= cur
    forced = valid & ((n_idx == 0) | (n_idx >= cur - 1))
    score = jnp.where(forced, FORCED_SCORE, jnp.where(valid, imp, -jnp.inf))
    thr = _kth_threshold(score, 1)
    above = score > thr
    tie = score == thr
    need = N_SEL - jnp.sum(jnp.where(above, 1.0, 0.0), axis=1, keepdims=True)
    rank = _dot(jnp.where(tie, 1.0, 0.0).astype(BF16), triu_ref[...])
    sel = jnp.where(above | (tie & (rank <= need)), 1.0, 0.0).astype(BF16)
    sel_keys = _dot(jnp.concatenate([sel, sel], axis=0), eexp_ref[...])

    for pg in range(n_pages):
        s_scr[:, pg * page:(pg + 1) * page] = _dot_nt(qbd, kpages[pg][...].astype(BF16))
    s_scr[:, n_pages * page:(n_pages + 1) * page] = _dot_nt(qbd, kst_ref[...])
    nk = (n_pages + 1) * page
    tok = lax.broadcasted_iota(jnp.int32, (nrow, nk), 1)
    p_s = _softmax_rows(s_scr[...], (sel_keys > 0.5) & (tok <= t_of((nrow, nk))))
    o_s = _dot(p_s[:, n_pages * page:].astype(BF16), vst_ref[...])
    for pg in range(n_pages):
        o_s = o_s + _dot(p_s[:, pg * page:(pg + 1) * page].astype(BF16), vpages[pg][...].astype(BF16))

    wb = kws_ref.shape[0]
    nw = wb + page
    sw = jnp.concatenate([_dot_nt(qbd, kws_ref[...].astype(BF16)), _dot_nt(qbd, kwt_ref[...])], axis=1)
    rel = t_of((nrow, nw)) - (past_len - wb + lax.broadcasted_iota(jnp.int32, (nrow, nw), 1))
    p_w = _softmax_rows(sw, (rel >= 0) & (rel < WINDOW))
    o_w = _dot(p_w[:, :wb].astype(BF16), vws_ref[...].astype(BF16)) + _dot(p_w[:, wb:].astype(BF16), vwt_ref[...])

    g = g_ref[...]
    o = g[:, 0:1] * o_c + g[:, 1:2] * o_s + g[:, 2:3] * o_w
    row_g = (lax.broadcasted_iota(jnp.int32, o.shape, 0) // NQ_PAD) & (N_KV - 1)
    lane_g = lax.broadcasted_iota(jnp.int32, o.shape, 1) // HEAD_DIM
    o = jnp.where(row_g == lane_g, o, 0.0)
    per_r = N_KV * NQ_PAD
    for r in range(2):
        acc = o[r * per_r:r * per_r + NQ_PAD]
        for gg in range(1, N_KV):
            acc = acc + o[r * per_r + gg * NQ_PAD:r * per_r + (gg + 1) * NQ_PAD]
        o_ref[r] = acc


def _attn_sample(page_table, qbd, gsm, kcc, vcc, cache_k, cache_v, k_tail, v_tail, kw_state, vw_state,
                 kw_tail, vw_tail, ov, triu, eexp, past_len):
    nb, n_pages = page_table.shape
    page = cache_k.shape[1]
    kvw = cache_k.shape[2]
    nrow = qbd.shape[1]
    per_b = lambda shape: pl.BlockSpec((None,) + shape, lambda b, pt: (b,) + (0,) * len(shape))
    const = lambda shape: pl.BlockSpec(shape, lambda b, pt: (0,) * len(shape))

    def page_spec(pg):
        return pl.BlockSpec((None, page, kvw), lambda b, pt: (pt[b, pg], 0, 0))

    in_specs = ([per_b((nrow, kvw)), per_b((nrow, 8)), per_b(kcc.shape[1:]), per_b(vcc.shape[1:])]
                + [page_spec(pg) for pg in range(n_pages)] * 1
                + [page_spec(pg) for pg in range(n_pages)]
                + [per_b((page, kvw)), per_b((page, kvw)), per_b(kw_state.shape[1:]), per_b(vw_state.shape[1:]),
                   per_b((page, kvw)), per_b((page, kvw)), const(ov.shape), const(triu.shape), const(eexp.shape)])
    grid_spec = pltpu.PrefetchScalarGridSpec(
        num_scalar_prefetch=1,
        grid=(nb,),
        in_specs=in_specs,
        out_specs=pl.BlockSpec((None, 2, NQ_PAD, kvw), lambda b, pt: (b, 0, 0, 0)),
        scratch_shapes=[pltpu.VMEM((nrow, (n_pages + 1) * page), F32)],
    )
    return pl.pallas_call(
        functools.partial(_attn_sample_kernel, n_pages=n_pages, past_len=past_len),
        grid_spec=grid_spec,
        out_shape=jax.ShapeDtypeStruct((nb, 2, NQ_PAD, kvw), F32),
        compiler_params=_params(("parallel",)),
        name="attn_sample",
    )(page_table, qbd, gsm, kcc, vcc, *([cache_k] * n_pages), *([cache_v] * n_pages),
      k_tail, v_tail, kw_state, vw_state, kw_tail, vw_tail, ov, triu, eexp)


FF_SPLIT = 2


def _finish_kernel(x_ref, a_ref, b_ref, gt1_ref, sh2_ref, sc2_ref, gt2_ref, wout_ref, gffn_ref, win_ref, wo2_ref,
                   o_ref):
    half = a_ref.shape[1]
    y = _dot(a_ref[...], wout_ref[0:half, :]) + _dot(b_ref[...], wout_ref[half:, :])
    x1 = x_ref[...] + gt1_ref[...] * y
    ms = jnp.mean(x1 * x1, axis=-1, keepdims=True)
    h = x1 * lax.rsqrt(ms + EPS) * gffn_ref[...]
    hb = (h * (1.0 + sc2_ref[...]) + sh2_ref[...]).astype(BF16)
    d_ff = wo2_ref.shape[0]
    step = d_ff // FF_SPLIT
    acc = None
    for c in range(FF_SPLIT):
        up = _dot(hb, win_ref[:, c * step:(c + 1) * step])
        gate = _dot(hb, win_ref[:, d_ff + c * step:d_ff + (c + 1) * step])
        z = (jax.nn.silu(up) * gate).astype(BF16)
        part = _dot(z, wo2_ref[c * step:(c + 1) * step, :])
        acc = part if acc is None else acc + part
    o_ref[...] = x1 + gt2_ref[...] * acc


def _finish(x, a, b, mods, per_row_mod, rows_per_mod, wout, gffn, win, wo2, tm):
    r, d = x.shape
    if per_row_mod:
        mod_spec = pl.BlockSpec((tm, d), lambda i: (i, 0))
    else:
        tiles_per_mod = rows_per_mod // tm
        mod_spec = pl.BlockSpec((None, 1, d), lambda i: (i // tiles_per_mod, 0, 0))
    single = lambda shape: pl.BlockSpec(shape, lambda i: (0,) * len(shape), pipeline_mode=pl.Buffered(1))
    row = lambda n: pl.BlockSpec((tm, n), lambda i: (i, 0))
    return pl.pallas_call(
        _finish_kernel,
        grid=(r // tm,),
        in_specs=[row(d), row(a.shape[1]), row(b.shape[1]), mod_spec, mod_spec, mod_spec, mod_spec,
                  single(wout.shape), single(gffn.shape), single(win.shape), single(wo2.shape)],
        out_specs=row(d),
        out_shape=jax.ShapeDtypeStruct((r, d), F32),
        compiler_params=_params(("parallel",)),
        name="finish",
    )(x, a, b, *mods, wout, gffn, win, wo2)


def _rope_tables(pos):
    half = HEAD_DIM // 2
    inv = ROPE_THETA ** (-jnp.arange(half, dtype=F32) / half)
    ang = pos.astype(F32)[:, None] * inv[None, :]
    cos, sin = jnp.cos(ang), jnp.sin(ang)
    return jnp.concatenate([cos] * 4, axis=1), jnp.concatenate([-sin, sin, -sin, sin], axis=1)


def _overlap(n_c, n_s):
    c_start = jnp.arange(n_c) * STRIDE
    blk = jnp.arange(n_s)
    return ((c_start[:, None] < (blk[None, :] + 1) * L_SLC)
            & (c_start[:, None] + L_CMP > blk[None, :] * L_SLC)).astype(BF16)


def _tail_page(new_rows, page):
    return jnp.pad(new_rows, ((0, 0), (0, page - new_rows.shape[1]), (0, 0))).astype(BF16)


def kernel(x_prompt, x_sample, cache_k_cmp, cache_v_cmp, cache_k_slc, cache_v_slc, state_k_win, state_v_win,
           page_table, c_prompt, c_sample, w_ada, b_ada, g_mix_norm, g_ffn_norm, w_in, g_sgu, w_sgu, b_sgu,
           g_q, g_k_cmp, g_k_slc, g_k_win, pe_k_cmp, pe_v_cmp, w_ck1, w_ck2, w_cv1, w_cv2, w_out, w_ffn_in,
           w_ffn_out):
    depth = w_in.shape[0]
    assert depth == 1, "single trunk layer"
    nb_p, seq, d = x_prompt.shape
    nb_s, n_new, _ = x_sample.shape
    n_pool, page = cache_k_cmp.shape[1], cache_k_cmp.shape[2]
    kvw = N_KV * HEAD_DIM
    n_pages = page_table.shape[1]
    past_len = n_pages * page
    wb_s = state_k_win.shape[2]
    l = 0

    in_cols = w_in.shape[2]
    w_in_pad = jnp.pad(w_in[l], ((0, 0), (0, 3200 - in_cols))).astype(BF16)
    tile_gain = lambda g, heads: jnp.tile(g, heads).reshape(1, heads * HEAD_DIM)
    gq_t, gkc_t = tile_gain(g_q[l], N_HEADS), tile_gain(g_k_cmp[l], N_KV)
    gks_t, gkw_t = tile_gain(g_k_slc[l], N_KV), tile_gain(g_k_win[l], N_KV)
    gsgu = g_sgu[l].reshape(1, -1)
    gmix = g_mix_norm[l].reshape(1, d)
    gffn = g_ffn_norm[l].reshape(1, d)
    gmat = (jnp.kron(jnp.eye(256 // HEAD_DIM, dtype=F32), jnp.ones((HEAD_DIM, HEAD_DIM), F32)) / HEAD_DIM).astype(BF16)
    w_tril = jnp.where(jnp.tril(jnp.ones((CHUNK, CHUNK), bool)), w_sgu[l], 0)
    wmix_p = w_tril.astype(BF16)
    bmix_p = jnp.repeat(b_sgu[l].T, HEAD_DIM, axis=1)
    eye_s = jnp.eye(CHUNK // n_new, dtype=F32)
    wmix_s = jax.vmap(lambda w: jnp.kron(eye_s, w[:n_new, :n_new]))(w_tril).astype(BF16)
    bmix_s = jnp.tile(jnp.repeat(b_sgu[l].T[:n_new], HEAD_DIM, axis=1), (CHUNK // n_new, 1))
    wout_b = w_out[l].astype(BF16)
    win_b = w_ffn_in[l].astype(BF16)
    wo2_b = w_ffn_out[l].astype(BF16)

    n_c = nb_p + nb_s
    n_c_pad = -(-n_c // 8) * 8
    c_all = jnp.pad(jnp.concatenate([c_prompt, c_sample], axis=0), ((0, n_c_pad - n_c), (0, 0)))
    ada = _ada(c_all, w_ada[l].astype(BF16), b_ada[l].reshape(1, -1))
    mods_p = [m.reshape(nb_p, 1, d) for m in jnp.split(ada[:nb_p], 6, axis=-1)]
    mods_s = [jnp.repeat(m, n_new, axis=0) for m in jnp.split(ada[nb_p:n_c], 6, axis=-1)]

    cos_p, sin_p = _rope_tables(jnp.arange(seq))
    cos_s, sin_s = _rope_tables(past_len + jnp.arange(n_new))
    reps = CHUNK // n_new
    cos_s, sin_s = jnp.tile(cos_s, (reps, 1)), jnp.tile(sin_s, (reps, 1))
    xp = x_prompt.reshape(nb_p * seq, d)
    xs = x_sample.reshape(nb_s * n_new, d)
    common = (gsgu, gq_t, gkc_t, gks_t, gkw_t)
    (a_p, vn_p, q_p, kc_p, vc_p, ks_p, vs_p, kw_p, vw_p, gate_p) = _proj(
        xp, mods_p[0], mods_p[1], False, seq, gmix, w_in_pad, cos_p, sin_p, seq, *common, wmix_p, bmix_p, gmat, 256)
    (a_s, vn_s, q_s, kc_s, vc_s, ks_s, vs_s, kw_s, vw_s, gate_s) = _proj(
        xs, mods_s[0], mods_s[1], True, 0, gmix, w_in_pad, cos_s, sin_s, CHUNK, *common, wmix_s, bmix_s, gmat, CHUNK)

    n_chunk_p = seq // STRIDE
    wk_p = _compress_weights(pe_k_cmp[l], w_ck1[l], w_ck2[l], LANES)
    wv_p = _compress_weights(pe_v_cmp[l], w_cv1[l], w_cv2[l], LANES)
    n_pair = N_KV // 2
    seg_p = [pl.BlockSpec((None, seq, LANES), functools.partial(lambda pp, i: (i, 0, pp), pp))
             for pp in range(n_pair)]
    kcc_p = _compress_call([kc_p.reshape(nb_p, seq, kvw)] * n_pair, seg_p, (nb_p,), nb_p, *wk_p, seq)
    vcc_p = _compress_call([vc_p.reshape(nb_p, seq, kvw)] * n_pair, seg_p, (nb_p,), nb_p, *wv_p, seq)
    kcc_p = kcc_p.astype(BF16)
    vcct_p = vcc_p.reshape(nb_p, n_chunk_p, N_KV, LANES)[..., :HEAD_DIM].transpose(0, 2, 3, 1).astype(BF16)

    heads_t = lambda a, n: a.reshape(nb_p, seq, n, HEAD_DIM)
    qt = (heads_t(q_p, N_HEADS) * (HEAD_DIM ** -0.5)).reshape(nb_p, seq, N_KV, 2 * HEAD_DIM)
    qt = qt.transpose(0, 2, 3, 1).astype(BF16)
    gt = gate_p[:, :3 * N_HEADS].reshape(nb_p, seq, N_KV, 6)
    gt = jnp.pad(gt, ((0, 0), (0, 0), (0, 0), (0, 2))).transpose(0, 2, 3, 1)
    onehot = jax.nn.one_hot((jnp.arange(seq) // L_SLC) % BIAS_ROWS, LANES - HEAD_DIM, dtype=BF16)
    ks_h = heads_t(ks_p, N_KV).transpose(0, 2, 1, 3).astype(BF16)
    ks_aug = jnp.concatenate([ks_h, jnp.broadcast_to(onehot, (nb_p, N_KV, seq, LANES - HEAD_DIM))], axis=-1)
    vst = heads_t(vs_p, N_KV).transpose(0, 2, 3, 1).astype(BF16)
    kw_h = heads_t(kw_p, N_KV).transpose(0, 2, 1, 3).astype(BF16)
    kw_pad = jnp.pad(kw_h, ((0, 0), (0, 0), (WINDOW, 0), (0, LANES - HEAD_DIM)))
    vwt_pad = jnp.pad(heads_t(vw_p, N_KV).transpose(0, 2, 3, 1).astype(BF16), ((0, 0), (0, 0), (0, 0), (WINDOW, 0)))
    ovt = _overlap(n_chunk_p, seq // L_SLC).T
    tri = jnp.tril(jnp.ones((seq // L_SLC, seq // L_SLC), BF16))
    ot = _attn_prompt(qt, gt, kcc_p, vcct_p, ks_aug, vst, kw_pad, vwt_pad, ovt, tri)
    b_p = ot.transpose(0, 3, 1, 2).reshape(nb_p * seq, N_HEADS * HEAD_DIM)

    y_p = _finish(xp, a_p, b_p, mods_p[2:], False, seq, wout_b, gffn, win_b, wo2_b, 256)

    ck = cache_k_cmp[l].reshape(n_pool, page, kvw)
    cv = cache_v_cmp[l].reshape(n_pool, page, kvw)
    cks = cache_k_slc[l].reshape(n_pool, page, kvw)
    cvs = cache_v_slc[l].reshape(n_pool, page, kvw)
    wk_s = _compress_weights(pe_k_cmp[l], w_ck1[l], w_ck2[l], HEAD_DIM)
    wv_s = _compress_weights(pe_v_cmp[l], w_cv1[l], w_cv2[l], HEAD_DIM)

    def seg_spec(pg, pp):
        return pl.BlockSpec((None, page, LANES), lambda i, pt: (pt[i, pg], 0, pp))

    seg_s = [seg_spec(pg, pp) for pg in range(n_pages) for pp in range(n_pair)]
    n_in = n_pages * n_pair
    kcc_s = _compress_call([ck] * n_in, seg_s, (nb_s,), nb_s, *wk_s, page, prefetch=page_table).astype(BF16)
    vcc_s = _compress_call([cv] * n_in, seg_s, (nb_s,), nb_s, *wv_s, page, prefetch=page_table).astype(BF16)

    q5 = (q_s * (HEAD_DIM ** -0.5)).reshape(nb_s, n_new, N_KV, 2, HEAD_DIM).transpose(0, 3, 2, 1, 4)
    q5 = jnp.pad(q5, ((0, 0), (0, 0), (0, 0), (0, NQ_PAD - n_new), (0, 0)))
    qbd = jnp.einsum("brgqd,gh->brgqhd", q5, jnp.eye(N_KV, dtype=F32))
    qbd = qbd.reshape(nb_s, 2 * N_KV * NQ_PAD, kvw).astype(BF16)
    g5 = gate_s[:, :3 * N_HEADS].reshape(nb_s, n_new, N_KV, 2, 3).transpose(0, 3, 2, 1, 4)
    g5 = jnp.pad(g5, ((0, 0), (0, 0), (0, 0), (0, NQ_PAD - n_new), (0, 5)))
    gsm = g5.reshape(nb_s, 2 * N_KV * NQ_PAD, 8)
    new = lambda a: a.reshape(nb_s, n_new, kvw)
    n_keys = (n_pages + 1) * page
    n_c_s = kcc_s.shape[1]
    ov_s = _overlap(n_c_s, LANES)
    triu = jnp.triu(jnp.ones((LANES, LANES), BF16))
    eexp = (jnp.arange(n_keys)[None, :] // L_SLC == jnp.arange(LANES)[:, None]).astype(BF16)
    os_ = _attn_sample(page_table, qbd, gsm, kcc_s, vcc_s, cks, cvs, _tail_page(new(ks_s), page),
                       _tail_page(new(vs_s), page), state_k_win[l].reshape(nb_s, wb_s, kvw),
                       state_v_win[l].reshape(nb_s, wb_s, kvw), _tail_page(new(kw_s), page),
                       _tail_page(new(vw_s), page), ov_s, triu, eexp, past_len)
    b_s = os_[:, :, :n_new].reshape(nb_s, 2, n_new, N_KV, HEAD_DIM).transpose(0, 2, 3, 1, 4)
    b_s = b_s.reshape(nb_s * n_new, N_HEADS * HEAD_DIM).astype(BF16)

    y_s = _finish(xs, a_s, b_s, mods_s[2:], True, 0, wout_b, gffn, win_b, wo2_b, CHUNK)

    wb_p = min(WINDOW, seq)
    open_p = seq - ((seq - 1) // CHUNK) * CHUNK
    kv5 = lambda a, nb, t: a.reshape(1, nb, t, N_KV, HEAD_DIM)
    outs_p = [kv5(kc_p, nb_p, seq), kv5(vc_p, nb_p, seq), kv5(ks_p, nb_p, seq), kv5(vs_p, nb_p, seq),
              kv5(kw_p, nb_p, seq)[:, :, seq - wb_p:], kv5(vw_p, nb_p, seq)[:, :, seq - wb_p:],
              vn_p.reshape(1, nb_p, seq, -1)[:, :, seq - open_p:]]
    kw_all = jnp.concatenate([state_k_win[l], kv5(kw_s, nb_s, n_new)[0]], axis=1)
    vw_all = jnp.concatenate([state_v_win[l], kv5(vw_s, nb_s, n_new)[0]], axis=1)
    outs_s = [kv5(kc_s, nb_s, n_new), kv5(vc_s, nb_s, n_new), kv5(ks_s, nb_s, n_new), kv5(vs_s, nb_s, n_new),
              kw_all[None, :, n_new:], vw_all[None, :, n_new:], vn_s.reshape(1, nb_s, n_new, -1)]
    return (y_p.reshape(nb_p, seq, d), y_s.reshape(nb_s, n_new, d), *outs_p, *outs_s)
```

```python
import functools
import math

import jax
import jax.numpy as jnp
from jax import lax
from jax.experimental import pallas as pl
from jax.experimental.pallas import tpu as pltpu

F32 = jnp.float32
BF16 = jnp.bfloat16

CHUNK = 128
A_GROUPS = 8
HEAD_DIM = 64
N_HEADS = 8
N_KV = 4
L_CMP = 32
STRIDE = 16
CMP_HID = 256
L_SLC = 64
N_SEL = 16
WINDOW = 512
ROPE_THETA = 10000.0
EPS = 1e-6
FORCED_SCORE = 1e4
NEG = -1e30
SEL_BIAS = -(2.0 ** 100)

LANES = 128
V_ROWS = HEAD_DIM + 16
VMEM_LIMIT = 52 * 1024 * 1024

_NT = (((1,), (1,)), ((), ()))


def _dot(a, b):
    return jnp.dot(a, b, preferred_element_type=F32)


def _dot_nt(a, b):
    return lax.dot_general(a, b, _NT, preferred_element_type=F32)


def _split_dot_left(coef, x):
    hi = x.astype(BF16)
    lo = (x - hi.astype(F32)).astype(BF16)
    return _dot(coef, hi) + _dot(coef, lo)


def _split_dot_right(x, coef):
    hi = x.astype(BF16)
    lo = (x - hi.astype(F32)).astype(BF16)
    return _dot(hi, coef) + _dot(lo, coef)


def _params(sem):
    return pltpu.CompilerParams(dimension_semantics=sem, vmem_limit_bytes=VMEM_LIMIT)


def _with_ones_row(vt):
    n = vt.shape[1]
    row = lax.broadcasted_iota(jnp.int32, (V_ROWS - HEAD_DIM, n), 0)
    return jnp.concatenate([vt, jnp.where(row == 0, 1.0, 0.0).astype(vt.dtype)], axis=0)


def _ada_kernel(c_ref, w_ref, b_ref, o_ref):
    c = c_ref[...]
    o_ref[...] = _dot(jax.nn.silu(c).astype(BF16), w_ref[...]) + b_ref[...]


def _ada(c, w, b):
    m, k = c.shape
    n = w.shape[1]
    tn = 1024
    return pl.pallas_call(
        _ada_kernel,
        grid=(n // tn,),
        in_specs=[pl.BlockSpec((m, k), lambda j: (0, 0)),
                  pl.BlockSpec((k, tn), lambda j: (0, j)),
                  pl.BlockSpec((1, tn), lambda j: (0, j))],
        out_specs=pl.BlockSpec((m, tn), lambda j: (0, j)),
        out_shape=jax.ShapeDtypeStruct((m, n), F32),
        compiler_params=_params(("parallel",)),
        name="ada",
    )(c, w, b)


def _group_mean_sq(y, g_ref):
    y2 = y * y
    cols = []
    for c in range(y.shape[1] // 256):
        cols.append(_split_dot_right(y2[:, 256 * c:256 * (c + 1)], g_ref[...]))
    return cols[0] if len(cols) == 1 else jnp.concatenate(cols, axis=1)


def _rope(x, cos, sin):
    n = x.shape[1]
    reps = n // LANES
    cos_t = cos if reps == 1 else jnp.concatenate([cos] * reps, axis=1)
    sin_t = sin if reps == 1 else jnp.concatenate([sin] * reps, axis=1)
    lane = lax.broadcasted_iota(jnp.int32, x.shape, 1)
    first_half = (lane & (HEAD_DIM - 1)) < (HEAD_DIM // 2)
    partner = jnp.where(first_half, pltpu.roll(x, n - HEAD_DIM // 2, 1), pltpu.roll(x, HEAD_DIM // 2, 1))
    return x * cos_t + partner * sin_t


def _proj_common(x_ref, sh_ref, sc_ref, gmix_ref, w_ref, cos_ref, sin_ref, gsgu_ref, gq_ref, gkc_ref, gks_ref,
                 gkw_ref, wmix_ref, bmix_ref, gmat_ref, a_ref, vn_ref):
    x = x_ref[...]
    tm = x.shape[0]
    ms = jnp.mean(x * x, axis=-1, keepdims=True)
    h = x * lax.rsqrt(ms + EPS) * gmix_ref[...]
    h = h * (1.0 + sc_ref[...]) + sh_ref[...]
    hb = h.astype(BF16)
    cos = cos_ref[...]
    sin = sin_ref[...]

    def seg(lo, hi):
        return _dot(hb, w_ref[:, lo:hi])

    def head_norm(y, g_ref):
        return y * lax.rsqrt(_group_mean_sq(y, gmat_ref) + EPS) * g_ref[...]

    u = jax.nn.gelu(seg(0, 512))
    v = jax.nn.gelu(seg(512, 1024))
    vn = head_norm(v, gsgu_ref)
    vn_ref[...] = vn[tm - CHUNK:]
    vb = vn.astype(BF16)
    lane = lax.broadcasted_iota(jnp.int32, (CHUNK, LANES), 1)
    low = lane < HEAD_DIM
    for ck in range(tm // CHUNK):
        rows = slice(ck * CHUNK, (ck + 1) * CHUNK)
        for pr in range(A_GROUPS // 2):
            cols = slice(pr * LANES, (pr + 1) * LANES)
            vp = vb[rows, cols]
            mixed = jnp.where(low, _dot(wmix_ref[2 * pr], vp), _dot(wmix_ref[2 * pr + 1], vp))
            mixed = mixed + bmix_ref[:, cols]
            a_ref[rows, cols] = (u[rows, cols] * mixed).astype(a_ref.dtype)

    q = _rope(head_norm(seg(1024, 1536), gq_ref), cos, sin)
    kc = _rope(head_norm(seg(1536, 1792), gkc_ref), cos, sin)
    vc = seg(1792, 2048)
    ks = _rope(head_norm(seg(2048, 2304), gks_ref), cos, sin)
    vs = seg(2304, 2560)
    kw = _rope(head_norm(seg(2560, 2816), gkw_ref), cos, sin)
    vw = seg(2816, 3072)
    gates = jax.nn.sigmoid(seg(3072, 3200))
    return q, kc, vc, ks, vs, kw, vw, gates


def _proj_rows_kernel(*refs):
    ins, (a_ref, vn_ref, q_ref, kc_ref, vc_ref, ks_ref, vs_ref, kw_ref, vw_ref, gate_ref) = refs[:15], refs[15:]
    outs = _proj_common(*ins, a_ref, vn_ref)
    for ref, val in zip((q_ref, kc_ref, vc_ref, ks_ref, vs_ref, kw_ref, vw_ref, gate_ref), outs):
        ref[...] = val


def _proj_cols_kernel(*refs, q_scale):
    ins, oh_ref = refs[:15], refs[15]
    (a_ref, vn_ref, qt_ref, gt_ref, kc_ref, vc_ref, kct_ref, vct_ref, kst_ref, vst_ref, kwt_ref, vwt_ref,
     ksa_ref, vsa_ref, kwa_ref, vwa_ref, stage_ref) = refs[16:]
    q, kc, vc, ks, vs, kw, vw, gates = _proj_common(*ins, a_ref, vn_ref)
    tm = q.shape[0]
    qt_ref[...] = (q * q_scale).T.astype(BF16)
    gt_ref[...] = gates.T[0:gt_ref.shape[0]]
    kc_ref[...] = kc
    vc_ref[...] = vc
    kct_ref[...] = kc.T
    vct_ref[...] = vc_ref[...].T
    kst_ref[...] = ks.T
    kwt_ref[...] = kw.T
    stage_ref[0] = vs
    stage_ref[1] = vw
    vst = stage_ref[0].T
    vwt = stage_ref[1].T
    vst_ref[...] = vst
    vwt_ref[...] = vwt
    low = lax.broadcasted_iota(jnp.int32, (tm, LANES), 1) < HEAD_DIM
    onehot = oh_ref[...]
    for g in range(N_KV):
        pair = slice((g // 2) * LANES, (g // 2 + 1) * LANES)
        ks_g, kw_g = ks[:, pair], kw[:, pair]
        if g % 2 == 1:
            ks_g, kw_g = pltpu.roll(ks_g, HEAD_DIM, 1), pltpu.roll(kw_g, HEAD_DIM, 1)
        ksa_ref[g] = jnp.where(low, ks_g, onehot).astype(BF16)
        kwa_ref[g] = jnp.where(low, kw_g, 0.0).astype(BF16)
        rows = slice(g * HEAD_DIM, (g + 1) * HEAD_DIM)
        vsa_ref[g] = _with_ones_row(vst[rows]).astype(BF16)
        vwa_ref[g] = _with_ones_row(vwt[rows]).astype(BF16)


def _proj_in_specs(tm, d, per_row_mod, rows_per_mod, rope_tiles, w_pad, wmix, bmix, gmat):
    if per_row_mod:
        mod_spec = pl.BlockSpec((tm, d), lambda i: (i, 0))
    else:
        tiles_per_mod = rows_per_mod // tm
        mod_spec = pl.BlockSpec((None, 1, d), lambda i: (i // tiles_per_mod, 0, 0))
    const = lambda shape: pl.BlockSpec(shape, lambda i: (0,) * len(shape))
    rope_spec = pl.BlockSpec((tm, LANES), lambda i: (i % rope_tiles, 0))
    return [pl.BlockSpec((tm, d), lambda i: (i, 0)), mod_spec, mod_spec, const((1, d)), const(w_pad.shape),
            rope_spec, rope_spec,
            const((1, 512)), const((1, 512)), const((1, 256)), const((1, 256)), const((1, 256)),
            const(wmix.shape), const(bmix.shape), const(gmat.shape)]


def _proj_rows(x, sh, sc, gmix, w_pad, cos, sin, gsgu, gq, gkc, gks, gkw, wmix, bmix, gmat, tm):
    r, d = x.shape
    row = lambda n: pl.BlockSpec((tm, n), lambda i: (i, 0))
    out_widths = [512, 512, 512, 256, 256, 256, 256, 256, 256, 128]
    out_dtypes = [BF16] + [F32] * 9
    return pl.pallas_call(
        _proj_rows_kernel,
        grid=(r // tm,),
        in_specs=_proj_in_specs(tm, d, True, 0, cos.shape[0] // tm, w_pad, wmix, bmix, gmat),
        out_specs=[row(n) for n in out_widths],
        out_shape=[jax.ShapeDtypeStruct((r, n), dt) for n, dt in zip(out_widths, out_dtypes)],
        compiler_params=_params(("parallel",)),
        name="proj_rows",
    )(x, sh, sc, gmix, w_pad, cos, sin, gsgu, gq, gkc, gks, gkw, wmix, bmix, gmat)


def _proj_cols(x, nb, seq, sh, sc, gmix, w_pad, cos, sin, onehot, gsgu, gq, gkc, gks, gkw, wmix, bmix, gmat, tm,
               q_scale):
    r, d = x.shape
    tpb = seq // tm
    kvw = N_KV * HEAD_DIM
    bi = lambda i: i // tpb
    ti = lambda i: i % tpb
    row = lambda n: pl.BlockSpec((tm, n), lambda i: (i, 0))
    colt = lambda n: pl.BlockSpec((None, n, tm), lambda i: (bi(i), 0, ti(i)))
    out_specs = [row(512),
                 pl.BlockSpec((None, CHUNK, 512), lambda i: (bi(i), 0, 0)),
                 colt(512), colt(32), row(kvw), row(kvw),
                 colt(kvw), colt(kvw), colt(kvw), colt(kvw), colt(kvw), colt(kvw),
                 pl.BlockSpec((None, N_KV, tm, LANES), lambda i: (bi(i), 0, ti(i), 0)),
                 pl.BlockSpec((None, N_KV, V_ROWS, tm), lambda i: (bi(i), 0, 0, ti(i))),
                 pl.BlockSpec((None, N_KV, tm, LANES), lambda i: (bi(i), 0, ti(i), 0)),
                 pl.BlockSpec((None, N_KV, V_ROWS, tm), lambda i: (bi(i), 0, 0, ti(i)))]
    sds = jax.ShapeDtypeStruct
    out_shape = [sds((r, 512), BF16), sds((nb, CHUNK, 512), F32), sds((nb, 512, seq), BF16), sds((nb, 32, seq), F32),
                 sds((r, kvw), F32), sds((r, kvw), F32)] + [sds((nb, kvw, seq), F32)] * 6 + [
                 sds((nb, N_KV, seq, LANES), BF16), sds((nb, N_KV, V_ROWS, seq), BF16),
                 sds((nb, N_KV, seq, LANES), BF16), sds((nb, N_KV, V_ROWS, seq), BF16)]
    in_specs = _proj_in_specs(tm, d, False, seq, tpb, w_pad, wmix, bmix, gmat)
    in_specs.append(pl.BlockSpec((tm, LANES), lambda i: (ti(i), 0)))
    return pl.pallas_call(
        functools.partial(_proj_cols_kernel, q_scale=q_scale),
        grid=(r // tm,),
        in_specs=in_specs,
        out_specs=out_specs,
        out_shape=out_shape,
        scratch_shapes=[pltpu.VMEM((2, tm, kvw), F32)],
        compiler_params=_params(("arbitrary",)),
        name="proj_cols",
    )(x, sh, sc, gmix, w_pad, cos, sin, gsgu, gq, gkc, gks, gkw, wmix, bmix, gmat, onehot)


def _compress_kernel(*refs, n_prefetch, n_seg, seg_rows, transposed_out):
    refs = refs[n_prefetch:]
    n_pair = N_KV // 2
    seg_refs = refs[:n_seg * n_pair]
    w1_ref, w1ab_ref, pe_ref, w2_ref, o_ref, lhs_ref, ab_ref, hid_ref = refs[n_seg * n_pair:]
    cps = seg_rows // STRIDE
    m = n_seg * cps
    lane = lax.broadcasted_iota(jnp.int32, (cps, LANES), 1)
    low = lane < HEAD_DIM
    for s in range(n_seg):
        rows = slice(s * cps, (s + 1) * cps)
        for rr in range(STRIDE // 2):
            for pp in range(n_pair):
                src = seg_refs[s * n_pair + pp]
                p0 = src[pl.ds(2 * rr, cps, stride=STRIDE), :]
                p1 = src[pl.ds(2 * rr + 1, cps, stride=STRIDE), :]
                r0 = pltpu.roll(p0, HEAD_DIM, 1)
                r1 = pltpu.roll(p1, HEAD_DIM, 1)
                g0 = 2 * pp
                base = rr * LANES
                lhs_ref[rows, g0 * 1024 + base:g0 * 1024 + base + LANES] = jnp.where(low, p0, r1)
                lhs_ref[rows, (g0 + 1) * 1024 + base:(g0 + 1) * 1024 + base + LANES] = jnp.where(low, r0, p1)
    pe_term = _dot(jnp.broadcast_to(pe_ref[...], (8, pe_ref.shape[1])).astype(BF16), w1_ref[...])[0:1]
    ab_ref[m:m + 8, :] = jnp.zeros((8, 2 * CMP_HID), F32)
    for g in range(N_KV):
        ab_ref[0:m, :] = _dot(lhs_ref[:, g * 1024:(g + 1) * 1024].astype(BF16), w1ab_ref[...])
        hidden = ab_ref[0:m, 0:CMP_HID] + ab_ref[pl.ds(1, m), CMP_HID:2 * CMP_HID] + pe_term
        hid_ref[:, g * CMP_HID:(g + 1) * CMP_HID] = jax.nn.gelu(hidden).astype(BF16)
    out = _dot(hid_ref[...], w2_ref[...])
    if transposed_out:
        ab_ref[0:m, :] = out
        out_t = ab_ref[0:m, :].T
        for g in range(N_KV):
            o_ref[g] = _with_ones_row(out_t[g * LANES:g * LANES + HEAD_DIM]).astype(o_ref.dtype)
    else:
        o_ref[...] = out.astype(o_ref.dtype)


def _compress_call(seg_arrays, seg_specs, grid, n_out_blocks, w1, w1ab, pe, w2, seg_rows, prefetch=None,
                   transposed_out=False):
    n_seg = len(seg_arrays) // (N_KV // 2)
    m = n_seg * seg_rows // STRIDE
    n_out = w2.shape[1]
    nsp = 0 if prefetch is None else 1
    const = lambda shape: pl.BlockSpec(shape, lambda *a: (0,) * len(shape))
    if transposed_out:
        out_spec = pl.BlockSpec((None, N_KV, V_ROWS, m), lambda i, *a: (i, 0, 0, 0))
        out_shape = jax.ShapeDtypeStruct((n_out_blocks, N_KV, V_ROWS, m), BF16)
    else:
        out_spec = pl.BlockSpec((None, m, n_out), lambda i, *a: (i, 0, 0))
        out_shape = jax.ShapeDtypeStruct((n_out_blocks, m, n_out), BF16)
    grid_spec = pltpu.PrefetchScalarGridSpec(
        num_scalar_prefetch=nsp,
        grid=grid,
        in_specs=list(seg_specs) + [const(w1.shape), const(w1ab.shape), const(pe.shape), const(w2.shape)],
        out_specs=out_spec,
        scratch_shapes=[pltpu.VMEM((m, N_KV * 1024), F32),
                        pltpu.VMEM((m + 8, 2 * CMP_HID), F32),
                        pltpu.VMEM((m, N_KV * CMP_HID), BF16)],
    )
    args = ([] if prefetch is None else [prefetch]) + list(seg_arrays) + [w1, w1ab, pe, w2]
    return pl.pallas_call(
        functools.partial(_compress_kernel, n_prefetch=nsp, n_seg=n_seg, seg_rows=seg_rows,
                          transposed_out=transposed_out),
        grid_spec=grid_spec,
        out_shape=out_shape,
        compiler_params=_params(("parallel",)),
        name="compress",
    )(*args)


def _compress_weights(pe, w1, w2, head_stride):
    half = STRIDE * HEAD_DIM
    w1b = w1.astype(BF16)
    w1ab = jnp.concatenate([w1b[:half], w1b[half:]], axis=1)
    w2p = jnp.pad(w2, ((0, 0), (0, head_stride - HEAD_DIM)))
    w2blk = jnp.einsum("gh,kd->gkhd", jnp.eye(N_KV, dtype=F32), w2p)
    w2blk = w2blk.reshape(N_KV * CMP_HID, N_KV * head_stride).astype(BF16)
    return w1b, w1ab, pe.reshape(1, L_CMP * HEAD_DIM), w2blk


def _kth_threshold(score, axis):
    shape = list(score.shape)
    shape[axis] = 1

    def body(_, carry):
        thr, cnt = carry
        cand = jnp.where(score < thr, score, -jnp.inf)
        mx = jnp.max(cand, axis=axis, keepdims=True)
        c = jnp.sum(jnp.where(score >= mx, 1.0, 0.0), axis=axis, keepdims=True)
        upd = cnt < N_SEL
        return jnp.where(upd, mx, thr), jnp.where(upd, c, cnt)

    thr, _ = lax.fori_loop(0, N_SEL, body, (jnp.full(shape, jnp.inf, F32), jnp.zeros(shape, F32)))
    return thr


TK = 1024
BIAS_ROWS = TK // L_SLC
N_WBLK = WINDOW // CHUNK + 1


def _attn_prompt_kernel(qt_ref, gt_ref, kcc_ref, vcc_ref, ks_ref, vs_ref, *refs):
    kw_refs, vw_refs = refs[:N_WBLK], refs[N_WBLK:2 * N_WBLK]
    (ovt_ref, tri_ref, dc_ref, dd_ref, wb_ref, o_ref,
     qaug_ref, sel_ref, m_ref, acc_ref, ob_ref, sw_ref) = refs[2 * N_WBLK:]
    i = pl.program_id(1)
    qs = i * CHUNK
    nq = CHUNK
    gt = gt_ref[...]

    def gate(g, branch):
        return jnp.concatenate([gt[6 * g + branch:6 * g + branch + 1],
                                gt[6 * g + 3 + branch:6 * g + 3 + branch + 1]], axis=1)

    cmask = dc_ref[...] <= qs
    ns = sel_ref.shape[1]
    n_idx = lax.broadcasted_iota(jnp.int32, (ns, nq), 0)
    cur = (qs + lax.broadcasted_iota(jnp.int32, (ns, nq), 1)) // L_SLC
    valid = n_idx <= cur
    forced = valid & ((n_idx == 0) | (n_idx >= cur - 1))
    for g in range(N_KV):
        base = 2 * g * HEAD_DIM
        qaug_ref[g, 0:HEAD_DIM, :] = jnp.concatenate(
            [qt_ref[base:base + HEAD_DIM, :], qt_ref[base + HEAD_DIM:base + 2 * HEAD_DIM, :]], axis=1)
        qaug_ref[g, HEAD_DIM:, :] = jnp.zeros((LANES - HEAD_DIM, 2 * nq), BF16)
        s = jnp.where(cmask, _dot(kcc_ref[:, g * LANES:(g + 1) * LANES], qaug_ref[g]), NEG)
        mx = jnp.max(s, axis=0, keepdims=True)
        mx = jnp.where(mx > 0.5 * NEG, mx, 0.0)
        p = jnp.exp2(s - mx)
        oc = _dot(vcc_ref[g], p.astype(BF16))
        inv = 1.0 / jnp.maximum(oc[HEAD_DIM:HEAD_DIM + 1], 1e-20)
        ob_ref[g] = gate(g, 0) * (oc[0:HEAD_DIM] * inv)
        imp = _split_dot_left(ovt_ref[...], p[:, :nq] * inv[:, :nq] + p[:, nq:] * inv[:, nq:])
        score = jnp.where(forced, FORCED_SCORE, jnp.where(valid, imp, -jnp.inf))
        thr = _kth_threshold(score, 0)
        above = score > thr
        tie = score == thr
        need = N_SEL - jnp.sum(jnp.where(above, 1.0, 0.0), axis=0, keepdims=True)
        rank = _dot(tri_ref[...], jnp.where(tie, 1.0, 0.0).astype(BF16))
        sel_ref[g] = jnp.where(above | (tie & (rank <= need)), 0.0, SEL_BIAS)
        m_ref[g] = jnp.full((1, 2 * nq), NEG, F32)
        acc_ref[g] = jnp.zeros((V_ROWS, 2 * nq), F32)

    def tile(kt, diagonal):
        k0 = pl.multiple_of(kt * TK, TK)
        b0 = pl.multiple_of(kt * BIAS_ROWS, BIAS_ROWS)
        for g in range(N_KV):
            bias = sel_ref[g, pl.ds(b0, BIAS_ROWS), :]
            qaug_ref[g, HEAD_DIM:HEAD_DIM + BIAS_ROWS, :] = jnp.concatenate([bias, bias], axis=1).astype(BF16)
            st = _dot(ks_ref[g, pl.ds(k0, TK), :], qaug_ref[g])
            if diagonal:
                st = jnp.where(dd_ref[...] <= qs - k0, st, NEG)
            m_old = m_ref[g]
            m_new = jnp.maximum(m_old, jnp.max(st, axis=0, keepdims=True))
            pt = jnp.exp2(st - m_new)
            acc_ref[g] = jnp.exp2(m_old - m_new) * acc_ref[g] + _dot(vs_ref[g, :, pl.ds(k0, TK)], pt.astype(BF16))
            m_ref[g] = m_new

    n_full = qs // TK

    def full_tile(kt, carry):
        tile(kt, False)
        return carry

    lax.fori_loop(0, n_full, full_tile, 0)
    tile(n_full, True)

    nw = N_WBLK * CHUNK
    outs = []
    for g in range(N_KV):
        acc = acc_ref[g]
        ob = ob_ref[g] + gate(g, 1) * (acc[0:HEAD_DIM] * (1.0 / acc[HEAD_DIM:HEAD_DIM + 1]))
        kwin = jnp.concatenate([r[g] for r in kw_refs], axis=0)
        sw_ref[...] = _dot(kwin, qaug_ref[g]) + wb_ref[...]

        @pl.when(i < WINDOW // CHUNK)
        def _():
            j_idx = lax.broadcasted_iota(jnp.int32, (nw, 2 * nq), 0)
            sw_ref[...] = jnp.where(j_idx >= WINDOW - qs, sw_ref[...], NEG)

        sw = sw_ref[...]
        pw = jnp.exp2(sw - jnp.max(sw, axis=0, keepdims=True))
        vwin = jnp.concatenate([r[g] for r in vw_refs], axis=1)
        ow = _dot(vwin, pw.astype(BF16))
        ob = ob + gate(g, 2) * (ow[0:HEAD_DIM] * (1.0 / ow[HEAD_DIM:HEAD_DIM + 1]))
        outs += [ob[:, :nq], ob[:, nq:]]
    o_ref[...] = jnp.concatenate(outs, axis=0).T.astype(o_ref.dtype)


def _attn_prompt(qt, gt, kcc, vcc, ks_aug, vs_aug, kw_aug, vw_aug, ovt, tri, dc, dd, wb):
    b, _, s = qt.shape
    nqb = s // CHUNK
    nc = kcc.shape[1]
    first = WINDOW // CHUNK
    whole = lambda shape: pl.BlockSpec((None,) + shape, lambda bi, i: (bi,) + (0,) * len(shape),
                                       pipeline_mode=pl.Buffered(1))
    const = lambda a: pl.BlockSpec(a.shape, lambda bi, i: (0,) * a.ndim)

    def kw_spec(j):
        return pl.BlockSpec((None, N_KV, CHUNK, LANES), lambda bi, i: (bi, 0, jnp.maximum(i - first + j, 0), 0))

    def vw_spec(j):
        return pl.BlockSpec((None, N_KV, V_ROWS, CHUNK), lambda bi, i: (bi, 0, 0, jnp.maximum(i - first + j, 0)))

    in_specs = ([pl.BlockSpec((None, N_HEADS * HEAD_DIM, CHUNK), lambda bi, i: (bi, 0, i)),
                 pl.BlockSpec((None, 32, CHUNK), lambda bi, i: (bi, 0, i)),
                 pl.BlockSpec((None, nc, N_KV * LANES), lambda bi, i: (bi, 0, 0)),
                 pl.BlockSpec((None, N_KV, V_ROWS, nc), lambda bi, i: (bi, 0, 0, 0)),
                 whole((N_KV, s, LANES)), whole((N_KV, V_ROWS, s))]
                + [kw_spec(j) for j in range(N_WBLK)] + [vw_spec(j) for j in range(N_WBLK)]
                + [const(ovt), const(tri), const(dc), const(dd), const(wb)])
    return pl.pallas_call(
        _attn_prompt_kernel,
        grid=(b, nqb),
        in_specs=in_specs,
        out_specs=pl.BlockSpec((None, CHUNK, N_HEADS * HEAD_DIM), lambda bi, i: (bi, i, 0)),
        out_shape=jax.ShapeDtypeStruct((b, s, N_HEADS * HEAD_DIM), BF16),
        scratch_shapes=[pltpu.VMEM((N_KV, LANES, 2 * CHUNK), BF16),
                        pltpu.VMEM((N_KV, s // L_SLC, CHUNK), F32),
                        pltpu.VMEM((N_KV, 1, 2 * CHUNK), F32),
                        pltpu.VMEM((N_KV, V_ROWS, 2 * CHUNK), F32),
                        pltpu.VMEM((N_KV, HEAD_DIM, 2 * CHUNK), F32),
                        pltpu.VMEM((N_WBLK * CHUNK, 2 * CHUNK), F32)],
        compiler_params=_params(("parallel", "arbitrary")),
        name="attn_prompt",
    )(qt, gt, kcc, vcc, ks_aug, vs_aug, *([kw_aug] * N_WBLK), *([vw_aug] * N_WBLK), ovt, tri, dc, dd, wb)


NQ_PAD = 8


def _softmax_rows(s, mask):
    s = jnp.where(mask, s, NEG)
    mx = jnp.max(s, axis=1, keepdims=True)
    p = jnp.where(mask, jnp.exp(s - mx), 0.0)
    return p / jnp.maximum(jnp.sum(p, axis=1, keepdims=True), 1e-20)


def _attn_sample_kernel(pt_ref, qbd_ref, g_ref, kcc_ref, vcc_ref, *refs, n_pages, past_len):
    kpages = refs[:n_pages]
    vpages = refs[n_pages:2 * n_pages]
    (kst_ref, vst_ref, kws_ref, vws_ref, kwt_ref, vwt_ref, ov_ref, triu_ref, eexp_ref,
     o_ref, s_scr) = refs[2 * n_pages:]
    del pt_ref
    qbd = qbd_ref[...]
    nrow = qbd.shape[0]
    page = kpages[0].shape[0]

    def t_of(shape):
        return past_len + (lax.broadcasted_iota(jnp.int32, shape, 0) & (NQ_PAD - 1))

    nc = kcc_ref.shape[0]
    s = _dot_nt(qbd, kcc_ref[...])
    c_idx = lax.broadcasted_iota(jnp.int32, (nrow, nc), 1)
    p_c = _softmax_rows(s, c_idx * STRIDE + (L_CMP - 1) <= t_of((nrow, nc)))
    o_c = _dot(p_c.astype(BF16), vcc_ref[...])

    half = nrow // 2
    imp = _split_dot_right(p_c[:half] + p_c[half:], ov_ref[...])
    ns = imp.shape[1]
    n_idx = lax.broadcasted_iota(jnp.int32, (half, ns), 1)
    cur = t_of((half, ns)) // L_SLC
    valid = n_idx <= cur
    forced = valid & ((n_idx == 0) | (n_idx >= cur - 1))
    score = jnp.where(forced, FORCED_SCORE, jnp.where(valid, imp, -jnp.inf))
    thr = _kth_threshold(score, 1)
    above = score > thr
    tie = score == thr
    need = N_SEL - jnp.sum(jnp.where(above, 1.0, 0.0), axis=1, keepdims=True)
    rank = _dot(jnp.where(tie, 1.0, 0.0).astype(BF16), triu_ref[...])
    sel = jnp.where(above | (tie & (rank <= need)), 1.0, 0.0).astype(BF16)
    sel_keys = _dot(jnp.concatenate([sel, sel], axis=0), eexp_ref[...])

    for pg in range(n_pages):
        s_scr[:, pg * page:(pg + 1) * page] = _dot_nt(qbd, kpages[pg][...].astype(BF16))
    s_scr[:, n_pages * page:(n_pages + 1) * page] = _dot_nt(qbd, kst_ref[...])
    nk = (n_pages + 1) * page
    tok = lax.broadcasted_iota(jnp.int32, (nrow, nk), 1)
    p_s = _softmax_rows(s_scr[...], (sel_keys > 0.5) & (tok <= t_of((nrow, nk))))
    o_s = _dot(p_s[:, n_pages * page:].astype(BF16), vst_ref[...])
    for pg in range(n_pages):
        o_s = o_s + _dot(p_s[:, pg * page:(pg + 1) * page].astype(BF16), vpages[pg][...].astype(BF16))

    wb = kws_ref.shape[0]
    nw = wb + page
    sw = jnp.concatenate([_dot_nt(qbd, kws_ref[...].astype(BF16)), _dot_nt(qbd, kwt_ref[...])], axis=1)
    rel = t_of((nrow, nw)) - (past_len - wb + lax.broadcasted_iota(jnp.int32, (nrow, nw), 1))
    p_w = _softmax_rows(sw, (rel >= 0) & (rel < WINDOW))
    o_w = _dot(p_w[:, :wb].astype(BF16), vws_ref[...].astype(BF16)) + _dot(p_w[:, wb:].astype(BF16), vwt_ref[...])

    g = g_ref[...]
    o = g[:, 0:1] * o_c + g[:, 1:2] * o_s + g[:, 2:3] * o_w
    row_g = (lax.broadcasted_iota(jnp.int32, o.shape, 0) // NQ_PAD) & (N_KV - 1)
    lane_g = lax.broadcasted_iota(jnp.int32, o.shape, 1) // HEAD_DIM
    o = jnp.where(row_g == lane_g, o, 0.0)
    per_r = N_KV * NQ_PAD
    for r in range(2):
        acc = o[r * per_r:r * per_r + NQ_PAD]
        for gg in range(1, N_KV):
            acc = acc + o[r * per_r + gg * NQ_PAD:r * per_r + (gg + 1) * NQ_PAD]
        o_ref[r] = acc


def _attn_sample(page_table, qbd, gsm, kcc, vcc, cache_k, cache_v, k_tail, v_tail, kw_state, vw_state,
                 kw_tail, vw_tail, ov, triu, eexp, past_len):
    nb, n_pages = page_table.shape
    page = cache_k.shape[1]
    kvw = cache_k.shape[2]
    nrow = qbd.shape[1]
    per_b = lambda shape: pl.BlockSpec((None,) + shape, lambda b, pt: (b,) + (0,) * len(shape))
    const = lambda shape: pl.BlockSpec(shape, lambda b, pt: (0,) * len(shape))

    def page_spec(pg):
        return pl.BlockSpec((None, page, kvw), lambda b, pt: (pt[b, pg], 0, 0))

    in_specs = ([per_b((nrow, kvw)), per_b((nrow, 8)), per_b(kcc.shape[1:]), per_b(vcc.shape[1:])]
                + [page_spec(pg) for pg in range(n_pages)]
                + [page_spec(pg) for pg in range(n_pages)]
                + [per_b((page, kvw)), per_b((page, kvw)), per_b(kw_state.shape[1:]), per_b(vw_state.shape[1:]),
                   per_b((page, kvw)), per_b((page, kvw)), const(ov.shape), const(triu.shape), const(eexp.shape)])
    grid_spec = pltpu.PrefetchScalarGridSpec(
        num_scalar_prefetch=1,
        grid=(nb,),
        in_specs=in_specs,
        out_specs=pl.BlockSpec((None, 2, NQ_PAD, kvw), lambda b, pt: (b, 0, 0, 0)),
        scratch_shapes=[pltpu.VMEM((nrow, (n_pages + 1) * page), F32)],
    )
    return pl.pallas_call(
        functools.partial(_attn_sample_kernel, n_pages=n_pages, past_len=past_len),
        grid_spec=grid_spec,
        out_shape=jax.ShapeDtypeStruct((nb, 2, NQ_PAD, kvw), F32),
        compiler_params=_params(("parallel",)),
        name="attn_sample",
    )(page_table, qbd, gsm, kcc, vcc, *([cache_k] * n_pages), *([cache_v] * n_pages),
      k_tail, v_tail, kw_state, vw_state, kw_tail, vw_tail, ov, triu, eexp)


FF_SPLIT = 2


def _finish_kernel(x_ref, a_ref, b_ref, gt1_ref, sh2_ref, sc2_ref, gt2_ref, wout_ref, gffn_ref, win_ref, wo2_ref,
                   o_ref):
    half = a_ref.shape[1]
    y = _dot(a_ref[...], wout_ref[0:half, :]) + _dot(b_ref[...], wout_ref[half:, :])
    x1 = x_ref[...] + gt1_ref[...] * y
    ms = jnp.mean(x1 * x1, axis=-1, keepdims=True)
    h = x1 * lax.rsqrt(ms + EPS) * gffn_ref[...]
    hb = (h * (1.0 + sc2_ref[...]) + sh2_ref[...]).astype(BF16)
    d_ff = wo2_ref.shape[0]
    step = d_ff // FF_SPLIT
    acc = None
    for c in range(FF_SPLIT):
        up = _dot(hb, win_ref[:, c * step:(c + 1) * step])
        gate = _dot(hb, win_ref[:, d_ff + c * step:d_ff + (c + 1) * step])
        z = (jax.nn.silu(up) * gate).astype(BF16)
        part = _dot(z, wo2_ref[c * step:(c + 1) * step, :])
        acc = part if acc is None else acc + part
    o_ref[...] = x1 + gt2_ref[...] * acc


def _finish(x, a, b, mods, per_row_mod, rows_per_mod, wout, gffn, win, wo2, tm):
    r, d = x.shape
    if per_row_mod:
        mod_spec = pl.BlockSpec((tm, d), lambda i: (i, 0))
    else:
        tiles_per_mod = rows_per_mod // tm
        mod_spec = pl.BlockSpec((None, 1, d), lambda i: (i // tiles_per_mod, 0, 0))
    single = lambda shape: pl.BlockSpec(shape, lambda i: (0,) * len(shape), pipeline_mode=pl.Buffered(1))
    row = lambda n: pl.BlockSpec((tm, n), lambda i: (i, 0))
    return pl.pallas_call(
        _finish_kernel,
        grid=(r // tm,),
        in_specs=[row(d), row(a.shape[1]), row(b.shape[1]), mod_spec, mod_spec, mod_spec, mod_spec,
                  single(wout.shape), single(gffn.shape), single(win.shape), single(wo2.shape)],
        out_specs=row(d),
        out_shape=jax.ShapeDtypeStruct((r, d), F32),
        compiler_params=_params(("parallel",)),
        name="finish",
    )(x, a, b, *mods, wout, gffn, win, wo2)


def _rope_tables(pos):
    half = HEAD_DIM // 2
    inv = ROPE_THETA ** (-jnp.arange(half, dtype=F32) / half)
    ang = pos.astype(F32)[:, None] * inv[None, :]
    cos, sin = jnp.cos(ang), jnp.sin(ang)
    return jnp.concatenate([cos] * 4, axis=1), jnp.concatenate([-sin, sin, -sin, sin], axis=1)


def _overlap(n_c, n_s):
    c_start = jnp.arange(n_c) * STRIDE
    blk = jnp.arange(n_s)
    return ((c_start[:, None] < (blk[None, :] + 1) * L_SLC)
            & (c_start[:, None] + L_CMP > blk[None, :] * L_SLC)).astype(BF16)


def _tail_page(new_rows, page):
    return jnp.pad(new_rows, ((0, 0), (0, page - new_rows.shape[1]), (0, 0))).astype(BF16)


def kernel(x_prompt, x_sample, cache_k_cmp, cache_v_cmp, cache_k_slc, cache_v_slc, state_k_win, state_v_win,
           page_table, c_prompt, c_sample, w_ada, b_ada, g_mix_norm, g_ffn_norm, w_in, g_sgu, w_sgu, b_sgu,
           g_q, g_k_cmp, g_k_slc, g_k_win, pe_k_cmp, pe_v_cmp, w_ck1, w_ck2, w_cv1, w_cv2, w_out, w_ffn_in,
           w_ffn_out):
    depth = w_in.shape[0]
    assert depth == 1, "single trunk layer"
    nb_p, seq, d = x_prompt.shape
    nb_s, n_new, _ = x_sample.shape
    n_pool, page = cache_k_cmp.shape[1], cache_k_cmp.shape[2]
    kvw = N_KV * HEAD_DIM
    n_pages = page_table.shape[1]
    past_len = n_pages * page
    wb_s = state_k_win.shape[2]
    l = 0

    in_cols = w_in.shape[2]
    w_in_pad = jnp.pad(w_in[l], ((0, 0), (0, 3200 - in_cols))).astype(BF16)
    tile_gain = lambda g, heads: jnp.tile(g, heads).reshape(1, heads * HEAD_DIM)
    gq_t, gkc_t = tile_gain(g_q[l], N_HEADS), tile_gain(g_k_cmp[l], N_KV)
    gks_t, gkw_t = tile_gain(g_k_slc[l], N_KV), tile_gain(g_k_win[l], N_KV)
    gsgu = g_sgu[l].reshape(1, -1)
    gmix = g_mix_norm[l].reshape(1, d)
    gffn = g_ffn_norm[l].reshape(1, d)
    gmat = (jnp.kron(jnp.eye(256 // HEAD_DIM, dtype=F32), jnp.ones((HEAD_DIM, HEAD_DIM), F32)) / HEAD_DIM).astype(BF16)
    w_tril = jnp.where(jnp.tril(jnp.ones((CHUNK, CHUNK), bool)), w_sgu[l], 0)
    wmix_p = w_tril.astype(BF16)
    bmix_p = jnp.repeat(b_sgu[l].T, HEAD_DIM, axis=1)
    eye_s = jnp.eye(CHUNK // n_new, dtype=F32)
    wmix_s = jax.vmap(lambda w: jnp.kron(eye_s, w[:n_new, :n_new]))(w_tril).astype(BF16)
    bmix_s = jnp.tile(jnp.repeat(b_sgu[l].T[:n_new], HEAD_DIM, axis=1), (CHUNK // n_new, 1))
    wout_b = w_out[l].astype(BF16)
    win_b = w_ffn_in[l].astype(BF16)
    wo2_b = w_ffn_out[l].astype(BF16)

    n_c = nb_p + nb_s
    n_c_pad = -(-n_c // 8) * 8
    c_all = jnp.pad(jnp.concatenate([c_prompt, c_sample], axis=0), ((0, n_c_pad - n_c), (0, 0)))
    ada = _ada(c_all, w_ada[l].astype(BF16), b_ada[l].reshape(1, -1))
    mods_p = [m.reshape(nb_p, 1, d) for m in jnp.split(ada[:nb_p], 6, axis=-1)]
    mods_s = [jnp.repeat(m, n_new, axis=0) for m in jnp.split(ada[nb_p:n_c], 6, axis=-1)]

    cos_p, sin_p = _rope_tables(jnp.arange(seq))
    cos_s, sin_s = _rope_tables(past_len + jnp.arange(n_new))
    reps = CHUNK // n_new
    cos_s, sin_s = jnp.tile(cos_s, (reps, 1)), jnp.tile(sin_s, (reps, 1))
    xp = x_prompt.reshape(nb_p * seq, d)
    xs = x_sample.reshape(nb_s * n_new, d)
    gains = (gsgu, gq_t, gkc_t, gks_t, gkw_t)
    blk_of_key = (jnp.arange(seq) // L_SLC) % BIAS_ROWS
    onehot = jnp.pad(jax.nn.one_hot(blk_of_key, LANES - HEAD_DIM, dtype=F32), ((0, 0), (HEAD_DIM, 0)))
    q_scale = HEAD_DIM ** -0.5 * math.log2(math.e)
    (a_p, vn_p, qt, gt, kc_p, vc_p, kct, vct, kst, vst, kwt, vwt, ks_aug, vs_aug, kw_aug, vw_aug) = _proj_cols(
        xp, nb_p, seq, mods_p[0], mods_p[1], gmix, w_in_pad, cos_p, sin_p, onehot, *gains, wmix_p, bmix_p, gmat,
        256, q_scale)
    (a_s, vn_s, q_s, kc_s, vc_s, ks_s, vs_s, kw_s, vw_s, gate_s) = _proj_rows(
        xs, mods_s[0], mods_s[1], gmix, w_in_pad, cos_s, sin_s, *gains, wmix_s, bmix_s, gmat, CHUNK)

    n_chunk_p = seq // STRIDE
    n_blk_p = seq // L_SLC
    n_pair = N_KV // 2
    wk_p = _compress_weights(pe_k_cmp[l], w_ck1[l], w_ck2[l], LANES)
    wv_p = _compress_weights(pe_v_cmp[l], w_cv1[l], w_cv2[l], LANES)
    seg_p = [pl.BlockSpec((None, seq, LANES), functools.partial(lambda pp, i: (i, 0, pp), pp))
             for pp in range(n_pair)]
    kcc_p = _compress_call([kc_p.reshape(nb_p, seq, kvw)] * n_pair, seg_p, (nb_p,), nb_p, *wk_p, seq)
    vcc_p = _compress_call([vc_p.reshape(nb_p, seq, kvw)] * n_pair, seg_p, (nb_p,), nb_p, *wv_p, seq,
                           transposed_out=True)
    ovt = _overlap(n_chunk_p, n_blk_p).T
    tri = jnp.tril(jnp.ones((n_blk_p, n_blk_p), BF16))
    q_lane = jnp.arange(2 * CHUNK)[None, :] % CHUNK
    dc = (jnp.arange(n_chunk_p)[:, None] * STRIDE + (L_CMP - 1) - q_lane).astype(jnp.int32)
    dd = (jnp.arange(TK)[:, None] - q_lane).astype(jnp.int32)
    jq = jnp.arange(N_WBLK * CHUNK)[:, None] - q_lane
    wb = jnp.where((jq > 0) & (jq <= WINDOW), 0.0, NEG).astype(F32)
    b_p = _attn_prompt(qt, gt, kcc_p, vcc_p, ks_aug, vs_aug, kw_aug, vw_aug, ovt, tri, dc, dd, wb)
    y_p = _finish(xp, a_p, b_p.reshape(nb_p * seq, N_HEADS * HEAD_DIM), mods_p[2:], False, seq, wout_b, gffn,
                  win_b, wo2_b, 256)

    ck = cache_k_cmp[l].reshape(n_pool, page, kvw)
    cv = cache_v_cmp[l].reshape(n_pool, page, kvw)
    cks = cache_k_slc[l].reshape(n_pool, page, kvw)
    cvs = cache_v_slc[l].reshape(n_pool, page, kvw)
    wk_s = _compress_weights(pe_k_cmp[l], w_ck1[l], w_ck2[l], HEAD_DIM)
    wv_s = _compress_weights(pe_v_cmp[l], w_cv1[l], w_cv2[l], HEAD_DIM)

    def seg_spec(pg, pp):
        return pl.BlockSpec((None, page, LANES), lambda i, pt: (pt[i, pg], 0, pp))

    seg_s = [seg_spec(pg, pp) for pg in range(n_pages) for pp in range(n_pair)]
    n_in = n_pages * n_pair
    kcc_s = _compress_call([ck] * n_in, seg_s, (nb_s,), nb_s, *wk_s, page, prefetch=page_table)
    vcc_s = _compress_call([cv] * n_in, seg_s, (nb_s,), nb_s, *wv_s, page, prefetch=page_table)

    q5 = (q_s * (HEAD_DIM ** -0.5)).reshape(nb_s, n_new, N_KV, 2, HEAD_DIM).transpose(0, 3, 2, 1, 4)
    q5 = jnp.pad(q5, ((0, 0), (0, 0), (0, 0), (0, NQ_PAD - n_new), (0, 0)))
    qbd = jnp.einsum("brgqd,gh->brgqhd", q5, jnp.eye(N_KV, dtype=F32))
    qbd = qbd.reshape(nb_s, 2 * N_KV * NQ_PAD, kvw).astype(BF16)
    g5 = gate_s[:, :3 * N_HEADS].reshape(nb_s, n_new, N_KV, 2, 3).transpose(0, 3, 2, 1, 4)
    g5 = jnp.pad(g5, ((0, 0), (0, 0), (0, 0), (0, NQ_PAD - n_new), (0, 5)))
    gsm = g5.reshape(nb_s, 2 * N_KV * NQ_PAD, 8)
    new = lambda a: a.reshape(nb_s, n_new, kvw)
    n_keys = (n_pages + 1) * page
    n_c_s = kcc_s.shape[1]
    ov_s = _overlap(n_c_s, LANES)
    triu = jnp.triu(jnp.ones((LANES, LANES), BF16))
    eexp = (jnp.arange(n_keys)[None, :] // L_SLC == jnp.arange(LANES)[:, None]).astype(BF16)
    os_ = _attn_sample(page_table, qbd, gsm, kcc_s, vcc_s, cks, cvs, _tail_page(new(ks_s), page),
                       _tail_page(new(vs_s), page), state_k_win[l].reshape(nb_s, wb_s, kvw),
                       state_v_win[l].reshape(nb_s, wb_s, kvw), _tail_page(new(kw_s), page),
                       _tail_page(new(vw_s), page), ov_s, triu, eexp, past_len)
    b_s = os_[:, :, :n_new].reshape(nb_s, 2, n_new, N_KV, HEAD_DIM).transpose(0, 2, 3, 1, 4)
    b_s = b_s.reshape(nb_s * n_new, N_HEADS * HEAD_DIM).astype(BF16)

    y_s = _finish(xs, a_s, b_s, mods_s[2:], True, 0, wout_b, gffn, win_b, wo2_b, CHUNK)

    wb_p = min(WINDOW, seq)
    assert seq - ((seq - 1) // CHUNK) * CHUNK == CHUNK, "the prompt ends on a full chunk"
    from_cols = lambda t: t.reshape(1, nb_p, N_KV, HEAD_DIM, seq).transpose(0, 1, 4, 2, 3)
    outs_p = [from_cols(kct), from_cols(vct), from_cols(kst), from_cols(vst),
              from_cols(kwt)[:, :, seq - wb_p:], from_cols(vwt)[:, :, seq - wb_p:], vn_p[None]]
    kv5 = lambda a, nb, t: a.reshape(1, nb, t, N_KV, HEAD_DIM)
    kw_all = jnp.concatenate([state_k_win[l], kv5(kw_s, nb_s, n_new)[0]], axis=1)
    vw_all = jnp.concatenate([state_v_win[l], kv5(vw_s, nb_s, n_new)[0]], axis=1)
    outs_s = [kv5(kc_s, nb_s, n_new), kv5(vc_s, nb_s, n_new), kv5(ks_s, nb_s, n_new), kv5(vs_s, nb_s, n_new),
              kw_all[None, :, n_new:], vw_all[None, :, n_new:], vn_s.reshape(1, nb_s, n_new, -1)]
    return (y_p.reshape(nb_p, seq, d), y_s.reshape(nb_s, n_new, d), *outs_p, *outs_s)
```

```python
import functools
import math

import jax
import jax.numpy as jnp
from jax import lax
from jax.experimental import pallas as pl
from jax.experimental.pallas import tpu as pltpu

F32 = jnp.float32
BF16 = jnp.bfloat16

CHUNK = 128
A_GROUPS = 8
HEAD_DIM = 64
N_HEADS = 8
N_KV = 4
L_CMP = 32
STRIDE = 16
CMP_HID = 256
L_SLC = 64
N_SEL = 16
WINDOW = 512
ROPE_THETA = 10000.0
EPS = 1e-6
FORCED_SCORE = 1e4
NEG = -1e30
SEL_BIAS = -(2.0 ** 100)

LANES = 128
V_ROWS = HEAD_DIM + 16
VMEM_LIMIT = 52 * 1024 * 1024

_NT = (((1,), (1,)), ((), ()))


def _dot(a, b):
    return jnp.dot(a, b, preferred_element_type=F32)


def _dot_nt(a, b):
    return lax.dot_general(a, b, _NT, preferred_element_type=F32)


def _split_dot_left(coef, x):
    hi = x.astype(BF16)
    lo = (x - hi.astype(F32)).astype(BF16)
    return _dot(coef, hi) + _dot(coef, lo)


def _split_dot_right(x, coef):
    hi = x.astype(BF16)
    lo = (x - hi.astype(F32)).astype(BF16)
    return _dot(hi, coef) + _dot(lo, coef)


def _params(sem):
    return pltpu.CompilerParams(dimension_semantics=sem, vmem_limit_bytes=VMEM_LIMIT)


def _with_ones_row(vt):
    n = vt.shape[1]
    row = lax.broadcasted_iota(jnp.int32, (V_ROWS - HEAD_DIM, n), 0)
    return jnp.concatenate([vt, jnp.where(row == 0, 1.0, 0.0).astype(vt.dtype)], axis=0)


def _ada_kernel(c_ref, w_ref, b_ref, o_ref):
    c = c_ref[...]
    o_ref[...] = _dot(jax.nn.silu(c).astype(BF16), w_ref[...]) + b_ref[...]


def _ada(c, w, b):
    m, k = c.shape
    n = w.shape[1]
    tn = 1024
    return pl.pallas_call(
        _ada_kernel,
        grid=(n // tn,),
        in_specs=[pl.BlockSpec((m, k), lambda j: (0, 0)),
                  pl.BlockSpec((k, tn), lambda j: (0, j)),
                  pl.BlockSpec((1, tn), lambda j: (0, j))],
        out_specs=pl.BlockSpec((m, tn), lambda j: (0, j)),
        out_shape=jax.ShapeDtypeStruct((m, n), F32),
        compiler_params=_params(("parallel",)),
        name="ada",
    )(c, w, b)


def _group_mean_sq(y, g_ref):
    y2 = y * y
    cols = []
    for c in range(y.shape[1] // 256):
        cols.append(_split_dot_right(y2[:, 256 * c:256 * (c + 1)], g_ref[...]))
    return cols[0] if len(cols) == 1 else jnp.concatenate(cols, axis=1)


def _rope(x, cos, sin):
    n = x.shape[1]
    reps = n // LANES
    cos_t = cos if reps == 1 else jnp.concatenate([cos] * reps, axis=1)
    sin_t = sin if reps == 1 else jnp.concatenate([sin] * reps, axis=1)
    lane = lax.broadcasted_iota(jnp.int32, x.shape, 1)
    first_half = (lane & (HEAD_DIM - 1)) < (HEAD_DIM // 2)
    partner = jnp.where(first_half, pltpu.roll(x, n - HEAD_DIM // 2, 1), pltpu.roll(x, HEAD_DIM // 2, 1))
    return x * cos_t + partner * sin_t


def _proj_common(x_ref, sh_ref, sc_ref, gmix_ref, w_ref, cos_ref, sin_ref, gsgu_ref, gq_ref, gkc_ref, gks_ref,
                 gkw_ref, wmix_ref, bmix_ref, gmat_ref, a_ref, vn_ref):
    x = x_ref[...]
    tm = x.shape[0]
    ms = jnp.mean(x * x, axis=-1, keepdims=True)
    h = x * lax.rsqrt(ms + EPS) * gmix_ref[...]
    h = h * (1.0 + sc_ref[...]) + sh_ref[...]
    hb = h.astype(BF16)
    cos = cos_ref[...]
    sin = sin_ref[...]

    def seg(lo, hi):
        return _dot(hb, w_ref[:, lo:hi])

    def head_norm(y, g_ref):
        return y * lax.rsqrt(_group_mean_sq(y, gmat_ref) + EPS) * g_ref[...]

    u = jax.nn.gelu(seg(0, 512))
    v = jax.nn.gelu(seg(512, 1024))
    vn = head_norm(v, gsgu_ref)
    vn_ref[...] = vn[tm - CHUNK:]
    vb = vn.astype(BF16)
    lane = lax.broadcasted_iota(jnp.int32, (CHUNK, LANES), 1)
    low = lane < HEAD_DIM
    for ck in range(tm // CHUNK):
        rows = slice(ck * CHUNK, (ck + 1) * CHUNK)
        for pr in range(A_GROUPS // 2):
            cols = slice(pr * LANES, (pr + 1) * LANES)
            vp = vb[rows, cols]
            mixed = jnp.where(low, _dot(wmix_ref[2 * pr], vp), _dot(wmix_ref[2 * pr + 1], vp))
            mixed = mixed + bmix_ref[:, cols]
            a_ref[rows, cols] = (u[rows, cols] * mixed).astype(a_ref.dtype)

    q = _rope(head_norm(seg(1024, 1536), gq_ref), cos, sin)
    kc = _rope(head_norm(seg(1536, 1792), gkc_ref), cos, sin)
    vc = seg(1792, 2048)
    ks = _rope(head_norm(seg(2048, 2304), gks_ref), cos, sin)
    vs = seg(2304, 2560)
    kw = _rope(head_norm(seg(2560, 2816), gkw_ref), cos, sin)
    vw = seg(2816, 3072)
    gates = jax.nn.sigmoid(seg(3072, 3200))
    return q, kc, vc, ks, vs, kw, vw, gates


def _proj_rows_kernel(*refs):
    ins, (a_ref, vn_ref, q_ref, kc_ref, vc_ref, ks_ref, vs_ref, kw_ref, vw_ref, gate_ref) = refs[:15], refs[15:]
    outs = _proj_common(*ins, a_ref, vn_ref)
    for ref, val in zip((q_ref, kc_ref, vc_ref, ks_ref, vs_ref, kw_ref, vw_ref, gate_ref), outs):
        ref[...] = val


def _proj_cols_kernel(*refs, q_scale):
    ins, oh_ref = refs[:15], refs[15]
    (a_ref, vn_ref, qt_ref, gt_ref, kct_ref, vct_ref, kst_ref, vst_ref, kwt_ref, vwt_ref,
     ksa_ref, vsa_ref, kwa_ref, vwa_ref, stage_ref) = refs[16:]
    q, kc, vc, ks, vs, kw, vw, gates = _proj_common(*ins, a_ref, vn_ref)
    tm = q.shape[0]
    qt_ref[...] = (q * q_scale).T.astype(BF16)
    gt_ref[...] = gates.T[0:gt_ref.shape[0]]
    kct_ref[...] = kc.T
    kst_ref[...] = ks.T
    kwt_ref[...] = kw.T
    stage_ref[0] = vc
    stage_ref[1] = vs
    stage_ref[2] = vw
    vct_ref[...] = stage_ref[0].T
    vst = stage_ref[1].T
    vwt = stage_ref[2].T
    vst_ref[...] = vst
    vwt_ref[...] = vwt
    low = lax.broadcasted_iota(jnp.int32, (tm, LANES), 1) < HEAD_DIM
    onehot = oh_ref[...]
    for g in range(N_KV):
        pair = slice((g // 2) * LANES, (g // 2 + 1) * LANES)
        ks_g, kw_g = ks[:, pair], kw[:, pair]
        if g % 2 == 1:
            ks_g, kw_g = pltpu.roll(ks_g, HEAD_DIM, 1), pltpu.roll(kw_g, HEAD_DIM, 1)
        ksa_ref[g] = jnp.where(low, ks_g, onehot).astype(BF16)
        kwa_ref[g] = jnp.where(low, kw_g, 0.0).astype(BF16)
        rows = slice(g * HEAD_DIM, (g + 1) * HEAD_DIM)
        vsa_ref[g] = _with_ones_row(vst[rows]).astype(BF16)
        vwa_ref[g] = _with_ones_row(vwt[rows]).astype(BF16)


def _proj_in_specs(tm, d, per_row_mod, rows_per_mod, rope_tiles, w_pad, wmix, bmix, gmat):
    if per_row_mod:
        mod_spec = pl.BlockSpec((tm, d), lambda i: (i, 0))
    else:
        tiles_per_mod = rows_per_mod // tm
        mod_spec = pl.BlockSpec((None, 1, d), lambda i: (i // tiles_per_mod, 0, 0))
    const = lambda shape: pl.BlockSpec(shape, lambda i: (0,) * len(shape))
    rope_spec = pl.BlockSpec((tm, LANES), lambda i: (i % rope_tiles, 0))
    return [pl.BlockSpec((tm, d), lambda i: (i, 0)), mod_spec, mod_spec, const((1, d)), const(w_pad.shape),
            rope_spec, rope_spec,
            const((1, 512)), const((1, 512)), const((1, 256)), const((1, 256)), const((1, 256)),
            const(wmix.shape), const(bmix.shape), const(gmat.shape)]


def _proj_rows(x, sh, sc, gmix, w_pad, cos, sin, gsgu, gq, gkc, gks, gkw, wmix, bmix, gmat, tm):
    r, d = x.shape
    row = lambda n: pl.BlockSpec((tm, n), lambda i: (i, 0))
    out_widths = [512, 512, 512, 256, 256, 256, 256, 256, 256, 128]
    out_dtypes = [BF16] + [F32] * 9
    return pl.pallas_call(
        _proj_rows_kernel,
        grid=(r // tm,),
        in_specs=_proj_in_specs(tm, d, True, 0, cos.shape[0] // tm, w_pad, wmix, bmix, gmat),
        out_specs=[row(n) for n in out_widths],
        out_shape=[jax.ShapeDtypeStruct((r, n), dt) for n, dt in zip(out_widths, out_dtypes)],
        compiler_params=_params(("parallel",)),
        name="proj_rows",
    )(x, sh, sc, gmix, w_pad, cos, sin, gsgu, gq, gkc, gks, gkw, wmix, bmix, gmat)


def _proj_cols(x, nb, seq, sh, sc, gmix, w_pad, cos, sin, onehot, gsgu, gq, gkc, gks, gkw, wmix, bmix, gmat, tm,
               q_scale):
    r, d = x.shape
    tpb = seq // tm
    kvw = N_KV * HEAD_DIM
    bi = lambda i: i // tpb
    ti = lambda i: i % tpb
    row = lambda n: pl.BlockSpec((tm, n), lambda i: (i, 0))
    colt = lambda n: pl.BlockSpec((None, n, tm), lambda i: (bi(i), 0, ti(i)))
    out_specs = [row(512),
                 pl.BlockSpec((None, CHUNK, 512), lambda i: (bi(i), 0, 0)),
                 colt(512), colt(32),
                 colt(kvw), colt(kvw), colt(kvw), colt(kvw), colt(kvw), colt(kvw),
                 pl.BlockSpec((None, N_KV, tm, LANES), lambda i: (bi(i), 0, ti(i), 0)),
                 pl.BlockSpec((None, N_KV, V_ROWS, tm), lambda i: (bi(i), 0, 0, ti(i))),
                 pl.BlockSpec((None, N_KV, tm, LANES), lambda i: (bi(i), 0, ti(i), 0)),
                 pl.BlockSpec((None, N_KV, V_ROWS, tm), lambda i: (bi(i), 0, 0, ti(i)))]
    sds = jax.ShapeDtypeStruct
    out_shape = [sds((r, 512), BF16), sds((nb, CHUNK, 512), F32), sds((nb, 512, seq), BF16),
                 sds((nb, 32, seq), F32)] + [sds((nb, kvw, seq), F32)] * 6 + [
                 sds((nb, N_KV, seq, LANES), BF16), sds((nb, N_KV, V_ROWS, seq), BF16),
                 sds((nb, N_KV, seq, LANES), BF16), sds((nb, N_KV, V_ROWS, seq), BF16)]
    in_specs = _proj_in_specs(tm, d, False, seq, tpb, w_pad, wmix, bmix, gmat)
    in_specs.append(pl.BlockSpec((tm, LANES), lambda i: (ti(i), 0)))
    return pl.pallas_call(
        functools.partial(_proj_cols_kernel, q_scale=q_scale),
        grid=(r // tm,),
        in_specs=in_specs,
        out_specs=out_specs,
        out_shape=out_shape,
        scratch_shapes=[pltpu.VMEM((3, tm, kvw), F32)],
        compiler_params=_params(("arbitrary",)),
        name="proj_cols",
    )(x, sh, sc, gmix, w_pad, cos, sin, gsgu, gq, gkc, gks, gkw, wmix, bmix, gmat, onehot)


def _compress_kernel(*refs, n_prefetch, n_page, transposed_out):
    refs = refs[n_prefetch:]
    page_refs = refs[:n_page]
    (w1_ref, w1a_ref, w1b_ref, pe_ref, w2_ref, o_ref,
     xs_ref, lhs_ref, a_ref, b_ref, hid_ref, stage_ref) = refs[n_page:]
    page = page_refs[0].shape[1]
    m = n_page * page // STRIDE
    for pg in range(n_page):
        xt = page_refs[pg][...].T
        for pp in range(N_KV // 2):
            xs_ref[pp, pg * page:(pg + 1) * page, :] = xt[:, pp * LANES:(pp + 1) * LANES]
    low = lax.broadcasted_iota(jnp.int32, (m, LANES), 1) < HEAD_DIM
    for rr in range(STRIDE // 2):
        cols = slice(rr * LANES, (rr + 1) * LANES)
        for pp in range(N_KV // 2):
            p0 = xs_ref[pp, pl.ds(2 * rr, m, stride=STRIDE), :]
            p1 = xs_ref[pp, pl.ds(2 * rr + 1, m, stride=STRIDE), :]
            r0 = pltpu.roll(p0, HEAD_DIM, 1)
            r1 = pltpu.roll(p1, HEAD_DIM, 1)
            g0 = 2 * pp
            lhs_ref[g0 * m:(g0 + 1) * m, cols] = jnp.where(low, p0, r1).astype(BF16)
            lhs_ref[(g0 + 1) * m:(g0 + 2) * m, cols] = jnp.where(low, r0, p1).astype(BF16)
    pe_term = _dot(jnp.broadcast_to(pe_ref[...], (8, pe_ref.shape[1])).astype(BF16), w1_ref[...])[0:1]
    rows = N_KV * m
    a_ref[...] = _dot(lhs_ref[...], w1a_ref[...])
    b_ref[0:rows, :] = _dot(lhs_ref[...], w1b_ref[...])
    b_ref[rows:rows + 8, :] = jnp.zeros((8, CMP_HID), F32)
    hid_ref[...] = jax.nn.gelu(a_ref[...] + b_ref[pl.ds(1, rows), :] + pe_term).astype(BF16)
    out = _dot(hid_ref[0:m, :], w2_ref[0])
    for g in range(1, N_KV):
        out = out + _dot(hid_ref[g * m:(g + 1) * m, :], w2_ref[g])
    if transposed_out:
        stage_ref[...] = out
        out_t = stage_ref[...].T
        for g in range(N_KV):
            o_ref[g] = _with_ones_row(out_t[g * LANES:g * LANES + HEAD_DIM]).astype(o_ref.dtype)
    else:
        o_ref[...] = out.astype(o_ref.dtype)


def _compress_call(page_array, page_specs, page, grid, weights, prefetch=None, transposed_out=False):
    n_page = len(page_specs)
    m = n_page * page // STRIDE
    n_out = weights[-1].shape[2]
    nsp = 0 if prefetch is None else 1
    const = lambda shape: pl.BlockSpec(shape, lambda *a: (0,) * len(shape))
    if transposed_out:
        out_spec = pl.BlockSpec((None, N_KV, V_ROWS, m), lambda i, *a: (i, 0, 0, 0))
        out_shape = jax.ShapeDtypeStruct((grid[0], N_KV, V_ROWS, m), BF16)
    else:
        out_spec = pl.BlockSpec((None, m, n_out), lambda i, *a: (i, 0, 0))
        out_shape = jax.ShapeDtypeStruct((grid[0], m, n_out), BF16)
    grid_spec = pltpu.PrefetchScalarGridSpec(
        num_scalar_prefetch=nsp,
        grid=grid,
        in_specs=list(page_specs) + [const(w.shape) for w in weights],
        out_specs=out_spec,
        scratch_shapes=[pltpu.VMEM((N_KV // 2, n_page * page, LANES), F32),
                        pltpu.VMEM((N_KV * m, STRIDE * HEAD_DIM), BF16),
                        pltpu.VMEM((N_KV * m, CMP_HID), F32),
                        pltpu.VMEM((N_KV * m + 8, CMP_HID), F32),
                        pltpu.VMEM((N_KV * m, CMP_HID), BF16),
                        pltpu.VMEM((m, n_out), F32)],
    )
    args = ([] if prefetch is None else [prefetch]) + [page_array] * n_page + list(weights)
    return pl.pallas_call(
        functools.partial(_compress_kernel, n_prefetch=nsp, n_page=n_page, transposed_out=transposed_out),
        grid_spec=grid_spec,
        out_shape=out_shape,
        compiler_params=_params(("parallel",)),
        name="compress",
    )(*args)


def _compress_weights(pe, w1, w2, head_stride):
    half = STRIDE * HEAD_DIM
    w1b = w1.astype(BF16)
    w2p = jnp.pad(w2, ((0, 0), (0, head_stride - HEAD_DIM)))
    w2blk = jnp.einsum("gh,kd->gkhd", jnp.eye(N_KV, dtype=F32), w2p)
    w2blk = w2blk.reshape(N_KV, CMP_HID, N_KV * head_stride).astype(BF16)
    return w1b, w1b[:half], w1b[half:], pe.reshape(1, L_CMP * HEAD_DIM), w2blk


def _kth_threshold(score, axis):
    shape = list(score.shape)
    shape[axis] = 1

    def body(_, carry):
        thr, cnt = carry
        cand = jnp.where(score < thr, score, -jnp.inf)
        mx = jnp.max(cand, axis=axis, keepdims=True)
        c = jnp.sum(jnp.where(score >= mx, 1.0, 0.0), axis=axis, keepdims=True)
        upd = cnt < N_SEL
        return jnp.where(upd, mx, thr), jnp.where(upd, c, cnt)

    thr, _ = lax.fori_loop(0, N_SEL, body, (jnp.full(shape, jnp.inf, F32), jnp.zeros(shape, F32)))
    return thr


TK = 1024
BIAS_ROWS = TK // L_SLC
N_WBLK = WINDOW // CHUNK + 1


def _attn_prompt_kernel(qt_ref, gt_ref, kcc_ref, vcc_ref, ks_ref, vs_ref, *refs):
    kw_refs, vw_refs = refs[:N_WBLK], refs[N_WBLK:2 * N_WBLK]
    (ovt_ref, tri_ref, dc_ref, dd_ref, wb_ref, o_ref,
     qaug_ref, sel_ref, m_ref, acc_ref, ob_ref, sw_ref) = refs[2 * N_WBLK:]
    i = pl.program_id(1)
    qs = i * CHUNK
    nq = CHUNK
    gt = gt_ref[...]

    def gate(g, branch):
        return jnp.concatenate([gt[6 * g + branch:6 * g + branch + 1],
                                gt[6 * g + 3 + branch:6 * g + 3 + branch + 1]], axis=1)

    cmask = dc_ref[...] <= qs
    ns = sel_ref.shape[1]
    n_idx = lax.broadcasted_iota(jnp.int32, (ns, nq), 0)
    cur = (qs + lax.broadcasted_iota(jnp.int32, (ns, nq), 1)) // L_SLC
    valid = n_idx <= cur
    forced = valid & ((n_idx == 0) | (n_idx >= cur - 1))
    for g in range(N_KV):
        base = 2 * g * HEAD_DIM
        qaug_ref[g, 0:HEAD_DIM, :] = jnp.concatenate(
            [qt_ref[base:base + HEAD_DIM, :], qt_ref[base + HEAD_DIM:base + 2 * HEAD_DIM, :]], axis=1)
        qaug_ref[g, HEAD_DIM:, :] = jnp.zeros((LANES - HEAD_DIM, 2 * nq), BF16)
        s = jnp.where(cmask, _dot(kcc_ref[:, g * LANES:(g + 1) * LANES], qaug_ref[g]), NEG)
        mx = jnp.max(s, axis=0, keepdims=True)
        mx = jnp.where(mx > 0.5 * NEG, mx, 0.0)
        p = jnp.exp2(s - mx)
        oc = _dot(vcc_ref[g], p.astype(BF16))
        inv = 1.0 / jnp.maximum(oc[HEAD_DIM:HEAD_DIM + 1], 1e-20)
        ob_ref[g] = gate(g, 0) * (oc[0:HEAD_DIM] * inv)
        imp = _split_dot_left(ovt_ref[...], p[:, :nq] * inv[:, :nq] + p[:, nq:] * inv[:, nq:])
        score = jnp.where(forced, FORCED_SCORE, jnp.where(valid, imp, -jnp.inf))
        thr = _kth_threshold(score, 0)
        above = score > thr
        tie = score == thr
        need = N_SEL - jnp.sum(jnp.where(above, 1.0, 0.0), axis=0, keepdims=True)
        rank = _dot(tri_ref[...], jnp.where(tie, 1.0, 0.0).astype(BF16))
        sel_ref[g] = jnp.where(above | (tie & (rank <= need)), 0.0, SEL_BIAS)
        m_ref[g] = jnp.full((1, 2 * nq), NEG, F32)
        acc_ref[g] = jnp.zeros((V_ROWS, 2 * nq), F32)

    def tile(kt, diagonal):
        k0 = pl.multiple_of(kt * TK, TK)
        b0 = pl.multiple_of(kt * BIAS_ROWS, BIAS_ROWS)
        for g in range(N_KV):
            bias = sel_ref[g, pl.ds(b0, BIAS_ROWS), :]
            qaug_ref[g, HEAD_DIM:HEAD_DIM + BIAS_ROWS, :] = jnp.concatenate([bias, bias], axis=1).astype(BF16)
            st = _dot(ks_ref[g, pl.ds(k0, TK), :], qaug_ref[g])
            if diagonal:
                st = jnp.where(dd_ref[...] <= qs - k0, st, NEG)
            m_old = m_ref[g]
            m_new = jnp.maximum(m_old, jnp.max(st, axis=0, keepdims=True))
            pt = jnp.exp2(st - m_new)
            acc_ref[g] = jnp.exp2(m_old - m_new) * acc_ref[g] + _dot(vs_ref[g, :, pl.ds(k0, TK)], pt.astype(BF16))
            m_ref[g] = m_new

    n_full = qs // TK

    def full_tile(kt, carry):
        tile(kt, False)
        return carry

    lax.fori_loop(0, n_full, full_tile, 0)
    tile(n_full, True)

    nw = N_WBLK * CHUNK
    outs = []
    for g in range(N_KV):
        acc = acc_ref[g]
        ob = ob_ref[g] + gate(g, 1) * (acc[0:HEAD_DIM] * (1.0 / acc[HEAD_DIM:HEAD_DIM + 1]))
        kwin = jnp.concatenate([r[g] for r in kw_refs], axis=0)
        sw_ref[...] = _dot(kwin, qaug_ref[g]) + wb_ref[...]

        @pl.when(i < WINDOW // CHUNK)
        def _():
            j_idx = lax.broadcasted_iota(jnp.int32, (nw, 2 * nq), 0)
            sw_ref[...] = jnp.where(j_idx >= WINDOW - qs, sw_ref[...], NEG)

        sw = sw_ref[...]
        pw = jnp.exp2(sw - jnp.max(sw, axis=0, keepdims=True))
        vwin = jnp.concatenate([r[g] for r in vw_refs], axis=1)
        ow = _dot(vwin, pw.astype(BF16))
        ob = ob + gate(g, 2) * (ow[0:HEAD_DIM] * (1.0 / ow[HEAD_DIM:HEAD_DIM + 1]))
        outs += [ob[:, :nq], ob[:, nq:]]
    o_ref[...] = jnp.concatenate(outs, axis=0).T.astype(o_ref.dtype)


def _attn_prompt(qt, gt, kcc, vcc, ks_aug, vs_aug, kw_aug, vw_aug, ovt, tri, dc, dd, wb):
    b, _, s = qt.shape
    nqb = s // CHUNK
    nc = kcc.shape[1]
    first = WINDOW // CHUNK
    whole = lambda shape: pl.BlockSpec((None,) + shape, lambda bi, i: (bi,) + (0,) * len(shape),
                                       pipeline_mode=pl.Buffered(1))
    const = lambda a: pl.BlockSpec(a.shape, lambda bi, i: (0,) * a.ndim)

    def kw_spec(j):
        return pl.BlockSpec((None, N_KV, CHUNK, LANES), lambda bi, i: (bi, 0, jnp.maximum(i - first + j, 0), 0))

    def vw_spec(j):
        return pl.BlockSpec((None, N_KV, V_ROWS, CHUNK), lambda bi, i: (bi, 0, 0, jnp.maximum(i - first + j, 0)))

    in_specs = ([pl.BlockSpec((None, N_HEADS * HEAD_DIM, CHUNK), lambda bi, i: (bi, 0, i)),
                 pl.BlockSpec((None, 32, CHUNK), lambda bi, i: (bi, 0, i)),
                 pl.BlockSpec((None, nc, N_KV * LANES), lambda bi, i: (bi, 0, 0)),
                 pl.BlockSpec((None, N_KV, V_ROWS, nc), lambda bi, i: (bi, 0, 0, 0)),
                 whole((N_KV, s, LANES)), whole((N_KV, V_ROWS, s))]
                + [kw_spec(j) for j in range(N_WBLK)] + [vw_spec(j) for j in range(N_WBLK)]
                + [const(ovt), const(tri), const(dc), const(dd), const(wb)])
    return pl.pallas_call(
        _attn_prompt_kernel,
        grid=(b, nqb),
        in_specs=in_specs,
        out_specs=pl.BlockSpec((None, CHUNK, N_HEADS * HEAD_DIM), lambda bi, i: (bi, i, 0)),
        out_shape=jax.ShapeDtypeStruct((b, s, N_HEADS * HEAD_DIM), BF16),
        scratch_shapes=[pltpu.VMEM((N_KV, LANES, 2 * CHUNK), BF16),
                        pltpu.VMEM((N_KV, s // L_SLC, CHUNK), F32),
                        pltpu.VMEM((N_KV, 1, 2 * CHUNK), F32),
                        pltpu.VMEM((N_KV, V_ROWS, 2 * CHUNK), F32),
                        pltpu.VMEM((N_KV, HEAD_DIM, 2 * CHUNK), F32),
                        pltpu.VMEM((N_WBLK * CHUNK, 2 * CHUNK), F32)],
        compiler_params=_params(("parallel", "arbitrary")),
        name="attn_prompt",
    )(qt, gt, kcc, vcc, ks_aug, vs_aug, *([kw_aug] * N_WBLK), *([vw_aug] * N_WBLK), ovt, tri, dc, dd, wb)


NQ_PAD = 8


def _softmax_rows(s, mask):
    s = jnp.where(mask, s, NEG)
    mx = jnp.max(s, axis=1, keepdims=True)
    p = jnp.where(mask, jnp.exp(s - mx), 0.0)
    return p / jnp.maximum(jnp.sum(p, axis=1, keepdims=True), 1e-20)


def _attn_sample_kernel(pt_ref, qbd_ref, g_ref, kcc_ref, vcc_ref, *refs, n_pages, past_len):
    kpages = refs[:n_pages]
    vpages = refs[n_pages:2 * n_pages]
    (kst_ref, vst_ref, kws_ref, vws_ref, kwt_ref, vwt_ref, ov_ref, triu_ref, eexp_ref,
     o_ref, s_scr) = refs[2 * n_pages:]
    del pt_ref
    qbd = qbd_ref[...]
    nrow = qbd.shape[0]
    page = kpages[0].shape[1]

    def t_of(shape):
        return past_len + (lax.broadcasted_iota(jnp.int32, shape, 0) & (NQ_PAD - 1))

    nc = kcc_ref.shape[0]
    s = _dot_nt(qbd, kcc_ref[...])
    c_idx = lax.broadcasted_iota(jnp.int32, (nrow, nc), 1)
    p_c = _softmax_rows(s, c_idx * STRIDE + (L_CMP - 1) <= t_of((nrow, nc)))
    o_c = _dot(p_c.astype(BF16), vcc_ref[...])

    half = nrow // 2
    imp = _split_dot_right(p_c[:half] + p_c[half:], ov_ref[...])
    ns = imp.shape[1]
    n_idx = lax.broadcasted_iota(jnp.int32, (half, ns), 1)
    cur = t_of((half, ns)) // L_SLC
    valid = n_idx <= cur
    forced = valid & ((n_idx == 0) | (n_idx >= cur - 1))
    score = jnp.where(forced, FORCED_SCORE, jnp.where(valid, imp, -jnp.inf))
    thr = _kth_threshold(score, 1)
    above = score > thr
    tie = score == thr
    need = N_SEL - jnp.sum(jnp.where(above, 1.0, 0.0), axis=1, keepdims=True)
    rank = _dot(jnp.where(tie, 1.0, 0.0).astype(BF16), triu_ref[...])
    sel = jnp.where(above | (tie & (rank <= need)), 1.0, 0.0).astype(BF16)
    sel_keys = _dot(jnp.concatenate([sel, sel], axis=0), eexp_ref[...])

    for pg in range(n_pages):
        s_scr[:, pg * page:(pg + 1) * page] = _dot(qbd, kpages[pg][...].astype(BF16))
    s_scr[:, n_pages * page:(n_pages + 1) * page] = _dot(qbd, kst_ref[...])
    nk = (n_pages + 1) * page
    tok = lax.broadcasted_iota(jnp.int32, (nrow, nk), 1)
    p_s = _softmax_rows(s_scr[...], (sel_keys > 0.5) & (tok <= t_of((nrow, nk))))
    o_s = _dot_nt(p_s[:, n_pages * page:].astype(BF16), vst_ref[...])
    for pg in range(n_pages):
        o_s = o_s + _dot_nt(p_s[:, pg * page:(pg + 1) * page].astype(BF16), vpages[pg][...].astype(BF16))

    wb = kws_ref.shape[1]
    nw = wb + page
    sw = jnp.concatenate([_dot(qbd, kws_ref[...].astype(BF16)), _dot(qbd, kwt_ref[...])], axis=1)
    rel = t_of((nrow, nw)) - (past_len - wb + lax.broadcasted_iota(jnp.int32, (nrow, nw), 1))
    p_w = _softmax_rows(sw, (rel >= 0) & (rel < WINDOW))
    o_w = (_dot_nt(p_w[:, :wb].astype(BF16), vws_ref[...].astype(BF16))
           + _dot_nt(p_w[:, wb:].astype(BF16), vwt_ref[...]))

    g = g_ref[...]
    o = g[:, 0:1] * o_c + g[:, 1:2] * o_s + g[:, 2:3] * o_w
    row_g = (lax.broadcasted_iota(jnp.int32, o.shape, 0) // NQ_PAD) & (N_KV - 1)
    lane_g = lax.broadcasted_iota(jnp.int32, o.shape, 1) // HEAD_DIM
    o = jnp.where(row_g == lane_g, o, 0.0)
    per_r = N_KV * NQ_PAD
    for r in range(2):
        acc = o[r * per_r:r * per_r + NQ_PAD]
        for gg in range(1, N_KV):
            acc = acc + o[r * per_r + gg * NQ_PAD:r * per_r + (gg + 1) * NQ_PAD]
        o_ref[r] = acc


def _attn_sample(page_table, qbd, gsm, kcc, vcc, cache_k, cache_v, k_tail, v_tail, kw_state, vw_state,
                 kw_tail, vw_tail, ov, triu, eexp, past_len):
    nb, n_pages = page_table.shape
    kvw = cache_k.shape[1]
    page = cache_k.shape[2]
    nrow = qbd.shape[1]
    per_b = lambda shape: pl.BlockSpec((None,) + shape, lambda b, pt: (b,) + (0,) * len(shape))
    const = lambda shape: pl.BlockSpec(shape, lambda b, pt: (0,) * len(shape))

    def page_spec(pg):
        return pl.BlockSpec((None, kvw, page), lambda b, pt: (pt[b, pg], 0, 0))

    in_specs = ([per_b((nrow, kvw)), per_b((nrow, 8)), per_b(kcc.shape[1:]), per_b(vcc.shape[1:])]
                + [page_spec(pg) for pg in range(n_pages)]
                + [page_spec(pg) for pg in range(n_pages)]
                + [per_b((kvw, page)), per_b((kvw, page)), per_b(kw_state.shape[1:]), per_b(vw_state.shape[1:]),
                   per_b((kvw, page)), per_b((kvw, page)), const(ov.shape), const(triu.shape), const(eexp.shape)])
    grid_spec = pltpu.PrefetchScalarGridSpec(
        num_scalar_prefetch=1,
        grid=(nb,),
        in_specs=in_specs,
        out_specs=pl.BlockSpec((None, 2, NQ_PAD, kvw), lambda b, pt: (b, 0, 0, 0)),
        scratch_shapes=[pltpu.VMEM((nrow, (n_pages + 1) * page), F32)],
    )
    return pl.pallas_call(
        functools.partial(_attn_sample_kernel, n_pages=n_pages, past_len=past_len),
        grid_spec=grid_spec,
        out_shape=jax.ShapeDtypeStruct((nb, 2, NQ_PAD, kvw), F32),
        compiler_params=_params(("parallel",)),
        name="attn_sample",
    )(page_table, qbd, gsm, kcc, vcc, *([cache_k] * n_pages), *([cache_v] * n_pages),
      k_tail, v_tail, kw_state, vw_state, kw_tail, vw_tail, ov, triu, eexp)


FF_SPLIT = 2


def _finish_kernel(x_ref, a_ref, b_ref, gt1_ref, sh2_ref, sc2_ref, gt2_ref, wout_ref, gffn_ref, win_ref, wo2_ref,
                   o_ref):
    half = a_ref.shape[1]
    y = _dot(a_ref[...], wout_ref[0:half, :]) + _dot(b_ref[...], wout_ref[half:, :])
    x1 = x_ref[...] + gt1_ref[...] * y
    ms = jnp.mean(x1 * x1, axis=-1, keepdims=True)
    h = x1 * lax.rsqrt(ms + EPS) * gffn_ref[...]
    hb = (h * (1.0 + sc2_ref[...]) + sh2_ref[...]).astype(BF16)
    d_ff = wo2_ref.shape[0]
    step = d_ff // FF_SPLIT
    acc = None
    for c in range(FF_SPLIT):
        up = _dot(hb, win_ref[:, c * step:(c + 1) * step])
        gate = _dot(hb, win_ref[:, d_ff + c * step:d_ff + (c + 1) * step])
        z = (jax.nn.silu(up) * gate).astype(BF16)
        part = _dot(z, wo2_ref[c * step:(c + 1) * step, :])
        acc = part if acc is None else acc + part
    o_ref[...] = x1 + gt2_ref[...] * acc


def _finish(x, a, b, mods, per_row_mod, rows_per_mod, wout, gffn, win, wo2, tm):
    r, d = x.shape
    if per_row_mod:
        mod_spec = pl.BlockSpec((tm, d), lambda i: (i, 0))
    else:
        tiles_per_mod = rows_per_mod // tm
        mod_spec = pl.BlockSpec((None, 1, d), lambda i: (i // tiles_per_mod, 0, 0))
    single = lambda shape: pl.BlockSpec(shape, lambda i: (0,) * len(shape), pipeline_mode=pl.Buffered(1))
    row = lambda n: pl.BlockSpec((tm, n), lambda i: (i, 0))
    return pl.pallas_call(
        _finish_kernel,
        grid=(r // tm,),
        in_specs=[row(d), row(a.shape[1]), row(b.shape[1]), mod_spec, mod_spec, mod_spec, mod_spec,
                  single(wout.shape), single(gffn.shape), single(win.shape), single(wo2.shape)],
        out_specs=row(d),
        out_shape=jax.ShapeDtypeStruct((r, d), F32),
        compiler_params=_params(("parallel",)),
        name="finish",
    )(x, a, b, *mods, wout, gffn, win, wo2)


def _rope_tables(pos):
    half = HEAD_DIM // 2
    inv = ROPE_THETA ** (-jnp.arange(half, dtype=F32) / half)
    ang = pos.astype(F32)[:, None] * inv[None, :]
    cos, sin = jnp.cos(ang), jnp.sin(ang)
    return jnp.concatenate([cos] * 4, axis=1), jnp.concatenate([-sin, sin, -sin, sin], axis=1)


def _overlap(n_c, n_s):
    c_start = jnp.arange(n_c) * STRIDE
    blk = jnp.arange(n_s)
    return ((c_start[:, None] < (blk[None, :] + 1) * L_SLC)
            & (c_start[:, None] + L_CMP > blk[None, :] * L_SLC)).astype(BF16)


def _tail_page(new_rows, page):
    return jnp.pad(new_rows.transpose(0, 2, 1), ((0, 0), (0, 0), (0, page - new_rows.shape[1]))).astype(BF16)


def kernel(x_prompt, x_sample, cache_k_cmp, cache_v_cmp, cache_k_slc, cache_v_slc, state_k_win, state_v_win,
           page_table, c_prompt, c_sample, w_ada, b_ada, g_mix_norm, g_ffn_norm, w_in, g_sgu, w_sgu, b_sgu,
           g_q, g_k_cmp, g_k_slc, g_k_win, pe_k_cmp, pe_v_cmp, w_ck1, w_ck2, w_cv1, w_cv2, w_out, w_ffn_in,
           w_ffn_out):
    depth = w_in.shape[0]
    assert depth == 1, "single trunk layer"
    nb_p, seq, d = x_prompt.shape
    nb_s, n_new, _ = x_sample.shape
    n_pool, page = cache_k_cmp.shape[1], cache_k_cmp.shape[2]
    kvw = N_KV * HEAD_DIM
    n_pages = page_table.shape[1]
    past_len = n_pages * page
    wb_s = state_k_win.shape[2]
    l = 0

    in_cols = w_in.shape[2]
    w_in_pad = jnp.pad(w_in[l], ((0, 0), (0, 3200 - in_cols))).astype(BF16)
    tile_gain = lambda g, heads: jnp.tile(g, heads).reshape(1, heads * HEAD_DIM)
    gq_t, gkc_t = tile_gain(g_q[l], N_HEADS), tile_gain(g_k_cmp[l], N_KV)
    gks_t, gkw_t = tile_gain(g_k_slc[l], N_KV), tile_gain(g_k_win[l], N_KV)
    gsgu = g_sgu[l].reshape(1, -1)
    gmix = g_mix_norm[l].reshape(1, d)
    gffn = g_ffn_norm[l].reshape(1, d)
    gmat = (jnp.kron(jnp.eye(256 // HEAD_DIM, dtype=F32), jnp.ones((HEAD_DIM, HEAD_DIM), F32)) / HEAD_DIM).astype(BF16)
    w_tril = jnp.where(jnp.tril(jnp.ones((CHUNK, CHUNK), bool)), w_sgu[l], 0)
    wmix_p = w_tril.astype(BF16)
    bmix_p = jnp.repeat(b_sgu[l].T, HEAD_DIM, axis=1)
    eye_s = jnp.eye(CHUNK // n_new, dtype=F32)
    wmix_s = jax.vmap(lambda w: jnp.kron(eye_s, w[:n_new, :n_new]))(w_tril).astype(BF16)
    bmix_s = jnp.tile(jnp.repeat(b_sgu[l].T[:n_new], HEAD_DIM, axis=1), (CHUNK // n_new, 1))
    wout_b = w_out[l].astype(BF16)
    win_b = w_ffn_in[l].astype(BF16)
    wo2_b = w_ffn_out[l].astype(BF16)

    n_c = nb_p + nb_s
    n_c_pad = -(-n_c // 8) * 8
    c_all = jnp.pad(jnp.concatenate([c_prompt, c_sample], axis=0), ((0, n_c_pad - n_c), (0, 0)))
    ada = _ada(c_all, w_ada[l].astype(BF16), b_ada[l].reshape(1, -1))
    mods_p = [m.reshape(nb_p, 1, d) for m in jnp.split(ada[:nb_p], 6, axis=-1)]
    mods_s = [jnp.repeat(m, n_new, axis=0) for m in jnp.split(ada[nb_p:n_c], 6, axis=-1)]

    cos_p, sin_p = _rope_tables(jnp.arange(seq))
    cos_s, sin_s = _rope_tables(past_len + jnp.arange(n_new))
    reps = CHUNK // n_new
    cos_s, sin_s = jnp.tile(cos_s, (reps, 1)), jnp.tile(sin_s, (reps, 1))
    xp = x_prompt.reshape(nb_p * seq, d)
    xs = x_sample.reshape(nb_s * n_new, d)
    gains = (gsgu, gq_t, gkc_t, gks_t, gkw_t)
    blk_of_key = (jnp.arange(seq) // L_SLC) % BIAS_ROWS
    onehot = jnp.pad(jax.nn.one_hot(blk_of_key, LANES - HEAD_DIM, dtype=F32), ((0, 0), (HEAD_DIM, 0)))
    q_scale = HEAD_DIM ** -0.5 * math.log2(math.e)
    (a_p, vn_p, qt, gt, kct, vct, kst, vst, kwt, vwt, ks_aug, vs_aug, kw_aug, vw_aug) = _proj_cols(
        xp, nb_p, seq, mods_p[0], mods_p[1], gmix, w_in_pad, cos_p, sin_p, onehot, *gains, wmix_p, bmix_p, gmat,
        256, q_scale)
    (a_s, vn_s, q_s, kc_s, vc_s, ks_s, vs_s, kw_s, vw_s, gate_s) = _proj_rows(
        xs, mods_s[0], mods_s[1], gmix, w_in_pad, cos_s, sin_s, *gains, wmix_s, bmix_s, gmat, CHUNK)

    n_chunk_p = seq // STRIDE
    n_blk_p = seq // L_SLC
    wk_p = _compress_weights(pe_k_cmp[l], w_ck1[l], w_ck2[l], LANES)
    wv_p = _compress_weights(pe_v_cmp[l], w_cv1[l], w_cv2[l], LANES)
    pages_p = [pl.BlockSpec((None, kvw, LANES), functools.partial(lambda pg, i: (i, 0, pg), pg))
               for pg in range(seq // LANES)]
    kcc_p = _compress_call(kct, pages_p, LANES, (nb_p,), wk_p)
    vcc_p = _compress_call(vct, pages_p, LANES, (nb_p,), wv_p, transposed_out=True)
    ovt = _overlap(n_chunk_p, n_blk_p).T
    tri = jnp.tril(jnp.ones((n_blk_p, n_blk_p), BF16))
    q_lane = jnp.arange(2 * CHUNK)[None, :] % CHUNK
    dc = (jnp.arange(n_chunk_p)[:, None] * STRIDE + (L_CMP - 1) - q_lane).astype(jnp.int32)
    dd = (jnp.arange(TK)[:, None] - q_lane).astype(jnp.int32)
    jq = jnp.arange(N_WBLK * CHUNK)[:, None] - q_lane
    wb = jnp.where((jq > 0) & (jq <= WINDOW), 0.0, NEG).astype(F32)
    b_p = _attn_prompt(qt, gt, kcc_p, vcc_p, ks_aug, vs_aug, kw_aug, vw_aug, ovt, tri, dc, dd, wb)
    y_p = _finish(xp, a_p, b_p.reshape(nb_p * seq, N_HEADS * HEAD_DIM), mods_p[2:], False, seq, wout_b, gffn,
                  win_b, wo2_b, 256)

    fm = lambda a: a.transpose(0, 2, 3, 1).reshape(a.shape[0], kvw, a.shape[1])
    ck, cv, cks, cvs = fm(cache_k_cmp[l]), fm(cache_v_cmp[l]), fm(cache_k_slc[l]), fm(cache_v_slc[l])
    wk_s = _compress_weights(pe_k_cmp[l], w_ck1[l], w_ck2[l], HEAD_DIM)
    wv_s = _compress_weights(pe_v_cmp[l], w_cv1[l], w_cv2[l], HEAD_DIM)
    spb = (seq // LANES) // n_pages
    assert nb_s % spb == 0 and page == LANES

    def page_spec(j):
        return pl.BlockSpec((None, kvw, page), lambda i, pt: (pt[i * spb + j // n_pages, j % n_pages], 0, 0))

    pages_s = [page_spec(j) for j in range(spb * n_pages)]
    n_c_s = past_len // STRIDE
    kcc_s = _compress_call(ck, pages_s, page, (nb_s // spb,), wk_s, prefetch=page_table).reshape(nb_s, n_c_s, kvw)
    vcc_s = _compress_call(cv, pages_s, page, (nb_s // spb,), wv_s, prefetch=page_table).reshape(nb_s, n_c_s, kvw)

    q5 = (q_s * (HEAD_DIM ** -0.5)).reshape(nb_s, n_new, N_KV, 2, HEAD_DIM).transpose(0, 3, 2, 1, 4)
    q5 = jnp.pad(q5, ((0, 0), (0, 0), (0, 0), (0, NQ_PAD - n_new), (0, 0)))
    qbd = jnp.einsum("brgqd,gh->brgqhd", q5, jnp.eye(N_KV, dtype=F32))
    qbd = qbd.reshape(nb_s, 2 * N_KV * NQ_PAD, kvw).astype(BF16)
    g5 = gate_s[:, :3 * N_HEADS].reshape(nb_s, n_new, N_KV, 2, 3).transpose(0, 3, 2, 1, 4)
    g5 = jnp.pad(g5, ((0, 0), (0, 0), (0, 0), (0, NQ_PAD - n_new), (0, 5)))
    gsm = g5.reshape(nb_s, 2 * N_KV * NQ_PAD, 8)
    new = lambda a: a.reshape(nb_s, n_new, kvw)
    n_keys = (n_pages + 1) * page
    ov_s = _overlap(n_c_s, LANES)
    triu = jnp.triu(jnp.ones((LANES, LANES), BF16))
    eexp = (jnp.arange(n_keys)[None, :] // L_SLC == jnp.arange(LANES)[:, None]).astype(BF16)
    os_ = _attn_sample(page_table, qbd, gsm, kcc_s, vcc_s, cks, cvs, _tail_page(new(ks_s), page),
                       _tail_page(new(vs_s), page), fm(state_k_win[l]), fm(state_v_win[l]),
                       _tail_page(new(kw_s), page), _tail_page(new(vw_s), page), ov_s, triu, eexp, past_len)
    b_s = os_[:, :, :n_new].reshape(nb_s, 2, n_new, N_KV, HEAD_DIM).transpose(0, 2, 3, 1, 4)
    b_s = b_s.reshape(nb_s * n_new, N_HEADS * HEAD_DIM).astype(BF16)

    y_s = _finish(xs, a_s, b_s, mods_s[2:], True, 0, wout_b, gffn, win_b, wo2_b, CHUNK)

    wb_p = min(WINDOW, seq)
    assert seq - ((seq - 1) // CHUNK) * CHUNK == CHUNK, "the prompt ends on a full chunk"
    from_cols = lambda t: t.reshape(1, nb_p, N_KV, HEAD_DIM, seq).transpose(0, 1, 4, 2, 3)
    outs_p = [from_cols(kct), from_cols(vct), from_cols(kst), from_cols(vst),
              from_cols(kwt)[:, :, seq - wb_p:], from_cols(vwt)[:, :, seq - wb_p:], vn_p[None]]
    kv5 = lambda a, nb, t: a.reshape(1, nb, t, N_KV, HEAD_DIM)
    kw_all = jnp.concatenate([state_k_win[l], kv5(kw_s, nb_s, n_new)[0]], axis=1)
    vw_all = jnp.concatenate([state_v_win[l], kv5(vw_s, nb_s, n_new)[0]], axis=1)
    outs_s = [kv5(kc_s, nb_s, n_new), kv5(vc_s, nb_s, n_new), kv5(ks_s, nb_s, n_new), kv5(vs_s, nb_s, n_new),
              kw_all[None, :, n_new:], vw_all[None, :, n_new:], vn_s.reshape(1, nb_s, n_new, -1)]
    return (y_p.reshape(nb_p, seq, d), y_s.reshape(nb_s, n_new, d), *outs_p, *outs_s)
```

```python
import functools
import math

import jax
import jax.numpy as jnp
from jax import lax
from jax.experimental import pallas as pl
from jax.experimental.pallas import tpu as pltpu

F32 = jnp.float32
BF16 = jnp.bfloat16

CHUNK = 128
A_GROUPS = 8
HEAD_DIM = 64
N_HEADS = 8
N_KV = 4
L_CMP = 32
STRIDE = 16
CMP_HID = 256
L_SLC = 64
N_SEL = 16
WINDOW = 512
ROPE_THETA = 10000.0
EPS = 1e-6
FORCED_SCORE = 1e4
NEG = -1e30
SEL_BIAS = -(2.0 ** 100)

LANES = 128
V_ROWS = HEAD_DIM + 16
VMEM_LIMIT = 52 * 1024 * 1024

_NT = (((1,), (1,)), ((), ()))


def _dot(a, b):
    return jnp.dot(a, b, preferred_element_type=F32)


def _dot_nt(a, b):
    return lax.dot_general(a, b, _NT, preferred_element_type=F32)


def _split_dot_left(coef, x):
    hi = x.astype(BF16)
    lo = (x - hi.astype(F32)).astype(BF16)
    return _dot(coef, hi) + _dot(coef, lo)


def _split_dot_right(x, coef):
    hi = x.astype(BF16)
    lo = (x - hi.astype(F32)).astype(BF16)
    return _dot(hi, coef) + _dot(lo, coef)


def _params(sem, flags=None):
    return pltpu.CompilerParams(dimension_semantics=sem, vmem_limit_bytes=VMEM_LIMIT, flags=flags)


def _with_ones_row(vt):
    n = vt.shape[1]
    row = lax.broadcasted_iota(jnp.int32, (V_ROWS - HEAD_DIM, n), 0)
    return jnp.concatenate([vt, jnp.where(row == 0, 1.0, 0.0).astype(vt.dtype)], axis=0)


def _ada_kernel(c_ref, w_ref, b_ref, o_ref):
    c = c_ref[...]
    o_ref[...] = _dot(jax.nn.silu(c).astype(BF16), w_ref[...]) + b_ref[...]


def _ada(c, w, b):
    m, k = c.shape
    n = w.shape[1]
    tn = 1024
    return pl.pallas_call(
        _ada_kernel,
        grid=(n // tn,),
        in_specs=[pl.BlockSpec((m, k), lambda j: (0, 0)),
                  pl.BlockSpec((k, tn), lambda j: (0, j)),
                  pl.BlockSpec((1, tn), lambda j: (0, j))],
        out_specs=pl.BlockSpec((m, tn), lambda j: (0, j)),
        out_shape=jax.ShapeDtypeStruct((m, n), F32),
        compiler_params=_params(("parallel",)),
        name="ada",
    )(c, w, b)


def _group_mean_sq(y, g_ref):
    y2 = y * y
    cols = []
    for c in range(y.shape[1] // 256):
        cols.append(_split_dot_right(y2[:, 256 * c:256 * (c + 1)], g_ref[...]))
    return cols[0] if len(cols) == 1 else jnp.concatenate(cols, axis=1)


def _rope(x, cos, sin):
    n = x.shape[1]
    reps = n // LANES
    cos_t = cos if reps == 1 else jnp.concatenate([cos] * reps, axis=1)
    sin_t = sin if reps == 1 else jnp.concatenate([sin] * reps, axis=1)
    lane = lax.broadcasted_iota(jnp.int32, x.shape, 1)
    first_half = (lane & (HEAD_DIM - 1)) < (HEAD_DIM // 2)
    partner = jnp.where(first_half, pltpu.roll(x, n - HEAD_DIM // 2, 1), pltpu.roll(x, HEAD_DIM // 2, 1))
    return x * cos_t + partner * sin_t


def _proj_common(x_ref, sh_ref, sc_ref, gmix_ref, w_ref, cos_ref, sin_ref, gsgu_ref, gq_ref, gkc_ref, gks_ref,
                 gkw_ref, wmix_ref, bmix_ref, gmat_ref, a_ref, vn_ref):
    x = x_ref[...]
    tm = x.shape[0]
    ms = jnp.mean(x * x, axis=-1, keepdims=True)
    h = x * lax.rsqrt(ms + EPS) * gmix_ref[...]
    h = h * (1.0 + sc_ref[...]) + sh_ref[...]
    hb = h.astype(BF16)
    cos = cos_ref[...]
    sin = sin_ref[...]

    def seg(lo, hi):
        return _dot(hb, w_ref[:, lo:hi])

    def head_norm(y, g_ref):
        return y * lax.rsqrt(_group_mean_sq(y, gmat_ref) + EPS) * g_ref[...]

    u = jax.nn.gelu(seg(0, 512))
    v = jax.nn.gelu(seg(512, 1024))
    vn = head_norm(v, gsgu_ref)
    vn_ref[...] = vn[tm - CHUNK:]
    vb = vn.astype(BF16)
    lane = lax.broadcasted_iota(jnp.int32, (CHUNK, LANES), 1)
    low = lane < HEAD_DIM
    for ck in range(tm // CHUNK):
        rows = slice(ck * CHUNK, (ck + 1) * CHUNK)
        for pr in range(A_GROUPS // 2):
            cols = slice(pr * LANES, (pr + 1) * LANES)
            vp = vb[rows, cols]
            mixed = jnp.where(low, _dot(wmix_ref[2 * pr], vp), _dot(wmix_ref[2 * pr + 1], vp))
            mixed = mixed + bmix_ref[:, cols]
            a_ref[rows, cols] = (u[rows, cols] * mixed).astype(a_ref.dtype)

    q = _rope(head_norm(seg(1024, 1536), gq_ref), cos, sin)
    kc = _rope(head_norm(seg(1536, 1792), gkc_ref), cos, sin)
    vc = seg(1792, 2048)
    ks = _rope(head_norm(seg(2048, 2304), gks_ref), cos, sin)
    vs = seg(2304, 2560)
    kw = _rope(head_norm(seg(2560, 2816), gkw_ref), cos, sin)
    vw = seg(2816, 3072)
    gates = jax.nn.sigmoid(seg(3072, 3200))
    return q, kc, vc, ks, vs, kw, vw, gates


def _proj_rows_kernel(*refs):
    ins, (a_ref, vn_ref, q_ref, kc_ref, vc_ref, ks_ref, vs_ref, kw_ref, vw_ref, gate_ref) = refs[:15], refs[15:]
    outs = _proj_common(*ins, a_ref, vn_ref)
    for ref, val in zip((q_ref, kc_ref, vc_ref, ks_ref, vs_ref, kw_ref, vw_ref, gate_ref), outs):
        ref[...] = val


def _proj_cols_kernel(*refs, q_scale):
    ins, oh_ref = refs[:15], refs[15]
    (a_ref, vn_ref, qt_ref, gt_ref, kct_ref, vct_ref, kst_ref, vst_ref, kwt_ref, vwt_ref,
     ksa_ref, vsa_ref, kwa_ref, vwa_ref, stage_ref) = refs[16:]
    q, kc, vc, ks, vs, kw, vw, gates = _proj_common(*ins, a_ref, vn_ref)
    tm = q.shape[0]
    qt_ref[...] = (q * q_scale).T.astype(BF16)
    gt_ref[...] = gates.T[0:gt_ref.shape[0]]
    kct_ref[...] = kc.T
    kst_ref[...] = ks.T
    kwt_ref[...] = kw.T
    stage_ref[0] = vc
    stage_ref[1] = vs
    stage_ref[2] = vw
    vct_ref[...] = stage_ref[0].T
    vst = stage_ref[1].T
    vwt = stage_ref[2].T
    vst_ref[...] = vst
    vwt_ref[...] = vwt
    low = lax.broadcasted_iota(jnp.int32, (tm, LANES), 1) < HEAD_DIM
    onehot = oh_ref[...]
    for g in range(N_KV):
        pair = slice((g // 2) * LANES, (g // 2 + 1) * LANES)
        ks_g, kw_g = ks[:, pair], kw[:, pair]
        if g % 2 == 1:
            ks_g, kw_g = pltpu.roll(ks_g, HEAD_DIM, 1), pltpu.roll(kw_g, HEAD_DIM, 1)
        ksa_ref[g] = jnp.where(low, ks_g, onehot).astype(BF16)
        kwa_ref[g] = jnp.where(low, kw_g, 0.0).astype(BF16)
        rows = slice(g * HEAD_DIM, (g + 1) * HEAD_DIM)
        vsa_ref[g] = _with_ones_row(vst[rows]).astype(BF16)
        vwa_ref[g] = _with_ones_row(vwt[rows]).astype(BF16)


def _proj_in_specs(tm, d, per_row_mod, rows_per_mod, rope_tiles, w_pad, wmix, bmix, gmat):
    if per_row_mod:
        mod_spec = pl.BlockSpec((tm, d), lambda i: (i, 0))
    else:
        tiles_per_mod = rows_per_mod // tm
        mod_spec = pl.BlockSpec((None, 1, d), lambda i: (i // tiles_per_mod, 0, 0))
    const = lambda shape: pl.BlockSpec(shape, lambda i: (0,) * len(shape))
    rope_spec = pl.BlockSpec((tm, LANES), lambda i: (i % rope_tiles, 0))
    return [pl.BlockSpec((tm, d), lambda i: (i, 0)), mod_spec, mod_spec, const((1, d)), const(w_pad.shape),
            rope_spec, rope_spec,
            const((1, 512)), const((1, 512)), const((1, 256)), const((1, 256)), const((1, 256)),
            const(wmix.shape), const(bmix.shape), const(gmat.shape)]


def _proj_rows(x, sh, sc, gmix, w_pad, cos, sin, gsgu, gq, gkc, gks, gkw, wmix, bmix, gmat, tm):
    r, d = x.shape
    row = lambda n: pl.BlockSpec((tm, n), lambda i: (i, 0))
    out_widths = [512, 512, 512, 256, 256, 256, 256, 256, 256, 128]
    out_dtypes = [BF16] + [F32] * 9
    return pl.pallas_call(
        _proj_rows_kernel,
        grid=(r // tm,),
        in_specs=_proj_in_specs(tm, d, True, 0, cos.shape[0] // tm, w_pad, wmix, bmix, gmat),
        out_specs=[row(n) for n in out_widths],
        out_shape=[jax.ShapeDtypeStruct((r, n), dt) for n, dt in zip(out_widths, out_dtypes)],
        compiler_params=_params(("parallel",)),
        name="proj_rows",
    )(x, sh, sc, gmix, w_pad, cos, sin, gsgu, gq, gkc, gks, gkw, wmix, bmix, gmat)


def _proj_cols(x, nb, seq, sh, sc, gmix, w_pad, cos, sin, onehot, gsgu, gq, gkc, gks, gkw, wmix, bmix, gmat, tm,
               q_scale):
    r, d = x.shape
    tpb = seq // tm
    kvw = N_KV * HEAD_DIM
    bi = lambda i: i // tpb
    ti = lambda i: i % tpb
    row = lambda n: pl.BlockSpec((tm, n), lambda i: (i, 0))
    colt = lambda n: pl.BlockSpec((None, n, tm), lambda i: (bi(i), 0, ti(i)))
    out_specs = [row(512),
                 pl.BlockSpec((None, CHUNK, 512), lambda i: (bi(i), 0, 0)),
                 colt(512), colt(32),
                 colt(kvw), colt(kvw), colt(kvw), colt(kvw), colt(kvw), colt(kvw),
                 pl.BlockSpec((None, N_KV, tm, LANES), lambda i: (bi(i), 0, ti(i), 0)),
                 pl.BlockSpec((None, N_KV, V_ROWS, tm), lambda i: (bi(i), 0, 0, ti(i))),
                 pl.BlockSpec((None, N_KV, tm, LANES), lambda i: (bi(i), 0, ti(i), 0)),
                 pl.BlockSpec((None, N_KV, V_ROWS, tm), lambda i: (bi(i), 0, 0, ti(i)))]
    sds = jax.ShapeDtypeStruct
    out_shape = [sds((r, 512), BF16), sds((nb, CHUNK, 512), F32), sds((nb, 512, seq), BF16),
                 sds((nb, 32, seq), F32)] + [sds((nb, kvw, seq), F32)] * 6 + [
                 sds((nb, N_KV, seq, LANES), BF16), sds((nb, N_KV, V_ROWS, seq), BF16),
                 sds((nb, N_KV, seq, LANES), BF16), sds((nb, N_KV, V_ROWS, seq), BF16)]
    in_specs = _proj_in_specs(tm, d, False, seq, tpb, w_pad, wmix, bmix, gmat)
    in_specs.append(pl.BlockSpec((tm, LANES), lambda i: (ti(i), 0)))
    return pl.pallas_call(
        functools.partial(_proj_cols_kernel, q_scale=q_scale),
        grid=(r // tm,),
        in_specs=in_specs,
        out_specs=out_specs,
        out_shape=out_shape,
        scratch_shapes=[pltpu.VMEM((3, tm, kvw), F32)],
        compiler_params=_params(("arbitrary",)),
        name="proj_cols",
    )(x, sh, sc, gmix, w_pad, cos, sin, gsgu, gq, gkc, gks, gkw, wmix, bmix, gmat, onehot)


def _compress_kernel(*refs, n_prefetch, n_page, transposed_out):
    refs = refs[n_prefetch:]
    page_refs = refs[:n_page]
    (w1_ref, w1a_ref, w1b_ref, pe_ref, w2_ref, o_ref,
     xs_ref, lhs_ref, a_ref, b_ref, hid_ref, stage_ref) = refs[n_page:]
    page = page_refs[0].shape[1]
    m = n_page * page // STRIDE
    for pg in range(n_page):
        xt = page_refs[pg][...].T
        for pp in range(N_KV // 2):
            xs_ref[pp, pg * page:(pg + 1) * page, :] = xt[:, pp * LANES:(pp + 1) * LANES]
    low = lax.broadcasted_iota(jnp.int32, (m, LANES), 1) < HEAD_DIM
    for rr in range(STRIDE // 2):
        cols = slice(rr * LANES, (rr + 1) * LANES)
        for pp in range(N_KV // 2):
            p0 = xs_ref[pp, pl.ds(2 * rr, m, stride=STRIDE), :]
            p1 = xs_ref[pp, pl.ds(2 * rr + 1, m, stride=STRIDE), :]
            r0 = pltpu.roll(p0, HEAD_DIM, 1)
            r1 = pltpu.roll(p1, HEAD_DIM, 1)
            g0 = 2 * pp
            lhs_ref[g0 * m:(g0 + 1) * m, cols] = jnp.where(low, p0, r1).astype(BF16)
            lhs_ref[(g0 + 1) * m:(g0 + 2) * m, cols] = jnp.where(low, r0, p1).astype(BF16)
    pe_term = _dot(jnp.broadcast_to(pe_ref[...], (8, pe_ref.shape[1])).astype(BF16), w1_ref[...])[0:1]
    rows = N_KV * m
    a_ref[...] = _dot(lhs_ref[...], w1a_ref[...])
    b_ref[0:rows, :] = _dot(lhs_ref[...], w1b_ref[...])
    b_ref[rows:rows + 8, :] = jnp.zeros((8, CMP_HID), F32)
    hid_ref[...] = jax.nn.gelu(a_ref[...] + b_ref[pl.ds(1, rows), :] + pe_term).astype(BF16)
    out = _dot(hid_ref[0:m, :], w2_ref[0])
    for g in range(1, N_KV):
        out = out + _dot(hid_ref[g * m:(g + 1) * m, :], w2_ref[g])
    if transposed_out:
        stage_ref[...] = out
        out_t = stage_ref[...].T
        for g in range(N_KV):
            o_ref[g] = _with_ones_row(out_t[g * LANES:g * LANES + HEAD_DIM]).astype(o_ref.dtype)
    else:
        o_ref[...] = out.astype(o_ref.dtype)


def _compress_call(page_array, page_specs, page, grid, weights, prefetch=None, transposed_out=False):
    n_page = len(page_specs)
    m = n_page * page // STRIDE
    n_out = weights[-1].shape[2]
    nsp = 0 if prefetch is None else 1
    const = lambda shape: pl.BlockSpec(shape, lambda *a: (0,) * len(shape))
    if transposed_out:
        out_spec = pl.BlockSpec((None, N_KV, V_ROWS, m), lambda i, *a: (i, 0, 0, 0))
        out_shape = jax.ShapeDtypeStruct((grid[0], N_KV, V_ROWS, m), BF16)
    else:
        out_spec = pl.BlockSpec((None, m, n_out), lambda i, *a: (i, 0, 0))
        out_shape = jax.ShapeDtypeStruct((grid[0], m, n_out), BF16)
    grid_spec = pltpu.PrefetchScalarGridSpec(
        num_scalar_prefetch=nsp,
        grid=grid,
        in_specs=list(page_specs) + [const(w.shape) for w in weights],
        out_specs=out_spec,
        scratch_shapes=[pltpu.VMEM((N_KV // 2, n_page * page, LANES), F32),
                        pltpu.VMEM((N_KV * m, STRIDE * HEAD_DIM), BF16),
                        pltpu.VMEM((N_KV * m, CMP_HID), F32),
                        pltpu.VMEM((N_KV * m + 8, CMP_HID), F32),
                        pltpu.VMEM((N_KV * m, CMP_HID), BF16),
                        pltpu.VMEM((m, n_out), F32)],
    )
    args = ([] if prefetch is None else [prefetch]) + [page_array] * n_page + list(weights)
    return pl.pallas_call(
        functools.partial(_compress_kernel, n_prefetch=nsp, n_page=n_page, transposed_out=transposed_out),
        grid_spec=grid_spec,
        out_shape=out_shape,
        compiler_params=_params(("parallel",)),
        name="compress",
    )(*args)


def _compress_weights(pe, w1, w2, head_stride):
    half = STRIDE * HEAD_DIM
    w1b = w1.astype(BF16)
    w2p = jnp.pad(w2, ((0, 0), (0, head_stride - HEAD_DIM)))
    w2blk = jnp.einsum("gh,kd->gkhd", jnp.eye(N_KV, dtype=F32), w2p)
    w2blk = w2blk.reshape(N_KV, CMP_HID, N_KV * head_stride).astype(BF16)
    return w1b, w1b[:half], w1b[half:], pe.reshape(1, L_CMP * HEAD_DIM), w2blk


def _kth_threshold(score, axis):
    shape = list(score.shape)
    shape[axis] = 1

    def body(_, carry):
        thr, cnt = carry
        cand = jnp.where(score < thr, score, -jnp.inf)
        mx = jnp.max(cand, axis=axis, keepdims=True)
        c = jnp.sum(jnp.where(score >= mx, 1.0, 0.0), axis=axis, keepdims=True)
        upd = cnt < N_SEL
        return jnp.where(upd, mx, thr), jnp.where(upd, c, cnt)

    thr, _ = lax.fori_loop(0, N_SEL, body, (jnp.full(shape, jnp.inf, F32), jnp.zeros(shape, F32)))
    return thr


TK = 1024
BIAS_ROWS = TK // L_SLC
N_WBLK = WINDOW // CHUNK + 1


def _attn_prompt_kernel(qt_ref, gt_ref, kcc_ref, vcc_ref, ks_ref, vs_ref, *refs):
    kw_refs, vw_refs = refs[:N_WBLK], refs[N_WBLK:2 * N_WBLK]
    (ovt_ref, tri_ref, dc_ref, dd_ref, wb_ref, o_ref,
     qaug_ref, sel_ref, m_ref, acc_ref, ob_ref, sc_ref, s_ref, p_ref, tmax_ref, alpha_ref, hi_ref,
     lo_ref) = refs[2 * N_WBLK:]
    i = pl.program_id(1)
    qs = i * CHUNK
    nq = CHUNK
    gt = gt_ref[...]

    def gate(g, branch):
        return jnp.concatenate([gt[6 * g + branch:6 * g + branch + 1],
                                gt[6 * g + 3 + branch:6 * g + 3 + branch + 1]], axis=1)

    def staged(stages, n):
        depth = len(stages)
        for step in range(n + depth - 1):
            for k, stage in enumerate(stages):
                if 0 <= step - k < n:
                    stage(step - k)

    nc = kcc_ref.shape[0]
    ns = sel_ref.shape[1]
    n_idx = lax.broadcasted_iota(jnp.int32, (ns, nq), 0)
    cur = (qs + lax.broadcasted_iota(jnp.int32, (ns, nq), 1)) // L_SLC
    valid = n_idx <= cur
    forced = valid & ((n_idx == 0) | (n_idx >= cur - 1))

    def cmp_scores(g):
        base = 2 * g * HEAD_DIM
        qaug_ref[g, 0:HEAD_DIM, :] = jnp.concatenate(
            [qt_ref[base:base + HEAD_DIM, :], qt_ref[base + HEAD_DIM:base + 2 * HEAD_DIM, :]], axis=1)
        qaug_ref[g, HEAD_DIM:, :] = jnp.zeros((LANES - HEAD_DIM, 2 * nq), BF16)
        s = jnp.where(dc_ref[...] <= qs, _dot(kcc_ref[:, g * LANES:(g + 1) * LANES], qaug_ref[g]), NEG)
        s_ref[g % 2, 0:nc, :] = s
        mx = jnp.max(s, axis=0, keepdims=True)
        tmax_ref[g % 2] = jnp.where(mx > 0.5 * NEG, mx, 0.0)
        m_ref[g] = jnp.full((1, 2 * nq), NEG, F32)
        acc_ref[g] = jnp.zeros((V_ROWS, 2 * nq), F32)

    def cmp_exps(g):
        p = jnp.exp2(s_ref[g % 2, 0:nc, :] - tmax_ref[g % 2])
        inv = 1.0 / jnp.maximum(jnp.sum(p, axis=0, keepdims=True), 1e-20)
        alpha_ref[g % 2] = inv
        p_ref[g % 2, 0:nc, :] = p.astype(BF16)
        psum = p[:, :nq] * inv[:, :nq] + p[:, nq:] * inv[:, nq:]
        hi = psum.astype(BF16)
        hi_ref[g % 2] = hi
        lo_ref[g % 2] = (psum - hi.astype(F32)).astype(BF16)

    def cmp_values(g):
        oc = _dot(vcc_ref[g], p_ref[g % 2, 0:nc, :])
        ob_ref[g] = gate(g, 0) * (oc[0:HEAD_DIM] * alpha_ref[g % 2])
        imp = _dot(ovt_ref[...], hi_ref[g % 2]) + _dot(ovt_ref[...], lo_ref[g % 2])
        sc_ref[:, g * nq:(g + 1) * nq] = jnp.where(forced, FORCED_SCORE, jnp.where(valid, imp, -jnp.inf))

    staged([cmp_scores, cmp_exps, cmp_values], N_KV)

    score = sc_ref[...]
    thr = _kth_threshold(score, 0)
    above = score > thr
    tie = score == thr
    need = N_SEL - jnp.sum(jnp.where(above, 1.0, 0.0), axis=0, keepdims=True)
    rank = _dot(tri_ref[...], jnp.where(tie, 1.0, 0.0).astype(BF16))
    before = lax.broadcasted_iota(jnp.int32, score.shape, 0) < 2 * i
    sel_bias = jnp.where((above | (tie & (rank <= need))) & before, 0.0, SEL_BIAS)
    for g in range(N_KV):
        sel_ref[g] = sel_bias[:, g * nq:(g + 1) * nq]

    last_tile = ks_ref.shape[1] // TK - 1
    half = TK // 2

    def scores_half(kt, g, buf, h):
        kc = jnp.clip(kt, 0, last_tile)
        k0 = pl.multiple_of(kc * TK, TK)
        if h == 0:
            bias = sel_ref[g, pl.ds(pl.multiple_of(kc * BIAS_ROWS, BIAS_ROWS), BIAS_ROWS), :]
            qaug_ref[g, HEAD_DIM:HEAD_DIM + BIAS_ROWS, :] = jnp.concatenate([bias, bias], axis=1).astype(BF16)
        rows = slice(h * half, (h + 1) * half)
        st = _dot(ks_ref[g, pl.ds(k0 + h * half, half), :], qaug_ref[g])
        s_ref[buf, rows, :] = st
        cmax = jnp.max(st, axis=0, keepdims=True)
        tmax_ref[buf] = cmax if h == 0 else jnp.maximum(tmax_ref[buf], cmax)

    def exps_half(g, buf, h):
        if h == 0:
            m_old = m_ref[g]
            m_new = jnp.maximum(m_old, tmax_ref[buf])
            m_ref[g] = m_new
            alpha_ref[buf] = jnp.exp2(m_old - m_new)
        m_new = m_ref[g]
        for c in range(2):
            rows = slice(h * half + c * (half // 2), h * half + (c + 1) * (half // 2))
            p_ref[buf, rows, :] = jnp.exp2(s_ref[buf, rows, :] - m_new).astype(BF16)

    def values_half(kt, g, buf, h):
        k0 = pl.multiple_of(jnp.clip(kt, 0, last_tile) * TK, TK)
        rows = slice(h * half, (h + 1) * half)
        part = _dot(vs_ref[g, :, pl.ds(k0 + h * half, half)], p_ref[buf, rows, :])
        acc_ref[g] = (alpha_ref[buf] * acc_ref[g] if h == 0 else acc_ref[g]) + part

    def substep(kt, g):
        g_next, kt_next = (g + 1) % N_KV, kt + (g + 1) // N_KV
        g_prev, kt_prev = (g - 1) % N_KV, kt - (1 if g == 0 else 0)
        other = (g + 1) % 2
        for h in range(2):
            scores_half(kt_next, g_next, other, h)
            exps_half(g, g % 2, h)
            values_half(kt_prev, g_prev, other, h)

    def tile(kt, carry):
        for g in range(N_KV):
            substep(kt, g)
        return carry

    n_tiles = (qs + TK - 1) // TK
    p_ref[1] = jnp.zeros(p_ref.shape[1:], BF16)
    alpha_ref[1] = jnp.ones(alpha_ref.shape[1:], F32)
    for h in range(2):
        scores_half(0, 0, 0, h)
    lax.fori_loop(0, n_tiles, tile, 0)
    for h in range(2):
        values_half(n_tiles - 1, N_KV - 1, 1, h)

    own = pl.ds(pl.multiple_of(qs, CHUNK), CHUNK)
    causal = dd_ref[0:CHUNK, :] <= 0
    for g in range(N_KV):
        qaug_ref[g, HEAD_DIM:HEAD_DIM + BIAS_ROWS, :] = jnp.zeros((BIAS_ROWS, 2 * nq), BF16)
        sd = jnp.where(causal, _dot(ks_ref[g, own, :], qaug_ref[g]), NEG)
        m_old = m_ref[g]
        m_new = jnp.maximum(m_old, jnp.max(sd, axis=0, keepdims=True))
        pd = jnp.exp2(sd - m_new).astype(BF16)
        acc = jnp.exp2(m_old - m_new) * acc_ref[g] + _dot(vs_ref[g, :, own], pd)
        ob_ref[g] = ob_ref[g] + gate(g, 1) * (acc[0:HEAD_DIM] * (1.0 / acc[HEAD_DIM:HEAD_DIM + 1]))

    nw = N_WBLK * CHUNK
    outs = [None] * N_KV

    def win_scores(g):
        kwin = jnp.concatenate([r[g] for r in kw_refs], axis=0)
        bias = jnp.concatenate(
            [jnp.where(i - (N_WBLK - 1) + j < 0, NEG, wb_ref[j * CHUNK:(j + 1) * CHUNK, :]) for j in range(N_WBLK)],
            axis=0)
        sw = _dot(kwin, qaug_ref[g]) + bias
        s_ref[g % 2, 0:nw, :] = sw
        tmax_ref[g % 2] = jnp.max(sw, axis=0, keepdims=True)

    def win_exps(g):
        p_ref[g % 2, 0:nw, :] = jnp.exp2(s_ref[g % 2, 0:nw, :] - tmax_ref[g % 2]).astype(BF16)

    def win_values(g):
        vwin = jnp.concatenate([r[g] for r in vw_refs], axis=1)
        ow = _dot(vwin, p_ref[g % 2, 0:nw, :])
        ob = ob_ref[g] + gate(g, 2) * (ow[0:HEAD_DIM] * (1.0 / ow[HEAD_DIM:HEAD_DIM + 1]))
        outs[g] = [ob[:, :nq], ob[:, nq:]]

    staged([win_scores, win_exps, win_values], N_KV)
    o_ref[...] = jnp.concatenate(sum(outs, []), axis=0).T.astype(o_ref.dtype)


def _attn_prompt(qt, gt, kcc, vcc, ks_aug, vs_aug, kw_aug, vw_aug, ovt, tri, dc, dd, wb):
    b, _, s = qt.shape
    nqb = s // CHUNK
    nc = kcc.shape[1]
    first = WINDOW // CHUNK
    whole = lambda shape: pl.BlockSpec((None,) + shape, lambda bi, i: (bi,) + (0,) * len(shape),
                                       pipeline_mode=pl.Buffered(1))
    const = lambda a: pl.BlockSpec(a.shape, lambda bi, i: (0,) * a.ndim)

    def kw_spec(j):
        return pl.BlockSpec((None, N_KV, CHUNK, LANES), lambda bi, i: (bi, 0, jnp.maximum(i - first + j, 0), 0))

    def vw_spec(j):
        return pl.BlockSpec((None, N_KV, V_ROWS, CHUNK), lambda bi, i: (bi, 0, 0, jnp.maximum(i - first + j, 0)))

    in_specs = ([pl.BlockSpec((None, N_HEADS * HEAD_DIM, CHUNK), lambda bi, i: (bi, 0, i)),
                 pl.BlockSpec((None, 32, CHUNK), lambda bi, i: (bi, 0, i)),
                 pl.BlockSpec((None, nc, N_KV * LANES), lambda bi, i: (bi, 0, 0)),
                 pl.BlockSpec((None, N_KV, V_ROWS, nc), lambda bi, i: (bi, 0, 0, 0)),
                 whole((N_KV, s, LANES)), whole((N_KV, V_ROWS, s))]
                + [kw_spec(j) for j in range(N_WBLK)] + [vw_spec(j) for j in range(N_WBLK)]
                + [const(ovt), const(tri), const(dc), const(dd), const(wb)])
    return pl.pallas_call(
        _attn_prompt_kernel,
        grid=(b, nqb),
        in_specs=in_specs,
        out_specs=pl.BlockSpec((None, CHUNK, N_HEADS * HEAD_DIM), lambda bi, i: (bi, i, 0)),
        out_shape=jax.ShapeDtypeStruct((b, s, N_HEADS * HEAD_DIM), BF16),
        scratch_shapes=[pltpu.VMEM((N_KV, LANES, 2 * CHUNK), BF16),
                        pltpu.VMEM((N_KV, s // L_SLC, CHUNK), F32),
                        pltpu.VMEM((N_KV, 1, 2 * CHUNK), F32),
                        pltpu.VMEM((N_KV, V_ROWS, 2 * CHUNK), F32),
                        pltpu.VMEM((N_KV, HEAD_DIM, 2 * CHUNK), F32),
                        pltpu.VMEM((s // L_SLC, N_KV * CHUNK), F32),
                        pltpu.VMEM((2, TK, 2 * CHUNK), F32),
                        pltpu.VMEM((2, TK, 2 * CHUNK), BF16),
                        pltpu.VMEM((2, 1, 2 * CHUNK), F32),
                        pltpu.VMEM((2, 1, 2 * CHUNK), F32),
                        pltpu.VMEM((2, nc, CHUNK), BF16),
                        pltpu.VMEM((2, nc, CHUNK), BF16)],
        compiler_params=_params(("parallel", "arbitrary")),
        name="attn_prompt",
    )(qt, gt, kcc, vcc, ks_aug, vs_aug, *([kw_aug] * N_WBLK), *([vw_aug] * N_WBLK), ovt, tri, dc, dd, wb)


NQ_PAD = 8


def _softmax_rows(s, mask):
    s = jnp.where(mask, s, NEG)
    mx = jnp.max(s, axis=1, keepdims=True)
    p = jnp.where(mask, jnp.exp(s - mx), 0.0)
    return p / jnp.maximum(jnp.sum(p, axis=1, keepdims=True), 1e-20)


def _attn_sample_kernel(pt_ref, qbd_ref, g_ref, kcc_ref, vcc_ref, *refs, n_pages, past_len):
    kpages = refs[:n_pages]
    vpages = refs[n_pages:2 * n_pages]
    (kst_ref, vst_ref, kws_ref, vws_ref, kwt_ref, vwt_ref, ov_ref, triu_ref, eexp_ref,
     o_ref, s_scr) = refs[2 * n_pages:]
    del pt_ref
    qbd = qbd_ref[...]
    nrow = qbd.shape[0]
    page = kpages[0].shape[1]

    def t_of(shape):
        return past_len + (lax.broadcasted_iota(jnp.int32, shape, 0) & (NQ_PAD - 1))

    nc = kcc_ref.shape[0]
    s = _dot_nt(qbd, kcc_ref[...])
    c_idx = lax.broadcasted_iota(jnp.int32, (nrow, nc), 1)
    p_c = _softmax_rows(s, c_idx * STRIDE + (L_CMP - 1) <= t_of((nrow, nc)))
    o_c = _dot(p_c.astype(BF16), vcc_ref[...])

    half = nrow // 2
    imp = _split_dot_right(p_c[:half] + p_c[half:], ov_ref[...])
    ns = imp.shape[1]
    n_idx = lax.broadcasted_iota(jnp.int32, (half, ns), 1)
    cur = t_of((half, ns)) // L_SLC
    valid = n_idx <= cur
    forced = valid & ((n_idx == 0) | (n_idx >= cur - 1))
    score = jnp.where(forced, FORCED_SCORE, jnp.where(valid, imp, -jnp.inf))
    thr = _kth_threshold(score, 1)
    above = score > thr
    tie = score == thr
    need = N_SEL - jnp.sum(jnp.where(above, 1.0, 0.0), axis=1, keepdims=True)
    rank = _dot(jnp.where(tie, 1.0, 0.0).astype(BF16), triu_ref[...])
    sel = jnp.where(above | (tie & (rank <= need)), 1.0, 0.0).astype(BF16)
    sel_keys = _dot(jnp.concatenate([sel, sel], axis=0), eexp_ref[...])

    for pg in range(n_pages):
        s_scr[:, pg * page:(pg + 1) * page] = _dot(qbd, kpages[pg][...].astype(BF16))
    s_scr[:, n_pages * page:(n_pages + 1) * page] = _dot(qbd, kst_ref[...])
    nk = (n_pages + 1) * page
    tok = lax.broadcasted_iota(jnp.int32, (nrow, nk), 1)
    p_s = _softmax_rows(s_scr[...], (sel_keys > 0.5) & (tok <= t_of((nrow, nk))))
    o_s = _dot_nt(p_s[:, n_pages * page:].astype(BF16), vst_ref[...])
    for pg in range(n_pages):
        o_s = o_s + _dot_nt(p_s[:, pg * page:(pg + 1) * page].astype(BF16), vpages[pg][...].astype(BF16))

    wb = kws_ref.shape[1]
    nw = wb + page
    sw = jnp.concatenate([_dot(qbd, kws_ref[...].astype(BF16)), _dot(qbd, kwt_ref[...])], axis=1)
    rel = t_of((nrow, nw)) - (past_len - wb + lax.broadcasted_iota(jnp.int32, (nrow, nw), 1))
    p_w = _softmax_rows(sw, (rel >= 0) & (rel < WINDOW))
    o_w = (_dot_nt(p_w[:, :wb].astype(BF16), vws_ref[...].astype(BF16))
           + _dot_nt(p_w[:, wb:].astype(BF16), vwt_ref[...]))

    g = g_ref[...]
    o = g[:, 0:1] * o_c + g[:, 1:2] * o_s + g[:, 2:3] * o_w
    row_g = (lax.broadcasted_iota(jnp.int32, o.shape, 0) // NQ_PAD) & (N_KV - 1)
    lane_g = lax.broadcasted_iota(jnp.int32, o.shape, 1) // HEAD_DIM
    o = jnp.where(row_g == lane_g, o, 0.0)
    per_r = N_KV * NQ_PAD
    for r in range(2):
        acc = o[r * per_r:r * per_r + NQ_PAD]
        for gg in range(1, N_KV):
            acc = acc + o[r * per_r + gg * NQ_PAD:r * per_r + (gg + 1) * NQ_PAD]
        o_ref[r] = acc


def _attn_sample(page_table, qbd, gsm, kcc, vcc, cache_k, cache_v, k_tail, v_tail, kw_state, vw_state,
                 kw_tail, vw_tail, ov, triu, eexp, past_len):
    nb, n_pages = page_table.shape
    kvw = cache_k.shape[1]
    page = cache_k.shape[2]
    nrow = qbd.shape[1]
    per_b = lambda shape: pl.BlockSpec((None,) + shape, lambda b, pt: (b,) + (0,) * len(shape))
    const = lambda shape: pl.BlockSpec(shape, lambda b, pt: (0,) * len(shape))

    def page_spec(pg):
        return pl.BlockSpec((None, kvw, page), lambda b, pt: (pt[b, pg], 0, 0))

    in_specs = ([per_b((nrow, kvw)), per_b((nrow, 8)), per_b(kcc.shape[1:]), per_b(vcc.shape[1:])]
                + [page_spec(pg) for pg in range(n_pages)]
                + [page_spec(pg) for pg in range(n_pages)]
                + [per_b((kvw, page)), per_b((kvw, page)), per_b(kw_state.shape[1:]), per_b(vw_state.shape[1:]),
                   per_b((kvw, page)), per_b((kvw, page)), const(ov.shape), const(triu.shape), const(eexp.shape)])
    grid_spec = pltpu.PrefetchScalarGridSpec(
        num_scalar_prefetch=1,
        grid=(nb,),
        in_specs=in_specs,
        out_specs=pl.BlockSpec((None, 2, NQ_PAD, kvw), lambda b, pt: (b, 0, 0, 0)),
        scratch_shapes=[pltpu.VMEM((nrow, (n_pages + 1) * page), F32)],
    )
    return pl.pallas_call(
        functools.partial(_attn_sample_kernel, n_pages=n_pages, past_len=past_len),
        grid_spec=grid_spec,
        out_shape=jax.ShapeDtypeStruct((nb, 2, NQ_PAD, kvw), F32),
        compiler_params=_params(("parallel",)),
        name="attn_sample",
    )(page_table, qbd, gsm, kcc, vcc, *([cache_k] * n_pages), *([cache_v] * n_pages),
      k_tail, v_tail, kw_state, vw_state, kw_tail, vw_tail, ov, triu, eexp)


FF_SPLIT = 2


def _finish_kernel(x_ref, a_ref, b_ref, gt1_ref, sh2_ref, sc2_ref, gt2_ref, wout_ref, gffn_ref, win_ref, wo2_ref,
                   o_ref):
    half = a_ref.shape[1]
    y = _dot(a_ref[...], wout_ref[0:half, :]) + _dot(b_ref[...], wout_ref[half:, :])
    x1 = x_ref[...] + gt1_ref[...] * y
    ms = jnp.mean(x1 * x1, axis=-1, keepdims=True)
    h = x1 * lax.rsqrt(ms + EPS) * gffn_ref[...]
    hb = (h * (1.0 + sc2_ref[...]) + sh2_ref[...]).astype(BF16)
    d_ff = wo2_ref.shape[0]
    step = d_ff // FF_SPLIT
    acc = None
    for c in range(FF_SPLIT):
        up = _dot(hb, win_ref[:, c * step:(c + 1) * step])
        gate = _dot(hb, win_ref[:, d_ff + c * step:d_ff + (c + 1) * step])
        z = (jax.nn.silu(up) * gate).astype(BF16)
        part = _dot(z, wo2_ref[c * step:(c + 1) * step, :])
        acc = part if acc is None else acc + part
    o_ref[...] = x1 + gt2_ref[...] * acc


def _finish(x, a, b, mods, per_row_mod, rows_per_mod, wout, gffn, win, wo2, tm):
    r, d = x.shape
    if per_row_mod:
        mod_spec = pl.BlockSpec((tm, d), lambda i: (i, 0))
    else:
        tiles_per_mod = rows_per_mod // tm
        mod_spec = pl.BlockSpec((None, 1, d), lambda i: (i // tiles_per_mod, 0, 0))
    single = lambda shape: pl.BlockSpec(shape, lambda i: (0,) * len(shape), pipeline_mode=pl.Buffered(1))
    row = lambda n: pl.BlockSpec((tm, n), lambda i: (i, 0))
    return pl.pallas_call(
        _finish_kernel,
        grid=(r // tm,),
        in_specs=[row(d), row(a.shape[1]), row(b.shape[1]), mod_spec, mod_spec, mod_spec, mod_spec,
                  single(wout.shape), single(gffn.shape), single(win.shape), single(wo2.shape)],
        out_specs=row(d),
        out_shape=jax.ShapeDtypeStruct((r, d), F32),
        compiler_params=_params(("parallel",)),
        name="finish",
    )(x, a, b, *mods, wout, gffn, win, wo2)


def _rope_tables(pos):
    half = HEAD_DIM // 2
    inv = ROPE_THETA ** (-jnp.arange(half, dtype=F32) / half)
    ang = pos.astype(F32)[:, None] * inv[None, :]
    cos, sin = jnp.cos(ang), jnp.sin(ang)
    return jnp.concatenate([cos] * 4, axis=1), jnp.concatenate([-sin, sin, -sin, sin], axis=1)


def _overlap(n_c, n_s):
    c_start = jnp.arange(n_c) * STRIDE
    blk = jnp.arange(n_s)
    return ((c_start[:, None] < (blk[None, :] + 1) * L_SLC)
            & (c_start[:, None] + L_CMP > blk[None, :] * L_SLC)).astype(BF16)


def _tail_page(new_rows, page):
    return jnp.pad(new_rows.transpose(0, 2, 1), ((0, 0), (0, 0), (0, page - new_rows.shape[1]))).astype(BF16)


def kernel(x_prompt, x_sample, cache_k_cmp, cache_v_cmp, cache_k_slc, cache_v_slc, state_k_win, state_v_win,
           page_table, c_prompt, c_sample, w_ada, b_ada, g_mix_norm, g_ffn_norm, w_in, g_sgu, w_sgu, b_sgu,
           g_q, g_k_cmp, g_k_slc, g_k_win, pe_k_cmp, pe_v_cmp, w_ck1, w_ck2, w_cv1, w_cv2, w_out, w_ffn_in,
           w_ffn_out):
    depth = w_in.shape[0]
    assert depth == 1, "single trunk layer"
    nb_p, seq, d = x_prompt.shape
    nb_s, n_new, _ = x_sample.shape
    n_pool, page = cache_k_cmp.shape[1], cache_k_cmp.shape[2]
    kvw = N_KV * HEAD_DIM
    n_pages = page_table.shape[1]
    past_len = n_pages * page
    wb_s = state_k_win.shape[2]
    l = 0

    in_cols = w_in.shape[2]
    w_in_pad = jnp.pad(w_in[l], ((0, 0), (0, 3200 - in_cols))).astype(BF16)
    tile_gain = lambda g, heads: jnp.tile(g, heads).reshape(1, heads * HEAD_DIM)
    gq_t, gkc_t = tile_gain(g_q[l], N_HEADS), tile_gain(g_k_cmp[l], N_KV)
    gks_t, gkw_t = tile_gain(g_k_slc[l], N_KV), tile_gain(g_k_win[l], N_KV)
    gsgu = g_sgu[l].reshape(1, -1)
    gmix = g_mix_norm[l].reshape(1, d)
    gffn = g_ffn_norm[l].reshape(1, d)
    gmat = (jnp.kron(jnp.eye(256 // HEAD_DIM, dtype=F32), jnp.ones((HEAD_DIM, HEAD_DIM), F32)) / HEAD_DIM).astype(BF16)
    w_tril = jnp.where(jnp.tril(jnp.ones((CHUNK, CHUNK), bool)), w_sgu[l], 0)
    wmix_p = w_tril.astype(BF16)
    bmix_p = jnp.repeat(b_sgu[l].T, HEAD_DIM, axis=1)
    eye_s = jnp.eye(CHUNK // n_new, dtype=F32)
    wmix_s = jax.vmap(lambda w: jnp.kron(eye_s, w[:n_new, :n_new]))(w_tril).astype(BF16)
    bmix_s = jnp.tile(jnp.repeat(b_sgu[l].T[:n_new], HEAD_DIM, axis=1), (CHUNK // n_new, 1))
    wout_b = w_out[l].astype(BF16)
    win_b = w_ffn_in[l].astype(BF16)
    wo2_b = w_ffn_out[l].astype(BF16)

    n_c = nb_p + nb_s
    n_c_pad = -(-n_c // 8) * 8
    c_all = jnp.pad(jnp.concatenate([c_prompt, c_sample], axis=0), ((0, n_c_pad - n_c), (0, 0)))
    ada = _ada(c_all, w_ada[l].astype(BF16), b_ada[l].reshape(1, -1))
    mods_p = [m.reshape(nb_p, 1, d) for m in jnp.split(ada[:nb_p], 6, axis=-1)]
    mods_s = [jnp.repeat(m, n_new, axis=0) for m in jnp.split(ada[nb_p:n_c], 6, axis=-1)]

    cos_p, sin_p = _rope_tables(jnp.arange(seq))
    cos_s, sin_s = _rope_tables(past_len + jnp.arange(n_new))
    reps = CHUNK // n_new
    cos_s, sin_s = jnp.tile(cos_s, (reps, 1)), jnp.tile(sin_s, (reps, 1))
    xp = x_prompt.reshape(nb_p * seq, d)
    xs = x_sample.reshape(nb_s * n_new, d)
    gains = (gsgu, gq_t, gkc_t, gks_t, gkw_t)
    blk_of_key = (jnp.arange(seq) // L_SLC) % BIAS_ROWS
    onehot = jnp.pad(jax.nn.one_hot(blk_of_key, LANES - HEAD_DIM, dtype=F32), ((0, 0), (HEAD_DIM, 0)))
    q_scale = HEAD_DIM ** -0.5 * math.log2(math.e)
    (a_p, vn_p, qt, gt, kct, vct, kst, vst, kwt, vwt, ks_aug, vs_aug, kw_aug, vw_aug) = _proj_cols(
        xp, nb_p, seq, mods_p[0], mods_p[1], gmix, w_in_pad, cos_p, sin_p, onehot, *gains, wmix_p, bmix_p, gmat,
        256, q_scale)
    (a_s, vn_s, q_s, kc_s, vc_s, ks_s, vs_s, kw_s, vw_s, gate_s) = _proj_rows(
        xs, mods_s[0], mods_s[1], gmix, w_in_pad, cos_s, sin_s, *gains, wmix_s, bmix_s, gmat, CHUNK)

    n_chunk_p = seq // STRIDE
    n_blk_p = seq // L_SLC
    wk_p = _compress_weights(pe_k_cmp[l], w_ck1[l], w_ck2[l], LANES)
    wv_p = _compress_weights(pe_v_cmp[l], w_cv1[l], w_cv2[l], LANES)
    pages_p = [pl.BlockSpec((None, kvw, LANES), functools.partial(lambda pg, i: (i, 0, pg), pg))
               for pg in range(seq // LANES)]
    kcc_p = _compress_call(kct, pages_p, LANES, (nb_p,), wk_p)
    vcc_p = _compress_call(vct, pages_p, LANES, (nb_p,), wv_p, transposed_out=True)
    ovt = _overlap(n_chunk_p, n_blk_p).T
    tri = jnp.tril(jnp.ones((n_blk_p, n_blk_p), BF16))
    q_lane = jnp.arange(2 * CHUNK)[None, :] % CHUNK
    dc = (jnp.arange(n_chunk_p)[:, None] * STRIDE + (L_CMP - 1) - q_lane).astype(jnp.int32)
    dd = (jnp.arange(TK)[:, None] - q_lane).astype(jnp.int32)
    jq = jnp.arange(N_WBLK * CHUNK)[:, None] - q_lane
    wb = jnp.where((jq > 0) & (jq <= WINDOW), 0.0, NEG).astype(F32)
    b_p = _attn_prompt(qt, gt, kcc_p, vcc_p, ks_aug, vs_aug, kw_aug, vw_aug, ovt, tri, dc, dd, wb)
    y_p = _finish(xp, a_p, b_p.reshape(nb_p * seq, N_HEADS * HEAD_DIM), mods_p[2:], False, seq, wout_b, gffn,
                  win_b, wo2_b, 256)

    fm = lambda a: a.transpose(0, 2, 3, 1).reshape(a.shape[0], kvw, a.shape[1])
    ck, cv, cks, cvs = fm(cache_k_cmp[l]), fm(cache_v_cmp[l]), fm(cache_k_slc[l]), fm(cache_v_slc[l])
    wk_s = _compress_weights(pe_k_cmp[l], w_ck1[l], w_ck2[l], HEAD_DIM)
    wv_s = _compress_weights(pe_v_cmp[l], w_cv1[l], w_cv2[l], HEAD_DIM)
    spb = (seq // LANES) // n_pages
    assert nb_s % spb == 0 and page == LANES

    def page_spec(j):
        return pl.BlockSpec((None, kvw, page), lambda i, pt: (pt[i * spb + j // n_pages, j % n_pages], 0, 0))

    pages_s = [page_spec(j) for j in range(spb * n_pages)]
    n_c_s = past_len // STRIDE
    kcc_s = _compress_call(ck, pages_s, page, (nb_s // spb,), wk_s, prefetch=page_table).reshape(nb_s, n_c_s, kvw)
    vcc_s = _compress_call(cv, pages_s, page, (nb_s // spb,), wv_s, prefetch=page_table).reshape(nb_s, n_c_s, kvw)

    q5 = (q_s * (HEAD_DIM ** -0.5)).reshape(nb_s, n_new, N_KV, 2, HEAD_DIM).transpose(0, 3, 2, 1, 4)
    q5 = jnp.pad(q5, ((0, 0), (0, 0), (0, 0), (0, NQ_PAD - n_new), (0, 0)))
    qbd = jnp.einsum("brgqd,gh->brgqhd", q5, jnp.eye(N_KV, dtype=F32))
    qbd = qbd.reshape(nb_s, 2 * N_KV * NQ_PAD, kvw).astype(BF16)
    g5 = gate_s[:, :3 * N_HEADS].reshape(nb_s, n_new, N_KV, 2, 3).transpose(0, 3, 2, 1, 4)
    g5 = jnp.pad(g5, ((0, 0), (0, 0), (0, 0), (0, NQ_PAD - n_new), (0, 5)))
    gsm = g5.reshape(nb_s, 2 * N_KV * NQ_PAD, 8)
    new = lambda a: a.reshape(nb_s, n_new, kvw)
    n_keys = (n_pages + 1) * page
    ov_s = _overlap(n_c_s, LANES)
    triu = jnp.triu(jnp.ones((LANES, LANES), BF16))
    eexp = (jnp.arange(n_keys)[None, :] // L_SLC == jnp.arange(LANES)[:, None]).astype(BF16)
    os_ = _attn_sample(page_table, qbd, gsm, kcc_s, vcc_s, cks, cvs, _tail_page(new(ks_s), page),
                       _tail_page(new(vs_s), page), fm(state_k_win[l]), fm(state_v_win[l]),
                       _tail_page(new(kw_s), page), _tail_page(new(vw_s), page), ov_s, triu, eexp, past_len)
    b_s = os_[:, :, :n_new].reshape(nb_s, 2, n_new, N_KV, HEAD_DIM).transpose(0, 2, 3, 1, 4)
    b_s = b_s.reshape(nb_s * n_new, N_HEADS * HEAD_DIM).astype(BF16)

    y_s = _finish(xs, a_s, b_s, mods_s[2:], True, 0, wout_b, gffn, win_b, wo2_b, CHUNK)

    wb_p = min(WINDOW, seq)
    assert seq - ((seq - 1) // CHUNK) * CHUNK == CHUNK, "the prompt ends on a full chunk"
    from_cols = lambda t: t.reshape(1, nb_p, N_KV, HEAD_DIM, seq).transpose(0, 1, 4, 2, 3)
    outs_p = [from_cols(kct), from_cols(vct), from_cols(kst), from_cols(vst),
              from_cols(kwt)[:, :, seq - wb_p:], from_cols(vwt)[:, :, seq - wb_p:], vn_p[None]]
    kv5 = lambda a, nb, t: a.reshape(1, nb, t, N_KV, HEAD_DIM)
    kw_all = jnp.concatenate([state_k_win[l], kv5(kw_s, nb_s, n_new)[0]], axis=1)
    vw_all = jnp.concatenate([state_v_win[l], kv5(vw_s, nb_s, n_new)[0]], axis=1)
    outs_s = [kv5(kc_s, nb_s, n_new), kv5(vc_s, nb_s, n_new), kv5(ks_s, nb_s, n_new), kv5(vs_s, nb_s, n_new),
              kw_all[None, :, n_new:], vw_all[None, :, n_new:], vn_s.reshape(1, nb_s, n_new, -1)]
    return (y_p.reshape(nb_p, seq, d), y_s.reshape(nb_s, n_new, d), *outs_p, *outs_s)
```

```python
import functools
import math

import jax
import jax.numpy as jnp
from jax import lax
from jax.experimental import pallas as pl
from jax.experimental.pallas import tpu as pltpu

F32 = jnp.float32
BF16 = jnp.bfloat16

CHUNK = 128
A_GROUPS = 8
HEAD_DIM = 64
N_HEADS = 8
N_KV = 4
L_CMP = 32
STRIDE = 16
CMP_HID = 256
L_SLC = 64
N_SEL = 16
WINDOW = 512
ROPE_THETA = 10000.0
EPS = 1e-6
FORCED_SCORE = 1e4
NEG = -1e30
SEL_BIAS = -(2.0 ** 100)

LANES = 128
V_ROWS = HEAD_DIM + 16
VMEM_LIMIT = 52 * 1024 * 1024
PROMPT_ROWS = 512

_NT = (((1,), (1,)), ((), ()))


def _dot(a, b):
    return jnp.dot(a, b, preferred_element_type=F32)


def _dot_nt(a, b):
    return lax.dot_general(a, b, _NT, preferred_element_type=F32)


def _split_dot_left(coef, x):
    hi = x.astype(BF16)
    lo = (x - hi.astype(F32)).astype(BF16)
    return _dot(coef, hi) + _dot(coef, lo)


def _split_dot_right(x, coef):
    hi = x.astype(BF16)
    lo = (x - hi.astype(F32)).astype(BF16)
    return _dot(hi, coef) + _dot(lo, coef)


def _params(sem, flags=None):
    return pltpu.CompilerParams(dimension_semantics=sem, vmem_limit_bytes=VMEM_LIMIT, flags=flags)


def _with_ones_row(vt):
    n = vt.shape[1]
    row = lax.broadcasted_iota(jnp.int32, (V_ROWS - HEAD_DIM, n), 0)
    return jnp.concatenate([vt, jnp.where(row == 0, 1.0, 0.0).astype(vt.dtype)], axis=0)


def _ada_kernel(c_ref, w_ref, b_ref, o_ref):
    c = c_ref[...]
    o_ref[...] = _dot(jax.nn.silu(c).astype(BF16), w_ref[...]) + b_ref[...]


def _ada(c, w, b):
    m, k = c.shape
    n = w.shape[1]
    tn = 1024
    return pl.pallas_call(
        _ada_kernel,
        grid=(n // tn,),
        in_specs=[pl.BlockSpec((m, k), lambda j: (0, 0)),
                  pl.BlockSpec((k, tn), lambda j: (0, j)),
                  pl.BlockSpec((1, tn), lambda j: (0, j))],
        out_specs=pl.BlockSpec((m, tn), lambda j: (0, j)),
        out_shape=jax.ShapeDtypeStruct((m, n), F32),
        compiler_params=_params(("parallel",)),
        name="ada",
    )(c, w, b)


def _group_mean_sq(y, g_ref):
    y2 = y * y
    cols = []
    for c in range(y.shape[1] // 256):
        cols.append(_split_dot_right(y2[:, 256 * c:256 * (c + 1)], g_ref[...]))
    return cols[0] if len(cols) == 1 else jnp.concatenate(cols, axis=1)


def _rope(x, cos, sin):
    n = x.shape[1]
    reps = n // LANES
    cos_t = cos if reps == 1 else jnp.concatenate([cos] * reps, axis=1)
    sin_t = sin if reps == 1 else jnp.concatenate([sin] * reps, axis=1)
    lane = lax.broadcasted_iota(jnp.int32, x.shape, 1)
    first_half = (lane & (HEAD_DIM - 1)) < (HEAD_DIM // 2)
    partner = jnp.where(first_half, pltpu.roll(x, n - HEAD_DIM // 2, 1), pltpu.roll(x, HEAD_DIM // 2, 1))
    return x * cos_t + partner * sin_t


def _proj_common(x_ref, sh_ref, sc_ref, gmix_ref, w_ref, cos_ref, sin_ref, gsgu_ref, gq_ref, gkc_ref, gks_ref,
                 gkw_ref, wmix_ref, bmix_ref, gmat_ref, a_ref, vn_ref):
    x = x_ref[...]
    tm = x.shape[0]
    ms = jnp.mean(x * x, axis=-1, keepdims=True)
    h = x * lax.rsqrt(ms + EPS) * gmix_ref[...]
    h = h * (1.0 + sc_ref[...]) + sh_ref[...]
    hb = h.astype(BF16)
    cos = cos_ref[...]
    sin = sin_ref[...]

    def seg(lo, hi):
        return _dot(hb, w_ref[:, lo:hi])

    def head_norm(y, g_ref):
        return y * lax.rsqrt(_group_mean_sq(y, gmat_ref) + EPS) * g_ref[...]

    u = jax.nn.gelu(seg(0, 512))
    v = jax.nn.gelu(seg(512, 1024))
    vn = head_norm(v, gsgu_ref)
    vn_ref[...] = vn[tm - CHUNK:]
    vb = vn.astype(BF16)
    lane = lax.broadcasted_iota(jnp.int32, (CHUNK, LANES), 1)
    low = lane < HEAD_DIM
    for ck in range(tm // CHUNK):
        rows = slice(ck * CHUNK, (ck + 1) * CHUNK)
        for pr in range(A_GROUPS // 2):
            cols = slice(pr * LANES, (pr + 1) * LANES)
            vp = vb[rows, cols]
            mixed = jnp.where(low, _dot(wmix_ref[2 * pr], vp), _dot(wmix_ref[2 * pr + 1], vp))
            mixed = mixed + bmix_ref[:, cols]
            a_ref[rows, cols] = (u[rows, cols] * mixed).astype(a_ref.dtype)

    q = _rope(head_norm(seg(1024, 1536), gq_ref), cos, sin)
    kc = _rope(head_norm(seg(1536, 1792), gkc_ref), cos, sin)
    vc = seg(1792, 2048)
    ks = _rope(head_norm(seg(2048, 2304), gks_ref), cos, sin)
    vs = seg(2304, 2560)
    kw = _rope(head_norm(seg(2560, 2816), gkw_ref), cos, sin)
    vw = seg(2816, 3072)
    gates = jax.nn.sigmoid(seg(3072, 3200))
    return q, kc, vc, ks, vs, kw, vw, gates


def _proj_rows_kernel(*refs):
    ins, (a_ref, vn_ref, q_ref, kc_ref, vc_ref, ks_ref, vs_ref, kw_ref, vw_ref, gate_ref) = refs[:15], refs[15:]
    outs = _proj_common(*ins, a_ref, vn_ref)
    for ref, val in zip((q_ref, kc_ref, vc_ref, ks_ref, vs_ref, kw_ref, vw_ref, gate_ref), outs):
        ref[...] = val


def _proj_cols_kernel(*refs, q_scale):
    ins, oh_ref = refs[:15], refs[15]
    (a_ref, vn_ref, qt_ref, gt_ref, kct_ref, vct_ref, kst_ref, vst_ref, kwt_ref, vwt_ref,
     ksa_ref, vsa_ref, kwa_ref, vwa_ref, stage_ref) = refs[16:]
    q, kc, vc, ks, vs, kw, vw, gates = _proj_common(*ins, a_ref, vn_ref)
    tm = q.shape[0]
    qt_ref[...] = (q * q_scale).T.astype(BF16)
    gt_ref[...] = gates.T[0:gt_ref.shape[0]]
    kct_ref[...] = kc.T
    kst_ref[...] = ks.T
    kwt_ref[...] = kw.T
    stage_ref[0] = vc
    stage_ref[1] = vs
    stage_ref[2] = vw
    vct_ref[...] = stage_ref[0].T
    vst = stage_ref[1].T
    vwt = stage_ref[2].T
    vst_ref[...] = vst
    vwt_ref[...] = vwt
    low = lax.broadcasted_iota(jnp.int32, (tm, LANES), 1) < HEAD_DIM
    onehot = oh_ref[...]
    for g in range(N_KV):
        pair = slice((g // 2) * LANES, (g // 2 + 1) * LANES)
        ks_g, kw_g = ks[:, pair], kw[:, pair]
        if g % 2 == 1:
            ks_g, kw_g = pltpu.roll(ks_g, HEAD_DIM, 1), pltpu.roll(kw_g, HEAD_DIM, 1)
        ksa_ref[g] = jnp.where(low, ks_g, onehot).astype(BF16)
        kwa_ref[g] = jnp.where(low, kw_g, 0.0).astype(BF16)
        rows = slice(g * HEAD_DIM, (g + 1) * HEAD_DIM)
        vsa_ref[g] = _with_ones_row(vst[rows]).astype(BF16)
        vwa_ref[g] = _with_ones_row(vwt[rows]).astype(BF16)


def _proj_in_specs(tm, d, per_row_mod, rows_per_mod, rope_tiles, w_pad, wmix, bmix, gmat):
    if per_row_mod:
        mod_spec = pl.BlockSpec((tm, d), lambda i: (i, 0))
    else:
        tiles_per_mod = rows_per_mod // tm
        mod_spec = pl.BlockSpec((None, 1, d), lambda i: (i // tiles_per_mod, 0, 0))
    const = lambda shape: pl.BlockSpec(shape, lambda i: (0,) * len(shape))
    rope_spec = pl.BlockSpec((tm, LANES), lambda i: (i % rope_tiles, 0))
    w_spec = pl.BlockSpec(w_pad.shape, lambda i: (0, 0), pipeline_mode=pl.Buffered(1))
    return [pl.BlockSpec((tm, d), lambda i: (i, 0)), mod_spec, mod_spec, const((1, d)), w_spec,
            rope_spec, rope_spec,
            const((1, 512)), const((1, 512)), const((1, 256)), const((1, 256)), const((1, 256)),
            const(wmix.shape), const(bmix.shape), const(gmat.shape)]


def _proj_rows(x, sh, sc, gmix, w_pad, cos, sin, gsgu, gq, gkc, gks, gkw, wmix, bmix, gmat, tm):
    r, d = x.shape
    row = lambda n: pl.BlockSpec((tm, n), lambda i: (i, 0))
    out_widths = [512, 512, 512, 256, 256, 256, 256, 256, 256, 128]
    out_dtypes = [BF16] + [F32] * 9
    return pl.pallas_call(
        _proj_rows_kernel,
        grid=(r // tm,),
        in_specs=_proj_in_specs(tm, d, True, 0, cos.shape[0] // tm, w_pad, wmix, bmix, gmat),
        out_specs=[row(n) for n in out_widths],
        out_shape=[jax.ShapeDtypeStruct((r, n), dt) for n, dt in zip(out_widths, out_dtypes)],
        compiler_params=_params(("parallel",)),
        name="proj_rows",
    )(x, sh, sc, gmix, w_pad, cos, sin, gsgu, gq, gkc, gks, gkw, wmix, bmix, gmat)


def _proj_cols(x, nb, seq, sh, sc, gmix, w_pad, cos, sin, onehot, gsgu, gq, gkc, gks, gkw, wmix, bmix, gmat, tm,
               q_scale):
    r, d = x.shape
    tpb = seq // tm
    kvw = N_KV * HEAD_DIM
    bi = lambda i: i // tpb
    ti = lambda i: i % tpb
    row = lambda n: pl.BlockSpec((tm, n), lambda i: (i, 0))
    colt = lambda n: pl.BlockSpec((None, n, tm), lambda i: (bi(i), 0, ti(i)))
    out_specs = [row(512),
                 pl.BlockSpec((None, CHUNK, 512), lambda i: (bi(i), 0, 0)),
                 colt(512), colt(32),
                 colt(kvw), colt(kvw), colt(kvw), colt(kvw), colt(kvw), colt(kvw),
                 pl.BlockSpec((None, N_KV, tm, LANES), lambda i: (bi(i), 0, ti(i), 0)),
                 pl.BlockSpec((None, N_KV, V_ROWS, tm), lambda i: (bi(i), 0, 0, ti(i))),
                 pl.BlockSpec((None, N_KV, tm, LANES), lambda i: (bi(i), 0, ti(i), 0)),
                 pl.BlockSpec((None, N_KV, V_ROWS, tm), lambda i: (bi(i), 0, 0, ti(i)))]
    sds = jax.ShapeDtypeStruct
    out_shape = [sds((r, 512), BF16), sds((nb, CHUNK, 512), F32), sds((nb, 512, seq), BF16),
                 sds((nb, 32, seq), F32)] + [sds((nb, kvw, seq), F32)] * 6 + [
                 sds((nb, N_KV, seq, LANES), BF16), sds((nb, N_KV, V_ROWS, seq), BF16),
                 sds((nb, N_KV, seq, LANES), BF16), sds((nb, N_KV, V_ROWS, seq), BF16)]
    in_specs = _proj_in_specs(tm, d, False, seq, tpb, w_pad, wmix, bmix, gmat)
    in_specs.append(pl.BlockSpec((tm, LANES), lambda i: (ti(i), 0)))
    return pl.pallas_call(
        functools.partial(_proj_cols_kernel, q_scale=q_scale),
        grid=(r // tm,),
        in_specs=in_specs,
        out_specs=out_specs,
        out_shape=out_shape,
        scratch_shapes=[pltpu.VMEM((3, tm, kvw), F32)],
        compiler_params=_params(("arbitrary",)),
        name="proj_cols",
    )(x, sh, sc, gmix, w_pad, cos, sin, gsgu, gq, gkc, gks, gkw, wmix, bmix, gmat, onehot)


def _compress_kernel(*refs, n_prefetch, n_page, transposed_out):
    refs = refs[n_prefetch:]
    page_refs = refs[:n_page]
    (w1_ref, w1a_ref, w1b_ref, pe_ref, w2_ref, o_ref,
     xs_ref, lhs_ref, a_ref, b_ref, hid_ref, stage_ref) = refs[n_page:]
    page = page_refs[0].shape[1]
    cpp = page // STRIDE
    m = n_page * cpp
    pitch = xs_ref.shape[1] // STRIDE
    for pg in range(n_page):
        xt = page_refs[pg][...].T
        for pp in range(N_KV // 2):
            for c in range(cpp):
                xs_ref[pp, pl.ds(pg * cpp + c, STRIDE, stride=pitch), :] = (
                    xt[c * STRIDE:(c + 1) * STRIDE, pp * LANES:(pp + 1) * LANES])
    low = lax.broadcasted_iota(jnp.int32, (m, LANES), 1) < HEAD_DIM
    for rr in range(STRIDE // 2):
        cols = slice(rr * LANES, (rr + 1) * LANES)
        for pp in range(N_KV // 2):
            p0 = xs_ref[pp, 2 * rr * pitch:2 * rr * pitch + m, :]
            p1 = xs_ref[pp, (2 * rr + 1) * pitch:(2 * rr + 1) * pitch + m, :]
            r0 = pltpu.roll(p0, HEAD_DIM, 1)
            r1 = pltpu.roll(p1, HEAD_DIM, 1)
            g0 = 2 * pp
            lhs_ref[g0 * m:(g0 + 1) * m, cols] = jnp.where(low, p0, r1).astype(BF16)
            lhs_ref[(g0 + 1) * m:(g0 + 2) * m, cols] = jnp.where(low, r0, p1).astype(BF16)
    pe_term = _dot(jnp.broadcast_to(pe_ref[...], (8, pe_ref.shape[1])).astype(BF16), w1_ref[...])[0:1]
    rows = N_KV * m
    a_ref[...] = _dot(lhs_ref[...], w1a_ref[...])
    b_ref[0:rows, :] = _dot(lhs_ref[...], w1b_ref[...])
    b_ref[rows:rows + 8, :] = jnp.zeros((8, CMP_HID), F32)
    hid_ref[...] = jax.nn.gelu(a_ref[...] + b_ref[pl.ds(1, rows), :] + pe_term).astype(BF16)
    out = _dot(hid_ref[0:m, :], w2_ref[0])
    for g in range(1, N_KV):
        out = out + _dot(hid_ref[g * m:(g + 1) * m, :], w2_ref[g])
    if transposed_out:
        stage_ref[...] = out
        out_t = stage_ref[...].T
        for g in range(N_KV):
            o_ref[g] = _with_ones_row(out_t[g * LANES:g * LANES + HEAD_DIM]).astype(o_ref.dtype)
    else:
        o_ref[...] = out.astype(o_ref.dtype)


def _compress_call(page_array, page_specs, page, grid, weights, prefetch=None, transposed_out=False):
    n_page = len(page_specs)
    m = n_page * page // STRIDE
    n_out = weights[-1].shape[2]
    nsp = 0 if prefetch is None else 1
    const = lambda shape: pl.BlockSpec(shape, lambda *a: (0,) * len(shape))
    if transposed_out:
        out_spec = pl.BlockSpec((None, N_KV, V_ROWS, m), lambda i, *a: (i, 0, 0, 0))
        out_shape = jax.ShapeDtypeStruct((grid[0], N_KV, V_ROWS, m), BF16)
    else:
        out_spec = pl.BlockSpec((None, m, n_out), lambda i, *a: (i, 0, 0))
        out_shape = jax.ShapeDtypeStruct((grid[0], m, n_out), BF16)
    grid_spec = pltpu.PrefetchScalarGridSpec(
        num_scalar_prefetch=nsp,
        grid=grid,
        in_specs=list(page_specs) + [const(w.shape) for w in weights],
        out_specs=out_spec,
        scratch_shapes=[pltpu.VMEM((N_KV // 2, STRIDE * (m + 8), LANES), F32),
                        pltpu.VMEM((N_KV * m, STRIDE * HEAD_DIM), BF16),
                        pltpu.VMEM((N_KV * m, CMP_HID), F32),
                        pltpu.VMEM((N_KV * m + 8, CMP_HID), F32),
                        pltpu.VMEM((N_KV * m, CMP_HID), BF16),
                        pltpu.VMEM((m, n_out), F32)],
    )
    args = ([] if prefetch is None else [prefetch]) + [page_array] * n_page + list(weights)
    return pl.pallas_call(
        functools.partial(_compress_kernel, n_prefetch=nsp, n_page=n_page, transposed_out=transposed_out),
        grid_spec=grid_spec,
        out_shape=out_shape,
        compiler_params=_params(("parallel",)),
        name="compress",
    )(*args)


def _compress_weights(pe, w1, w2, head_stride):
    half = STRIDE * HEAD_DIM
    w1b = w1.astype(BF16)
    w2p = jnp.pad(w2, ((0, 0), (0, head_stride - HEAD_DIM)))
    w2blk = jnp.einsum("gh,kd->gkhd", jnp.eye(N_KV, dtype=F32), w2p)
    w2blk = w2blk.reshape(N_KV, CMP_HID, N_KV * head_stride).astype(BF16)
    return w1b, w1b[:half], w1b[half:], pe.reshape(1, L_CMP * HEAD_DIM), w2blk


def _kth_threshold(score, axis):
    shape = list(score.shape)
    shape[axis] = 1

    def body(_, carry):
        thr, cnt = carry
        cand = jnp.where(score < thr, score, -jnp.inf)
        mx = jnp.max(cand, axis=axis, keepdims=True)
        c = jnp.sum(jnp.where(score >= mx, 1.0, 0.0), axis=axis, keepdims=True)
        upd = cnt < N_SEL
        return jnp.where(upd, mx, thr), jnp.where(upd, c, cnt)

    thr, _ = lax.fori_loop(0, N_SEL, body, (jnp.full(shape, jnp.inf, F32), jnp.zeros(shape, F32)))
    return thr


TK = 1024
BIAS_ROWS = TK // L_SLC
N_WBLK = WINDOW // CHUNK + 1


def _attn_prompt_kernel(qt_ref, gt_ref, kcc_ref, vcc_ref, ks_ref, vs_ref, *refs):
    kw_refs, vw_refs = refs[:N_WBLK], refs[N_WBLK:2 * N_WBLK]
    (ovt_ref, tri_ref, dc_ref, dd_ref, wb_ref, o_ref,
     qaug_ref, sel_ref, m_ref, acc_ref, ob_ref, sc_ref, s_ref, p_ref, tmax_ref, alpha_ref, hi_ref,
     lo_ref) = refs[2 * N_WBLK:]
    i = pl.program_id(1)
    qs = i * CHUNK
    nq = CHUNK
    gt = gt_ref[...]

    def gate(g, branch):
        return jnp.concatenate([gt[6 * g + branch:6 * g + branch + 1],
                                gt[6 * g + 3 + branch:6 * g + 3 + branch + 1]], axis=1)

    def staged(stages, n):
        depth = len(stages)
        for step in range(n + depth - 1):
            for k, stage in enumerate(stages):
                if 0 <= step - k < n:
                    stage(step - k)

    nc = kcc_ref.shape[0]
    ns = sel_ref.shape[1]
    n_idx = lax.broadcasted_iota(jnp.int32, (ns, nq), 0)
    cur = (qs + lax.broadcasted_iota(jnp.int32, (ns, nq), 1)) // L_SLC
    valid = n_idx <= cur
    forced = valid & ((n_idx == 0) | (n_idx >= cur - 1))

    def cmp_scores(g):
        base = 2 * g * HEAD_DIM
        qaug_ref[g, 0:HEAD_DIM, :] = jnp.concatenate(
            [qt_ref[base:base + HEAD_DIM, :], qt_ref[base + HEAD_DIM:base + 2 * HEAD_DIM, :]], axis=1)
        qaug_ref[g, HEAD_DIM:, :] = jnp.zeros((LANES - HEAD_DIM, 2 * nq), BF16)
        s = jnp.where(dc_ref[...] <= qs, _dot(kcc_ref[:, g * LANES:(g + 1) * LANES], qaug_ref[g]), NEG)
        s_ref[g % 2, 0:nc, :] = s
        mx = jnp.max(s, axis=0, keepdims=True)
        tmax_ref[g % 2] = jnp.where(mx > 0.5 * NEG, mx, 0.0)
        m_ref[g] = jnp.full((1, 2 * nq), NEG, F32)
        acc_ref[g] = jnp.zeros((V_ROWS, 2 * nq), F32)

    def cmp_exps(g):
        p = jnp.exp2(s_ref[g % 2, 0:nc, :] - tmax_ref[g % 2])
        inv = 1.0 / jnp.maximum(jnp.sum(p, axis=0, keepdims=True), 1e-20)
        alpha_ref[g % 2] = inv
        p_ref[g % 2, 0:nc, :] = p.astype(BF16)
        psum = p[:, :nq] * inv[:, :nq] + p[:, nq:] * inv[:, nq:]
        hi = psum.astype(BF16)
        hi_ref[g % 2] = hi
        lo_ref[g % 2] = (psum - hi.astype(F32)).astype(BF16)

    def cmp_values(g):
        oc = _dot(vcc_ref[g], p_ref[g % 2, 0:nc, :])
        ob_ref[g] = gate(g, 0) * (oc[0:HEAD_DIM] * alpha_ref[g % 2])
        imp = _dot(ovt_ref[...], hi_ref[g % 2]) + _dot(ovt_ref[...], lo_ref[g % 2])
        sc_ref[:, g * nq:(g + 1) * nq] = jnp.where(forced, FORCED_SCORE, jnp.where(valid, imp, -jnp.inf))

    staged([cmp_scores, cmp_exps, cmp_values], N_KV)

    score = sc_ref[...]
    thr = _kth_threshold(score, 0)
    above = score > thr
    tie = score == thr
    need = N_SEL - jnp.sum(jnp.where(above, 1.0, 0.0), axis=0, keepdims=True)
    rank = _dot(tri_ref[...], jnp.where(tie, 1.0, 0.0).astype(BF16))
    before = lax.broadcasted_iota(jnp.int32, score.shape, 0) < 2 * i
    sel_bias = jnp.where((above | (tie & (rank <= need))) & before, 0.0, SEL_BIAS)
    for g in range(N_KV):
        sel_ref[g] = sel_bias[:, g * nq:(g + 1) * nq]

    last_tile = ks_ref.shape[1] // TK - 1
    half = TK // 2

    def scores_half(kt, g, buf, h):
        kc = jnp.clip(kt, 0, last_tile)
        k0 = pl.multiple_of(kc * TK, TK)
        if h == 0:
            bias = sel_ref[g, pl.ds(pl.multiple_of(kc * BIAS_ROWS, BIAS_ROWS), BIAS_ROWS), :]
            qaug_ref[g, HEAD_DIM:HEAD_DIM + BIAS_ROWS, :] = jnp.concatenate([bias, bias], axis=1).astype(BF16)
        rows = slice(h * half, (h + 1) * half)
        st = _dot(ks_ref[g, pl.ds(k0 + h * half, half), :], qaug_ref[g])
        s_ref[buf, rows, :] = st
        cmax = jnp.max(st, axis=0, keepdims=True)
        tmax_ref[buf] = cmax if h == 0 else jnp.maximum(tmax_ref[buf], cmax)

    def exps_half(g, buf, h):
        if h == 0:
            m_old = m_ref[g]
            m_new = jnp.maximum(m_old, tmax_ref[buf])
            m_ref[g] = m_new
            alpha_ref[buf] = jnp.exp2(m_old - m_new)
        m_new = m_ref[g]
        for c in range(2):
            rows = slice(h * half + c * (half // 2), h * half + (c + 1) * (half // 2))
            p_ref[buf, rows, :] = jnp.exp2(s_ref[buf, rows, :] - m_new).astype(BF16)

    def values_half(kt, g, buf, h):
        k0 = pl.multiple_of(jnp.clip(kt, 0, last_tile) * TK, TK)
        rows = slice(h * half, (h + 1) * half)
        part = _dot(vs_ref[g, :, pl.ds(k0 + h * half, half)], p_ref[buf, rows, :])
        acc_ref[g] = (alpha_ref[buf] * acc_ref[g] if h == 0 else acc_ref[g]) + part

    def substep(kt, g):
        g_next, kt_next = (g + 1) % N_KV, kt + (g + 1) // N_KV
        g_prev, kt_prev = (g - 1) % N_KV, kt - (1 if g == 0 else 0)
        other = (g + 1) % 2
        for h in range(2):
            scores_half(kt_next, g_next, other, h)
            exps_half(g, g % 2, h)
            values_half(kt_prev, g_prev, other, h)

    def tile(kt, carry):
        for g in range(N_KV):
            substep(kt, g)
        return carry

    n_tiles = (qs + TK - 1) // TK
    p_ref[1] = jnp.zeros(p_ref.shape[1:], BF16)
    alpha_ref[1] = jnp.ones(alpha_ref.shape[1:], F32)
    for h in range(2):
        scores_half(0, 0, 0, h)
    lax.fori_loop(0, n_tiles, tile, 0)
    for h in range(2):
        values_half(n_tiles - 1, N_KV - 1, 1, h)

    own = pl.ds(pl.multiple_of(qs, CHUNK), CHUNK)
    causal = dd_ref[0:CHUNK, :] <= 0
    for g in range(N_KV):
        qaug_ref[g, HEAD_DIM:HEAD_DIM + BIAS_ROWS, :] = jnp.zeros((BIAS_ROWS, 2 * nq), BF16)
        sd = jnp.where(causal, _dot(ks_ref[g, own, :], qaug_ref[g]), NEG)
        m_old = m_ref[g]
        m_new = jnp.maximum(m_old, jnp.max(sd, axis=0, keepdims=True))
        pd = jnp.exp2(sd - m_new).astype(BF16)
        acc = jnp.exp2(m_old - m_new) * acc_ref[g] + _dot(vs_ref[g, :, own], pd)
        ob_ref[g] = ob_ref[g] + gate(g, 1) * (acc[0:HEAD_DIM] * (1.0 / acc[HEAD_DIM:HEAD_DIM + 1]))

    nw = N_WBLK * CHUNK
    outs = [None] * N_KV

    def win_scores(g):
        kwin = jnp.concatenate([r[g] for r in kw_refs], axis=0)
        bias = jnp.concatenate(
            [jnp.where(i - (N_WBLK - 1) + j < 0, NEG, wb_ref[j * CHUNK:(j + 1) * CHUNK, :]) for j in range(N_WBLK)],
            axis=0)
        sw = _dot(kwin, qaug_ref[g]) + bias
        s_ref[g % 2, 0:nw, :] = sw
        tmax_ref[g % 2] = jnp.max(sw, axis=0, keepdims=True)

    def win_exps(g):
        p_ref[g % 2, 0:nw, :] = jnp.exp2(s_ref[g % 2, 0:nw, :] - tmax_ref[g % 2]).astype(BF16)

    def win_values(g):
        vwin = jnp.concatenate([r[g] for r in vw_refs], axis=1)
        ow = _dot(vwin, p_ref[g % 2, 0:nw, :])
        ob = ob_ref[g] + gate(g, 2) * (ow[0:HEAD_DIM] * (1.0 / ow[HEAD_DIM:HEAD_DIM + 1]))
        outs[g] = [ob[:, :nq], ob[:, nq:]]

    staged([win_scores, win_exps, win_values], N_KV)
    o_ref[...] = jnp.concatenate(sum(outs, []), axis=0).T.astype(o_ref.dtype)


def _attn_prompt(qt, gt, kcc, vcc, ks_aug, vs_aug, kw_aug, vw_aug, ovt, tri, dc, dd, wb):
    b, _, s = qt.shape
    nqb = s // CHUNK
    nc = kcc.shape[1]
    first = WINDOW // CHUNK
    whole = lambda shape: pl.BlockSpec((None,) + shape, lambda bi, i: (bi,) + (0,) * len(shape),
                                       pipeline_mode=pl.Buffered(1))
    const = lambda a: pl.BlockSpec(a.shape, lambda bi, i: (0,) * a.ndim)

    def kw_spec(j):
        return pl.BlockSpec((None, N_KV, CHUNK, LANES), lambda bi, i: (bi, 0, jnp.maximum(i - first + j, 0), 0))

    def vw_spec(j):
        return pl.BlockSpec((None, N_KV, V_ROWS, CHUNK), lambda bi, i: (bi, 0, 0, jnp.maximum(i - first + j, 0)))

    in_specs = ([pl.BlockSpec((None, N_HEADS * HEAD_DIM, CHUNK), lambda bi, i: (bi, 0, i)),
                 pl.BlockSpec((None, 32, CHUNK), lambda bi, i: (bi, 0, i)),
                 pl.BlockSpec((None, nc, N_KV * LANES), lambda bi, i: (bi, 0, 0)),
                 pl.BlockSpec((None, N_KV, V_ROWS, nc), lambda bi, i: (bi, 0, 0, 0)),
                 whole((N_KV, s, LANES)), whole((N_KV, V_ROWS, s))]
                + [kw_spec(j) for j in range(N_WBLK)] + [vw_spec(j) for j in range(N_WBLK)]
                + [const(ovt), const(tri), const(dc), const(dd), const(wb)])
    return pl.pallas_call(
        _attn_prompt_kernel,
        grid=(b, nqb),
        in_specs=in_specs,
        out_specs=pl.BlockSpec((None, CHUNK, N_HEADS * HEAD_DIM), lambda bi, i: (bi, i, 0)),
        out_shape=jax.ShapeDtypeStruct((b, s, N_HEADS * HEAD_DIM), BF16),
        scratch_shapes=[pltpu.VMEM((N_KV, LANES, 2 * CHUNK), BF16),
                        pltpu.VMEM((N_KV, s // L_SLC, CHUNK), F32),
                        pltpu.VMEM((N_KV, 1, 2 * CHUNK), F32),
                        pltpu.VMEM((N_KV, V_ROWS, 2 * CHUNK), F32),
                        pltpu.VMEM((N_KV, HEAD_DIM, 2 * CHUNK), F32),
                        pltpu.VMEM((s // L_SLC, N_KV * CHUNK), F32),
                        pltpu.VMEM((2, TK, 2 * CHUNK), F32),
                        pltpu.VMEM((2, TK, 2 * CHUNK), BF16),
                        pltpu.VMEM((2, 1, 2 * CHUNK), F32),
                        pltpu.VMEM((2, 1, 2 * CHUNK), F32),
                        pltpu.VMEM((2, nc, CHUNK), BF16),
                        pltpu.VMEM((2, nc, CHUNK), BF16)],
        compiler_params=_params(("parallel", "arbitrary")),
        name="attn_prompt",
    )(qt, gt, kcc, vcc, ks_aug, vs_aug, *([kw_aug] * N_WBLK), *([vw_aug] * N_WBLK), ovt, tri, dc, dd, wb)


NQ_PAD = 8


def _softmax_rows(s, mask):
    s = jnp.where(mask, s, NEG)
    mx = jnp.max(s, axis=1, keepdims=True)
    p = jnp.where(mask, jnp.exp(s - mx), 0.0)
    return p / jnp.maximum(jnp.sum(p, axis=1, keepdims=True), 1e-20)


SAMPLES_PER_STEP = 2


def _attn_sample_kernel(pt_ref, *refs, n_pages, past_len, spb):
    del pt_ref
    per = 10 + 2 * n_pages
    slots = [refs[j * per:(j + 1) * per] for j in range(spb)]
    ovt_ref, tri_ref, eexp_ref, o_ref, s_scr = refs[spb * per:]
    qbd = [r[0][...] for r in slots]
    g_refs, kcc_refs, vcc_refs = ([r[k] for r in slots] for k in (1, 2, 3))
    kpages = [r[4:4 + n_pages] for r in slots]
    vpages = [r[4 + n_pages:4 + 2 * n_pages] for r in slots]
    kst, vst, kws, vws, kwt, vwt = ([r[4 + 2 * n_pages + k] for r in slots] for k in range(6))
    nrow = qbd[0].shape[0]
    half = nrow // 2
    page = kpages[0][0].shape[1]
    every = range(spb)

    def t_of(shape):
        return past_len + (lax.broadcasted_iota(jnp.int32, shape, 0) & (NQ_PAD - 1))

    nc = kcc_refs[0].shape[0]
    c_idx = lax.broadcasted_iota(jnp.int32, (nrow, nc), 1)
    cmask = c_idx * STRIDE + (L_CMP - 1) <= t_of((nrow, nc))
    p_c = [_softmax_rows(_dot_nt(qbd[j], kcc_refs[j][...]), cmask) for j in every]
    o_c = [_dot(p_c[j].astype(BF16), vcc_refs[j][...]) for j in every]

    psum = jnp.concatenate([p_c[j][:half] + p_c[j][half:] for j in every]
                           + [jnp.zeros((LANES - spb * half, nc), F32)], axis=0)
    imp = _split_dot_left(ovt_ref[...], psum.T)
    n_idx = lax.broadcasted_iota(jnp.int32, imp.shape, 0)
    cur = (past_len + (lax.broadcasted_iota(jnp.int32, imp.shape, 1) & (NQ_PAD - 1))) // L_SLC
    valid = n_idx <= cur
    forced = valid & ((n_idx == 0) | (n_idx >= cur - 1))
    score = jnp.where(forced, FORCED_SCORE, jnp.where(valid, imp, -jnp.inf))
    thr = _kth_threshold(score, 0)
    above = score > thr
    tie = score == thr
    need = N_SEL - jnp.sum(jnp.where(above, 1.0, 0.0), axis=0, keepdims=True)
    rank = _dot(tri_ref[...], jnp.where(tie, 1.0, 0.0).astype(BF16))
    sel = jnp.where(above | (tie & (rank <= need)), 1.0, 0.0).T.astype(BF16)
    sel_keys = [_dot(jnp.concatenate([sel[j * half:(j + 1) * half]] * 2, axis=0), eexp_ref[...]) for j in every]

    for pg in range(n_pages):
        for j in every:
            s_scr[j, :, pg * page:(pg + 1) * page] = _dot(qbd[j], kpages[j][pg][...].astype(BF16))
    for j in every:
        s_scr[j, :, n_pages * page:(n_pages + 1) * page] = _dot(qbd[j], kst[j][...])
    nk = (n_pages + 1) * page
    tok_ok = lax.broadcasted_iota(jnp.int32, (nrow, nk), 1) <= t_of((nrow, nk))
    p_s = [_softmax_rows(s_scr[j], (sel_keys[j] > 0.5) & tok_ok).astype(BF16) for j in every]
    o_s = [_dot_nt(p_s[j][:, n_pages * page:], vst[j][...]) for j in every]
    for pg in range(n_pages):
        for j in every:
            o_s[j] = o_s[j] + _dot_nt(p_s[j][:, pg * page:(pg + 1) * page], vpages[j][pg][...].astype(BF16))

    wb = kws[0].shape[1]
    nw = wb + page
    rel = t_of((nrow, nw)) - (past_len - wb + lax.broadcasted_iota(jnp.int32, (nrow, nw), 1))
    wmask = (rel >= 0) & (rel < WINDOW)
    sw = [jnp.concatenate([_dot(qbd[j], kws[j][...].astype(BF16)), _dot(qbd[j], kwt[j][...])], axis=1) for j in every]
    p_w = [_softmax_rows(sw[j], wmask).astype(BF16) for j in every]
    o_w = [_dot_nt(p_w[j][:, :wb], vws[j][...].astype(BF16)) + _dot_nt(p_w[j][:, wb:], vwt[j][...]) for j in every]

    row_g = (lax.broadcasted_iota(jnp.int32, o_c[0].shape, 0) // NQ_PAD) & (N_KV - 1)
    lane_g = lax.broadcasted_iota(jnp.int32, o_c[0].shape, 1) // HEAD_DIM
    per_r = N_KV * NQ_PAD
    for j in every:
        g = g_refs[j][...]
        o = g[:, 0:1] * o_c[j] + g[:, 1:2] * o_s[j] + g[:, 2:3] * o_w[j]
        o = jnp.where(row_g == lane_g, o, 0.0)
        for r in range(2):
            acc = o[r * per_r:r * per_r + NQ_PAD]
            for gg in range(1, N_KV):
                acc = acc + o[r * per_r + gg * NQ_PAD:r * per_r + (gg + 1) * NQ_PAD]
            o_ref[j, r] = acc


def _attn_sample(page_table, qbd, gsm, kcc, vcc, cache_k, cache_v, k_tail, v_tail, kw_state, vw_state,
                 kw_tail, vw_tail, ovt, tri, eexp, past_len):
    nb, n_pages = page_table.shape
    spb = SAMPLES_PER_STEP
    kvw = cache_k.shape[1]
    page = cache_k.shape[2]
    nrow = qbd.shape[1]
    const = lambda shape: pl.BlockSpec(shape, lambda b, pt: (0,) * len(shape))
    in_specs, args = [], []
    for j in range(spb):
        per_b = lambda shape, j=j: pl.BlockSpec((None,) + shape, lambda b, pt: (b * spb + j,) + (0,) * len(shape))
        page_spec = lambda pg, j=j: pl.BlockSpec((None, kvw, page), lambda b, pt: (pt[b * spb + j, pg], 0, 0))
        in_specs += ([per_b((nrow, kvw)), per_b((nrow, 8)), per_b(kcc.shape[1:]), per_b(vcc.shape[1:])]
                     + [page_spec(pg) for pg in range(n_pages)] + [page_spec(pg) for pg in range(n_pages)]
                     + [per_b((kvw, page)), per_b((kvw, page)), per_b(kw_state.shape[1:]),
                        per_b(vw_state.shape[1:]), per_b((kvw, page)), per_b((kvw, page))])
        args += ([qbd, gsm, kcc, vcc] + [cache_k] * n_pages + [cache_v] * n_pages
                 + [k_tail, v_tail, kw_state, vw_state, kw_tail, vw_tail])
    in_specs += [const(ovt.shape), const(tri.shape), const(eexp.shape)]
    grid_spec = pltpu.PrefetchScalarGridSpec(
        num_scalar_prefetch=1,
        grid=(nb // spb,),
        in_specs=in_specs,
        out_specs=pl.BlockSpec((None, spb, 2, NQ_PAD, kvw), lambda b, pt: (b, 0, 0, 0, 0)),
        scratch_shapes=[pltpu.VMEM((spb, nrow, (n_pages + 1) * page), F32)],
    )
    out = pl.pallas_call(
        functools.partial(_attn_sample_kernel, n_pages=n_pages, past_len=past_len, spb=spb),
        grid_spec=grid_spec,
        out_shape=jax.ShapeDtypeStruct((nb // spb, spb, 2, NQ_PAD, kvw), F32),
        compiler_params=_params(("parallel",)),
        name="attn_sample",
    )(page_table, *args, ovt, tri, eexp)
    return out.reshape(nb, 2, NQ_PAD, kvw)


FF_SPLIT = 2


def _finish_kernel(x_ref, a_ref, b_ref, gt1_ref, sh2_ref, sc2_ref, gt2_ref, wout_ref, gffn_ref, win_ref, wo2_ref,
                   o_ref):
    half = a_ref.shape[1]
    y = _dot(a_ref[...], wout_ref[0:half, :]) + _dot(b_ref[...], wout_ref[half:, :])
    x1 = x_ref[...] + gt1_ref[...] * y
    ms = jnp.mean(x1 * x1, axis=-1, keepdims=True)
    h = x1 * lax.rsqrt(ms + EPS) * gffn_ref[...]
    hb = (h * (1.0 + sc2_ref[...]) + sh2_ref[...]).astype(BF16)
    d_ff = wo2_ref.shape[0]
    step = d_ff // FF_SPLIT
    acc = None
    for c in range(FF_SPLIT):
        up = _dot(hb, win_ref[:, c * step:(c + 1) * step])
        gate = _dot(hb, win_ref[:, d_ff + c * step:d_ff + (c + 1) * step])
        z = (jax.nn.silu(up) * gate).astype(BF16)
        part = _dot(z, wo2_ref[c * step:(c + 1) * step, :])
        acc = part if acc is None else acc + part
    o_ref[...] = x1 + gt2_ref[...] * acc


def _finish(x, a, b, mods, per_row_mod, rows_per_mod, wout, gffn, win, wo2, tm):
    r, d = x.shape
    if per_row_mod:
        mod_spec = pl.BlockSpec((tm, d), lambda i: (i, 0))
    else:
        tiles_per_mod = rows_per_mod // tm
        mod_spec = pl.BlockSpec((None, 1, d), lambda i: (i // tiles_per_mod, 0, 0))
    single = lambda shape: pl.BlockSpec(shape, lambda i: (0,) * len(shape), pipeline_mode=pl.Buffered(1))
    row = lambda n: pl.BlockSpec((tm, n), lambda i: (i, 0))
    return pl.pallas_call(
        _finish_kernel,
        grid=(r // tm,),
        in_specs=[row(d), row(a.shape[1]), row(b.shape[1]), mod_spec, mod_spec, mod_spec, mod_spec,
                  single(wout.shape), single(gffn.shape), single(win.shape), single(wo2.shape)],
        out_specs=row(d),
        out_shape=jax.ShapeDtypeStruct((r, d), F32),
        compiler_params=_params(("parallel",)),
        name="finish",
    )(x, a, b, *mods, wout, gffn, win, wo2)


def _rope_tables(pos):
    half = HEAD_DIM // 2
    inv = ROPE_THETA ** (-jnp.arange(half, dtype=F32) / half)
    ang = pos.astype(F32)[:, None] * inv[None, :]
    cos, sin = jnp.cos(ang), jnp.sin(ang)
    return jnp.concatenate([cos] * 4, axis=1), jnp.concatenate([-sin, sin, -sin, sin], axis=1)


def _overlap(n_c, n_s):
    c_start = jnp.arange(n_c) * STRIDE
    blk = jnp.arange(n_s)
    return ((c_start[:, None] < (blk[None, :] + 1) * L_SLC)
            & (c_start[:, None] + L_CMP > blk[None, :] * L_SLC)).astype(BF16)


def _tail_page(new_rows, page):
    return jnp.pad(new_rows.transpose(0, 2, 1), ((0, 0), (0, 0), (0, page - new_rows.shape[1]))).astype(BF16)


def kernel(x_prompt, x_sample, cache_k_cmp, cache_v_cmp, cache_k_slc, cache_v_slc, state_k_win, state_v_win,
           page_table, c_prompt, c_sample, w_ada, b_ada, g_mix_norm, g_ffn_norm, w_in, g_sgu, w_sgu, b_sgu,
           g_q, g_k_cmp, g_k_slc, g_k_win, pe_k_cmp, pe_v_cmp, w_ck1, w_ck2, w_cv1, w_cv2, w_out, w_ffn_in,
           w_ffn_out):
    depth = w_in.shape[0]
    assert depth == 1, "single trunk layer"
    nb_p, seq, d = x_prompt.shape
    nb_s, n_new, _ = x_sample.shape
    n_pool, page = cache_k_cmp.shape[1], cache_k_cmp.shape[2]
    kvw = N_KV * HEAD_DIM
    n_pages = page_table.shape[1]
    past_len = n_pages * page
    wb_s = state_k_win.shape[2]
    l = 0

    in_cols = w_in.shape[2]
    w_in_pad = jnp.pad(w_in[l], ((0, 0), (0, 3200 - in_cols))).astype(BF16)
    tile_gain = lambda g, heads: jnp.tile(g, heads).reshape(1, heads * HEAD_DIM)
    gq_t, gkc_t = tile_gain(g_q[l], N_HEADS), tile_gain(g_k_cmp[l], N_KV)
    gks_t, gkw_t = tile_gain(g_k_slc[l], N_KV), tile_gain(g_k_win[l], N_KV)
    gsgu = g_sgu[l].reshape(1, -1)
    gmix = g_mix_norm[l].reshape(1, d)
    gffn = g_ffn_norm[l].reshape(1, d)
    gmat = (jnp.kron(jnp.eye(256 // HEAD_DIM, dtype=F32), jnp.ones((HEAD_DIM, HEAD_DIM), F32)) / HEAD_DIM).astype(BF16)
    w_tril = jnp.where(jnp.tril(jnp.ones((CHUNK, CHUNK), bool)), w_sgu[l], 0)
    wmix_p = w_tril.astype(BF16)
    bmix_p = jnp.repeat(b_sgu[l].T, HEAD_DIM, axis=1)
    eye_s = jnp.eye(CHUNK // n_new, dtype=F32)
    wmix_s = jax.vmap(lambda w: jnp.kron(eye_s, w[:n_new, :n_new]))(w_tril).astype(BF16)
    bmix_s = jnp.tile(jnp.repeat(b_sgu[l].T[:n_new], HEAD_DIM, axis=1), (CHUNK // n_new, 1))
    wout_b = w_out[l].astype(BF16)
    win_b = w_ffn_in[l].astype(BF16)
    wo2_b = w_ffn_out[l].astype(BF16)

    n_c = nb_p + nb_s
    n_c_pad = -(-n_c // 8) * 8
    c_all = jnp.pad(jnp.concatenate([c_prompt, c_sample], axis=0), ((0, n_c_pad - n_c), (0, 0)))
    ada = _ada(c_all, w_ada[l].astype(BF16), b_ada[l].reshape(1, -1))
    mods_p = [m.reshape(nb_p, 1, d) for m in jnp.split(ada[:nb_p], 6, axis=-1)]
    mods_s = [jnp.repeat(m, n_new, axis=0) for m in jnp.split(ada[nb_p:n_c], 6, axis=-1)]

    cos_p, sin_p = _rope_tables(jnp.arange(seq))
    cos_s, sin_s = _rope_tables(past_len + jnp.arange(n_new))
    reps = CHUNK // n_new
    cos_s, sin_s = jnp.tile(cos_s, (reps, 1)), jnp.tile(sin_s, (reps, 1))
    xp = x_prompt.reshape(nb_p * seq, d)
    xs = x_sample.reshape(nb_s * n_new, d)
    gains = (gsgu, gq_t, gkc_t, gks_t, gkw_t)
    blk_of_key = (jnp.arange(seq) // L_SLC) % BIAS_ROWS
    onehot = jnp.pad(jax.nn.one_hot(blk_of_key, LANES - HEAD_DIM, dtype=F32), ((0, 0), (HEAD_DIM, 0)))
    q_scale = HEAD_DIM ** -0.5 * math.log2(math.e)
    (a_p, vn_p, qt, gt, kct, vct, kst, vst, kwt, vwt, ks_aug, vs_aug, kw_aug, vw_aug) = _proj_cols(
        xp, nb_p, seq, mods_p[0], mods_p[1], gmix, w_in_pad, cos_p, sin_p, onehot, *gains, wmix_p, bmix_p, gmat,
        PROMPT_ROWS, q_scale)
    (a_s, vn_s, q_s, kc_s, vc_s, ks_s, vs_s, kw_s, vw_s, gate_s) = _proj_rows(
        xs, mods_s[0], mods_s[1], gmix, w_in_pad, cos_s, sin_s, *gains, wmix_s, bmix_s, gmat, CHUNK)

    n_chunk_p = seq // STRIDE
    n_blk_p = seq // L_SLC
    wk_p = _compress_weights(pe_k_cmp[l], w_ck1[l], w_ck2[l], LANES)
    wv_p = _compress_weights(pe_v_cmp[l], w_cv1[l], w_cv2[l], LANES)
    pages_p = [pl.BlockSpec((None, kvw, LANES), functools.partial(lambda pg, i: (i, 0, pg), pg))
               for pg in range(seq // LANES)]
    kcc_p = _compress_call(kct, pages_p, LANES, (nb_p,), wk_p)
    vcc_p = _compress_call(vct, pages_p, LANES, (nb_p,), wv_p, transposed_out=True)
    ovt = _overlap(n_chunk_p, n_blk_p).T
    tri = jnp.tril(jnp.ones((n_blk_p, n_blk_p), BF16))
    q_lane = jnp.arange(2 * CHUNK)[None, :] % CHUNK
    dc = (jnp.arange(n_chunk_p)[:, None] * STRIDE + (L_CMP - 1) - q_lane).astype(jnp.int32)
    dd = (jnp.arange(TK)[:, None] - q_lane).astype(jnp.int32)
    jq = jnp.arange(N_WBLK * CHUNK)[:, None] - q_lane
    wb = jnp.where((jq > 0) & (jq <= WINDOW), 0.0, NEG).astype(F32)
    b_p = _attn_prompt(qt, gt, kcc_p, vcc_p, ks_aug, vs_aug, kw_aug, vw_aug, ovt, tri, dc, dd, wb)
    y_p = _finish(xp, a_p, b_p.reshape(nb_p * seq, N_HEADS * HEAD_DIM), mods_p[2:], False, seq, wout_b, gffn,
                  win_b, wo2_b, PROMPT_ROWS)

    fm = lambda a: a.transpose(0, 2, 3, 1).reshape(a.shape[0], kvw, a.shape[1])
    ck, cv, cks, cvs = fm(cache_k_cmp[l]), fm(cache_v_cmp[l]), fm(cache_k_slc[l]), fm(cache_v_slc[l])
    wk_s = _compress_weights(pe_k_cmp[l], w_ck1[l], w_ck2[l], HEAD_DIM)
    wv_s = _compress_weights(pe_v_cmp[l], w_cv1[l], w_cv2[l], HEAD_DIM)
    spb = (seq // LANES) // n_pages
    assert nb_s % spb == 0 and page == LANES

    def page_spec(j):
        return pl.BlockSpec((None, kvw, page), lambda i, pt: (pt[i * spb + j // n_pages, j % n_pages], 0, 0))

    pages_s = [page_spec(j) for j in range(spb * n_pages)]
    n_c_s = past_len // STRIDE
    kcc_s = _compress_call(ck, pages_s, page, (nb_s // spb,), wk_s, prefetch=page_table).reshape(nb_s, n_c_s, kvw)
    vcc_s = _compress_call(cv, pages_s, page, (nb_s // spb,), wv_s, prefetch=page_table).reshape(nb_s, n_c_s, kvw)

    q5 = (q_s * (HEAD_DIM ** -0.5)).reshape(nb_s, n_new, N_KV, 2, HEAD_DIM).transpose(0, 3, 2, 1, 4)
    q5 = jnp.pad(q5, ((0, 0), (0, 0), (0, 0), (0, NQ_PAD - n_new), (0, 0)))
    qbd = jnp.einsum("brgqd,gh->brgqhd", q5, jnp.eye(N_KV, dtype=F32))
    qbd = qbd.reshape(nb_s, 2 * N_KV * NQ_PAD, kvw).astype(BF16)
    g5 = gate_s[:, :3 * N_HEADS].reshape(nb_s, n_new, N_KV, 2, 3).transpose(0, 3, 2, 1, 4)
    g5 = jnp.pad(g5, ((0, 0), (0, 0), (0, 0), (0, NQ_PAD - n_new), (0, 5)))
    gsm = g5.reshape(nb_s, 2 * N_KV * NQ_PAD, 8)
    new = lambda a: a.reshape(nb_s, n_new, kvw)
    n_keys = (n_pages + 1) * page
    ov_s = _overlap(n_c_s, LANES).T
    tri_s = jnp.tril(jnp.ones((LANES, LANES), BF16))
    eexp = (jnp.arange(n_keys)[None, :] // L_SLC == jnp.arange(LANES)[:, None]).astype(BF16)
    os_ = _attn_sample(page_table, qbd, gsm, kcc_s, vcc_s, cks, cvs, _tail_page(new(ks_s), page),
                       _tail_page(new(vs_s), page), fm(state_k_win[l]), fm(state_v_win[l]),
                       _tail_page(new(kw_s), page), _tail_page(new(vw_s), page), ov_s, tri_s, eexp, past_len)
    b_s = os_[:, :, :n_new].reshape(nb_s, 2, n_new, N_KV, HEAD_DIM).transpose(0, 2, 3, 1, 4)
    b_s = b_s.reshape(nb_s * n_new, N_HEADS * HEAD_DIM).astype(BF16)

    y_s = _finish(xs, a_s, b_s, mods_s[2:], True, 0, wout_b, gffn, win_b, wo2_b, CHUNK)

    wb_p = min(WINDOW, seq)
    assert seq - ((seq - 1) // CHUNK) * CHUNK == CHUNK, "the prompt ends on a full chunk"
    from_cols = lambda t: t.reshape(1, nb_p, N_KV, HEAD_DIM, seq).transpose(0, 1, 4, 2, 3)
    outs_p = [from_cols(kct), from_cols(vct), from_cols(kst), from_cols(vst),
              from_cols(kwt)[:, :, seq - wb_p:], from_cols(vwt)[:, :, seq - wb_p:], vn_p[None]]
    kv5 = lambda a, nb, t: a.reshape(1, nb, t, N_KV, HEAD_DIM)
    kw_all = jnp.concatenate([state_k_win[l], kv5(kw_s, nb_s, n_new)[0]], axis=1)
    vw_all = jnp.concatenate([state_v_win[l], kv5(vw_s, nb_s, n_new)[0]], axis=1)
    outs_s = [kv5(kc_s, nb_s, n_new), kv5(vc_s, nb_s, n_new), kv5(ks_s, nb_s, n_new), kv5(vs_s, nb_s, n_new),
              kw_all[None, :, n_new:], vw_all[None, :, n_new:], vn_s.reshape(1, nb_s, n_new, -1)]
    return (y_p.reshape(nb_p, seq, d), y_s.reshape(nb_s, n_new, d), *outs_p, *outs_s)
```

```python
import functools
import math

import jax
import jax.numpy as jnp
from jax import lax
from jax.experimental import pallas as pl
from jax.experimental.pallas import tpu as pltpu

F32 = jnp.float32
BF16 = jnp.bfloat16

CHUNK = 128
A_GROUPS = 8
HEAD_DIM = 64
N_HEADS = 8
N_KV = 4
L_CMP = 32
STRIDE = 16
CMP_HID = 256
L_SLC = 64
N_SEL = 16
WINDOW = 512
ROPE_THETA = 10000.0
EPS = 1e-6
NEG = -1e30
SEL_BIAS = -(2.0 ** 100)

LANES = 128
V_ROWS = HEAD_DIM + 16
VMEM_LIMIT = 52 * 1024 * 1024
PROMPT_ROWS = 512

_NT = (((1,), (1,)), ((), ()))


def _dot(a, b):
    return jnp.dot(a, b, preferred_element_type=F32)


def _dot_nt(a, b):
    return lax.dot_general(a, b, _NT, preferred_element_type=F32)


def _split_dot_left(coef, x):
    hi = x.astype(BF16)
    lo = (x - hi.astype(F32)).astype(BF16)
    return _dot(coef, hi) + _dot(coef, lo)


def _split_dot_right(x, coef):
    hi = x.astype(BF16)
    lo = (x - hi.astype(F32)).astype(BF16)
    return _dot(hi, coef) + _dot(lo, coef)


def _params(sem, flags=None):
    return pltpu.CompilerParams(dimension_semantics=sem, vmem_limit_bytes=VMEM_LIMIT, flags=flags)


def _staged(stages, n):
    depth = len(stages)
    for step in range(n + depth - 1):
        for k, stage in enumerate(stages):
            if 0 <= step - k < n:
                stage(step - k)


def _with_ones_row(vt):
    n = vt.shape[1]
    row = lax.broadcasted_iota(jnp.int32, (V_ROWS - HEAD_DIM, n), 0)
    return jnp.concatenate([vt, jnp.where(row == 0, 1.0, 0.0).astype(vt.dtype)], axis=0)


def _ada_kernel(c_ref, w_ref, b_ref, o_ref):
    c = c_ref[...]
    o_ref[...] = _dot(jax.nn.silu(c).astype(BF16), w_ref[...]) + b_ref[...]


def _ada(c, w, b):
    m, k = c.shape
    n = w.shape[1]
    tn = 1024
    return pl.pallas_call(
        _ada_kernel,
        grid=(n // tn,),
        in_specs=[pl.BlockSpec((m, k), lambda j: (0, 0)),
                  pl.BlockSpec((k, tn), lambda j: (0, j)),
                  pl.BlockSpec((1, tn), lambda j: (0, j))],
        out_specs=pl.BlockSpec((m, tn), lambda j: (0, j)),
        out_shape=jax.ShapeDtypeStruct((m, n), F32),
        compiler_params=_params(("parallel",)),
        name="ada",
    )(c, w, b)


def _group_mean_sq(y, g_ref):
    y2 = y * y
    cols = []
    for c in range(y.shape[1] // 256):
        cols.append(_split_dot_right(y2[:, 256 * c:256 * (c + 1)], g_ref[...]))
    return cols[0] if len(cols) == 1 else jnp.concatenate(cols, axis=1)


def _rope(x, cos, sin):
    n = x.shape[1]
    reps = n // LANES
    cos_t = cos if reps == 1 else jnp.concatenate([cos] * reps, axis=1)
    sin_t = sin if reps == 1 else jnp.concatenate([sin] * reps, axis=1)
    lane = lax.broadcasted_iota(jnp.int32, x.shape, 1)
    first_half = (lane & (HEAD_DIM - 1)) < (HEAD_DIM // 2)
    partner = jnp.where(first_half, pltpu.roll(x, n - HEAD_DIM // 2, 1), pltpu.roll(x, HEAD_DIM // 2, 1))
    return x * cos_t + partner * sin_t


def _proj_common(x_ref, sh_ref, sc_ref, gmix_ref, w_ref, cos_ref, sin_ref, gsgu_ref, gq_ref, gkc_ref, gks_ref,
                 gkw_ref, wmix_ref, bmix_ref, gmat_ref, a_ref, vn_ref):
    x = x_ref[...]
    tm = x.shape[0]
    ms = jnp.mean(x * x, axis=-1, keepdims=True)
    h = x * lax.rsqrt(ms + EPS) * gmix_ref[...]
    h = h * (1.0 + sc_ref[...]) + sh_ref[...]
    hb = h.astype(BF16)
    cos = cos_ref[...]
    sin = sin_ref[...]

    def seg(lo, hi):
        return _dot(hb, w_ref[:, lo:hi])

    def head_norm(y, g_ref):
        return y * lax.rsqrt(_group_mean_sq(y, gmat_ref) + EPS) * g_ref[...]

    u = jax.nn.gelu(seg(0, 512))
    v = jax.nn.gelu(seg(512, 1024))
    vn = head_norm(v, gsgu_ref)
    vn_ref[...] = vn[tm - CHUNK:]
    vb = vn.astype(BF16)
    lane = lax.broadcasted_iota(jnp.int32, (CHUNK, LANES), 1)
    low = lane < HEAD_DIM
    for ck in range(tm // CHUNK):
        rows = slice(ck * CHUNK, (ck + 1) * CHUNK)
        for pr in range(A_GROUPS // 2):
            cols = slice(pr * LANES, (pr + 1) * LANES)
            vp = vb[rows, cols]
            mixed = jnp.where(low, _dot(wmix_ref[2 * pr], vp), _dot(wmix_ref[2 * pr + 1], vp))
            mixed = mixed + bmix_ref[:, cols]
            a_ref[rows, cols] = (u[rows, cols] * mixed).astype(a_ref.dtype)

    q = _rope(head_norm(seg(1024, 1536), gq_ref), cos, sin)
    kc = _rope(head_norm(seg(1536, 1792), gkc_ref), cos, sin)
    vc = seg(1792, 2048)
    ks = _rope(head_norm(seg(2048, 2304), gks_ref), cos, sin)
    vs = seg(2304, 2560)
    kw = _rope(head_norm(seg(2560, 2816), gkw_ref), cos, sin)
    vw = seg(2816, 3072)
    gates = jax.nn.sigmoid(seg(3072, 3200))
    return q, kc, vc, ks, vs, kw, vw, gates


def _proj_rows_kernel(*refs):
    ins, (a_ref, vn_ref, q_ref, kc_ref, vc_ref, ks_ref, vs_ref, kw_ref, vw_ref, gate_ref) = refs[:15], refs[15:]
    outs = _proj_common(*ins, a_ref, vn_ref)
    for ref, val in zip((q_ref, kc_ref, vc_ref, ks_ref, vs_ref, kw_ref, vw_ref, gate_ref), outs):
        ref[...] = val


def _proj_cols_kernel(*refs, q_scale):
    ins, oh_ref = refs[:15], refs[15]
    (a_ref, vn_ref, qt_ref, gt_ref, kct_ref, vct_ref, kst_ref, vst_ref, kwt_ref, vwt_ref,
     ksa_ref, vsa_ref, kwa_ref, vwa_ref, stage_ref) = refs[16:]
    q, kc, vc, ks, vs, kw, vw, gates = _proj_common(*ins, a_ref, vn_ref)
    tm = q.shape[0]
    qt_ref[...] = (q * q_scale).T.astype(BF16)
    gt_ref[...] = gates.T[0:gt_ref.shape[0]]
    kct_ref[...] = kc.T
    kst_ref[...] = ks.T
    kwt_ref[...] = kw.T
    stage_ref[0] = vc
    stage_ref[1] = vs
    stage_ref[2] = vw
    vct_ref[...] = stage_ref[0].T
    vst = stage_ref[1].T
    vwt = stage_ref[2].T
    vst_ref[...] = vst
    vwt_ref[...] = vwt
    low = lax.broadcasted_iota(jnp.int32, (tm, LANES), 1) < HEAD_DIM
    onehot = oh_ref[...]
    for g in range(N_KV):
        pair = slice((g // 2) * LANES, (g // 2 + 1) * LANES)
        ks_g, kw_g = ks[:, pair], kw[:, pair]
        if g % 2 == 1:
            ks_g, kw_g = pltpu.roll(ks_g, HEAD_DIM, 1), pltpu.roll(kw_g, HEAD_DIM, 1)
        ksa_ref[g] = jnp.where(low, ks_g, onehot).astype(BF16)
        kwa_ref[g] = jnp.where(low, kw_g, 0.0).astype(BF16)
        rows = slice(g * HEAD_DIM, (g + 1) * HEAD_DIM)
        vsa_ref[g] = _with_ones_row(vst[rows]).astype(BF16)
        vwa_ref[g] = _with_ones_row(vwt[rows]).astype(BF16)


def _proj_in_specs(tm, d, per_row_mod, rows_per_mod, rope_tiles, w_pad, wmix, bmix, gmat):
    if per_row_mod:
        mod_spec = pl.BlockSpec((tm, d), lambda i: (i, 0))
    else:
        tiles_per_mod = rows_per_mod // tm
        mod_spec = pl.BlockSpec((None, 1, d), lambda i: (i // tiles_per_mod, 0, 0))
    const = lambda shape: pl.BlockSpec(shape, lambda i: (0,) * len(shape))
    rope_spec = pl.BlockSpec((tm, LANES), lambda i: (i % rope_tiles, 0))
    w_spec = pl.BlockSpec(w_pad.shape, lambda i: (0, 0), pipeline_mode=pl.Buffered(1))
    return [pl.BlockSpec((tm, d), lambda i: (i, 0)), mod_spec, mod_spec, const((1, d)), w_spec,
            rope_spec, rope_spec,
            const((1, 512)), const((1, 512)), const((1, 256)), const((1, 256)), const((1, 256)),
            const(wmix.shape), const(bmix.shape), const(gmat.shape)]


def _proj_rows(x, sh, sc, gmix, w_pad, cos, sin, gsgu, gq, gkc, gks, gkw, wmix, bmix, gmat, tm):
    r, d = x.shape
    row = lambda n: pl.BlockSpec((tm, n), lambda i: (i, 0))
    out_widths = [512, 512, 512, 256, 256, 256, 256, 256, 256, 128]
    out_dtypes = [BF16] + [F32] * 9
    return pl.pallas_call(
        _proj_rows_kernel,
        grid=(r // tm,),
        in_specs=_proj_in_specs(tm, d, True, 0, cos.shape[0] // tm, w_pad, wmix, bmix, gmat),
        out_specs=[row(n) for n in out_widths],
        out_shape=[jax.ShapeDtypeStruct((r, n), dt) for n, dt in zip(out_widths, out_dtypes)],
        compiler_params=_params(("parallel",)),
        name="proj_rows",
    )(x, sh, sc, gmix, w_pad, cos, sin, gsgu, gq, gkc, gks, gkw, wmix, bmix, gmat)


def _proj_cols(x, nb, seq, sh, sc, gmix, w_pad, cos, sin, onehot, gsgu, gq, gkc, gks, gkw, wmix, bmix, gmat, tm,
               q_scale):
    r, d = x.shape
    tpb = seq // tm
    kvw = N_KV * HEAD_DIM
    bi = lambda i: i // tpb
    ti = lambda i: i % tpb
    row = lambda n: pl.BlockSpec((tm, n), lambda i: (i, 0))
    colt = lambda n: pl.BlockSpec((None, n, tm), lambda i: (bi(i), 0, ti(i)))
    out_specs = [row(512),
                 pl.BlockSpec((None, CHUNK, 512), lambda i: (bi(i), 0, 0)),
                 colt(512), colt(32),
                 colt(kvw), colt(kvw), colt(kvw), colt(kvw), colt(kvw), colt(kvw),
                 pl.BlockSpec((None, N_KV, tm, LANES), lambda i: (bi(i), 0, ti(i), 0)),
                 pl.BlockSpec((None, N_KV, V_ROWS, tm), lambda i: (bi(i), 0, 0, ti(i))),
                 pl.BlockSpec((None, N_KV, tm, LANES), lambda i: (bi(i), 0, ti(i), 0)),
                 pl.BlockSpec((None, N_KV, V_ROWS, tm), lambda i: (bi(i), 0, 0, ti(i)))]
    sds = jax.ShapeDtypeStruct
    out_shape = [sds((r, 512), BF16), sds((nb, CHUNK, 512), F32), sds((nb, 512, seq), BF16),
                 sds((nb, 32, seq), F32)] + [sds((nb, kvw, seq), F32)] * 6 + [
                 sds((nb, N_KV, seq, LANES), BF16), sds((nb, N_KV, V_ROWS, seq), BF16),
                 sds((nb, N_KV, seq, LANES), BF16), sds((nb, N_KV, V_ROWS, seq), BF16)]
    in_specs = _proj_in_specs(tm, d, False, seq, tpb, w_pad, wmix, bmix, gmat)
    in_specs.append(pl.BlockSpec((tm, LANES), lambda i: (ti(i), 0)))
    return pl.pallas_call(
        functools.partial(_proj_cols_kernel, q_scale=q_scale),
        grid=(r // tm,),
        in_specs=in_specs,
        out_specs=out_specs,
        out_shape=out_shape,
        scratch_shapes=[pltpu.VMEM((3, tm, kvw), F32)],
        compiler_params=_params(("arbitrary",)),
        name="proj_cols",
    )(x, sh, sc, gmix, w_pad, cos, sin, gsgu, gq, gkc, gks, gkw, wmix, bmix, gmat, onehot)


def _compress_kernel(*refs, n_prefetch, n_page, n_item, transposed_out):
    refs = refs[n_prefetch:]
    page_refs = refs[:n_page]
    (w1_ref, w1a_ref, w1b_ref, pe_ref, w2_ref, o_ref,
     xs_ref, lhs_ref, a_ref, b_ref, hid_ref, stage_ref) = refs[n_page:]
    page = page_refs[0].shape[1]
    cpp = page // STRIDE
    ppi = n_page // n_item
    m = ppi * cpp
    rows = N_KV * m
    pitch = xs_ref.shape[2] // STRIDE
    low = lax.broadcasted_iota(jnp.int32, (m, LANES), 1) < HEAD_DIM
    pe_term = _dot(jnp.broadcast_to(pe_ref[...], (8, pe_ref.shape[1])).astype(BF16), w1_ref[...])[0:1]
    b_ref[rows:rows + 8, :] = jnp.zeros((8, CMP_HID), F32)

    def planes(it):
        for pg in range(ppi):
            xt = page_refs[it * ppi + pg][...].T
            for pp in range(N_KV // 2):
                for c in range(cpp):
                    xs_ref[it % 2, pp, pl.ds(pg * cpp + c, STRIDE, stride=pitch), :] = (
                        xt[c * STRIDE:(c + 1) * STRIDE, pp * LANES:(pp + 1) * LANES])

    def relayout(it):
        for rr in range(STRIDE // 2):
            cols = slice(rr * LANES, (rr + 1) * LANES)
            for pp in range(N_KV // 2):
                p0 = xs_ref[it % 2, pp, 2 * rr * pitch:2 * rr * pitch + m, :]
                p1 = xs_ref[it % 2, pp, (2 * rr + 1) * pitch:(2 * rr + 1) * pitch + m, :]
                r0 = pltpu.roll(p0, HEAD_DIM, 1)
                r1 = pltpu.roll(p1, HEAD_DIM, 1)
                g0 = 2 * pp
                lhs_ref[it % 2, g0 * m:(g0 + 1) * m, cols] = jnp.where(low, p0, r1).astype(BF16)
                lhs_ref[it % 2, (g0 + 1) * m:(g0 + 2) * m, cols] = jnp.where(low, r0, p1).astype(BF16)

    def mlp(it):
        a_ref[...] = _dot(lhs_ref[it % 2], w1a_ref[...])
        b_ref[0:rows, :] = _dot(lhs_ref[it % 2], w1b_ref[...])
        hid_ref[...] = jax.nn.gelu(a_ref[...] + b_ref[pl.ds(1, rows), :] + pe_term).astype(BF16)
        out = _dot(hid_ref[0:m, :], w2_ref[0])
        for g in range(1, N_KV):
            out = out + _dot(hid_ref[g * m:(g + 1) * m, :], w2_ref[g])
        if transposed_out:
            stage_ref[...] = out
            out_t = stage_ref[...].T
            for g in range(N_KV):
                o_ref[g] = _with_ones_row(out_t[g * LANES:g * LANES + HEAD_DIM]).astype(o_ref.dtype)
        else:
            o_ref[it * m:(it + 1) * m, :] = out.astype(o_ref.dtype)

    _staged([planes, relayout, mlp], n_item)


def _compress_call(page_array, page_specs, page, grid, weights, prefetch=None, transposed_out=False, n_item=1):
    n_page = len(page_specs)
    assert not (transposed_out and n_item > 1)
    m_all = n_page * page // STRIDE
    m = m_all // n_item
    n_out = weights[-1].shape[2]
    nsp = 0 if prefetch is None else 1
    const = lambda shape: pl.BlockSpec(shape, lambda *a: (0,) * len(shape))
    if transposed_out:
        out_spec = pl.BlockSpec((None, N_KV, V_ROWS, m_all), lambda i, *a: (i, 0, 0, 0))
        out_shape = jax.ShapeDtypeStruct((grid[0], N_KV, V_ROWS, m_all), BF16)
    else:
        out_spec = pl.BlockSpec((None, m_all, n_out), lambda i, *a: (i, 0, 0))
        out_shape = jax.ShapeDtypeStruct((grid[0], m_all, n_out), BF16)
    n_buf = min(n_item, 2)
    grid_spec = pltpu.PrefetchScalarGridSpec(
        num_scalar_prefetch=nsp,
        grid=grid,
        in_specs=list(page_specs) + [const(w.shape) for w in weights],
        out_specs=out_spec,
        scratch_shapes=[pltpu.VMEM((n_buf, N_KV // 2, STRIDE * (m + 8), LANES), F32),
                        pltpu.VMEM((n_buf, N_KV * m, STRIDE * HEAD_DIM), BF16),
                        pltpu.VMEM((N_KV * m, CMP_HID), F32),
                        pltpu.VMEM((N_KV * m + 8, CMP_HID), F32),
                        pltpu.VMEM((N_KV * m, CMP_HID), BF16),
                        pltpu.VMEM((m, n_out), F32)],
    )
    args = ([] if prefetch is None else [prefetch]) + [page_array] * n_page + list(weights)
    return pl.pallas_call(
        functools.partial(_compress_kernel, n_prefetch=nsp, n_page=n_page, n_item=n_item,
                          transposed_out=transposed_out),
        grid_spec=grid_spec,
        out_shape=out_shape,
        compiler_params=_params(("parallel",)),
        name="compress",
    )(*args)


def _compress_weights(pe, w1, w2, head_stride):
    half = STRIDE * HEAD_DIM
    w1b = w1.astype(BF16)
    w2p = jnp.pad(w2, ((0, 0), (0, head_stride - HEAD_DIM)))
    w2blk = jnp.einsum("gh,kd->gkhd", jnp.eye(N_KV, dtype=F32), w2p)
    w2blk = w2blk.reshape(N_KV, CMP_HID, N_KV * head_stride).astype(BF16)
    return w1b, w1b[:half], w1b[half:], pe.reshape(1, L_CMP * HEAD_DIM), w2blk


N_FORCED = 3


def _select_blocks(imp, valid, forced, tri_ref):
    rest = jnp.where(valid & jnp.logical_not(forced), imp, -jnp.inf)
    k = N_SEL - jnp.sum(jnp.where(forced, 1.0, 0.0), axis=0, keepdims=True)

    def body(_, carry):
        thr, cnt = carry
        mx = jnp.max(jnp.where(rest < thr, rest, -jnp.inf), axis=0, keepdims=True)
        c = jnp.sum(jnp.where(rest >= mx, 1.0, 0.0), axis=0, keepdims=True)
        upd = cnt < k
        return jnp.where(upd, mx, thr), jnp.where(upd, c, cnt)

    thr, _ = lax.fori_loop(0, N_SEL - N_FORCED, body, (jnp.full(k.shape, jnp.inf, F32), jnp.zeros(k.shape, F32)))
    above = rest > thr
    tie = (rest == thr) & jnp.logical_not(forced)
    need = k - jnp.sum(jnp.where(above, 1.0, 0.0), axis=0, keepdims=True)
    rank = _dot(tri_ref[...], jnp.where(tie, 1.0, 0.0).astype(BF16))
    return forced | above | (tie & (rank <= need))


TK = 1024
BIAS_ROWS = TK // L_SLC
N_WBLK = WINDOW // CHUNK + 1


def _attn_prompt_kernel(qt_ref, gt_ref, kcc_ref, vcc_ref, ks_ref, vs_ref, *refs):
    kw_refs, vw_refs = refs[:N_WBLK], refs[N_WBLK:2 * N_WBLK]
    (ovt_ref, tri_ref, dc_ref, dd_ref, wb_ref, o_ref,
     qaug_ref, sel_ref, m_ref, acc_ref, ob_ref, sc_ref, s_ref, p_ref, tmax_ref, alpha_ref, hi_ref,
     lo_ref) = refs[2 * N_WBLK:]
    i = pl.program_id(1)
    qs = i * CHUNK
    nq = CHUNK
    gt = gt_ref[...]

    def gate(g, branch):
        return jnp.concatenate([gt[6 * g + branch:6 * g + branch + 1],
                                gt[6 * g + 3 + branch:6 * g + 3 + branch + 1]], axis=1)

    nc = kcc_ref.shape[0]
    ns = sel_ref.shape[1]
    n_idx = lax.broadcasted_iota(jnp.int32, (ns, nq), 0)
    cur = (qs + lax.broadcasted_iota(jnp.int32, (ns, nq), 1)) // L_SLC
    valid = n_idx <= cur
    forced = valid & ((n_idx == 0) | (n_idx >= cur - 1))

    def cmp_scores(g):
        base = 2 * g * HEAD_DIM
        qaug_ref[g, 0:HEAD_DIM, :] = jnp.concatenate(
            [qt_ref[base:base + HEAD_DIM, :], qt_ref[base + HEAD_DIM:base + 2 * HEAD_DIM, :]], axis=1)
        qaug_ref[g, HEAD_DIM:, :] = jnp.zeros((LANES - HEAD_DIM, 2 * nq), BF16)
        s = jnp.where(dc_ref[...] <= qs, _dot(kcc_ref[:, g * LANES:(g + 1) * LANES], qaug_ref[g]), NEG)
        s_ref[g % 2, 0:nc, :] = s
        mx = jnp.max(s, axis=0, keepdims=True)
        tmax_ref[g % 2] = jnp.where(mx > 0.5 * NEG, mx, 0.0)
        m_ref[g] = jnp.full((1, 2 * nq), NEG, F32)
        acc_ref[g] = jnp.zeros((V_ROWS, 2 * nq), F32)

    def cmp_exps(g):
        p = jnp.exp2(s_ref[g % 2, 0:nc, :] - tmax_ref[g % 2])
        inv = 1.0 / jnp.maximum(jnp.sum(p, axis=0, keepdims=True), 1e-20)
        alpha_ref[g % 2] = inv
        p_ref[g % 2, 0:nc, :] = p.astype(BF16)
        psum = p[:, :nq] * inv[:, :nq] + p[:, nq:] * inv[:, nq:]
        hi = psum.astype(BF16)
        hi_ref[g % 2] = hi
        lo_ref[g % 2] = (psum - hi.astype(F32)).astype(BF16)

    def cmp_values(g):
        oc = _dot(vcc_ref[g], p_ref[g % 2, 0:nc, :])
        ob_ref[g] = gate(g, 0) * (oc[0:HEAD_DIM] * alpha_ref[g % 2])
        imp = _dot(ovt_ref[...], hi_ref[g % 2]) + _dot(ovt_ref[...], lo_ref[g % 2])
        sc_ref[:, g * nq:(g + 1) * nq] = imp

    _staged([cmp_scores, cmp_exps, cmp_values], N_KV)

    sel = _select_blocks(sc_ref[...], jnp.concatenate([valid] * N_KV, axis=1),
                         jnp.concatenate([forced] * N_KV, axis=1), tri_ref)
    before = lax.broadcasted_iota(jnp.int32, sel.shape, 0) < 2 * i
    sel_bias = jnp.where(sel & before, 0.0, SEL_BIAS)
    for g in range(N_KV):
        sel_ref[g] = sel_bias[:, g * nq:(g + 1) * nq]

    last_tile = ks_ref.shape[1] // TK - 1
    half = TK // 2

    def scores_half(kt, g, buf, h):
        kc = jnp.clip(kt, 0, last_tile)
        k0 = pl.multiple_of(kc * TK, TK)
        if h == 0:
            bias = sel_ref[g, pl.ds(pl.multiple_of(kc * BIAS_ROWS, BIAS_ROWS), BIAS_ROWS), :]
            qaug_ref[g, HEAD_DIM:HEAD_DIM + BIAS_ROWS, :] = jnp.concatenate([bias, bias], axis=1).astype(BF16)
        rows = slice(h * half, (h + 1) * half)
        st = _dot(ks_ref[g, pl.ds(k0 + h * half, half), :], qaug_ref[g])
        s_ref[buf, rows, :] = st
        cmax = jnp.max(st, axis=0, keepdims=True)
        tmax_ref[buf] = cmax if h == 0 else jnp.maximum(tmax_ref[buf], cmax)

    def exps_half(g, buf, h):
        if h == 0:
            m_old = m_ref[g]
            m_new = jnp.maximum(m_old, tmax_ref[buf])
            m_ref[g] = m_new
            alpha_ref[buf] = jnp.exp2(m_old - m_new)
        m_new = m_ref[g]
        for c in range(2):
            rows = slice(h * half + c * (half // 2), h * half + (c + 1) * (half // 2))
            p_ref[buf, rows, :] = jnp.exp2(s_ref[buf, rows, :] - m_new).astype(BF16)

    def values_half(kt, g, buf, h):
        k0 = pl.multiple_of(jnp.clip(kt, 0, last_tile) * TK, TK)
        rows = slice(h * half, (h + 1) * half)
        part = _dot(vs_ref[g, :, pl.ds(k0 + h * half, half)], p_ref[buf, rows, :])
        acc_ref[g] = (alpha_ref[buf] * acc_ref[g] if h == 0 else acc_ref[g]) + part

    def substep(kt, g):
        g_next, kt_next = (g + 1) % N_KV, kt + (g + 1) // N_KV
        g_prev, kt_prev = (g - 1) % N_KV, kt - (1 if g == 0 else 0)
        other = (g + 1) % 2
        for h in range(2):
            scores_half(kt_next, g_next, other, h)
            exps_half(g, g % 2, h)
            values_half(kt_prev, g_prev, other, h)

    def tile(kt, carry):
        for g in range(N_KV):
            substep(kt, g)
        return carry

    n_tiles = (qs + TK - 1) // TK
    p_ref[1] = jnp.zeros(p_ref.shape[1:], BF16)
    alpha_ref[1] = jnp.ones(alpha_ref.shape[1:], F32)
    for h in range(2):
        scores_half(0, 0, 0, h)
    lax.fori_loop(0, n_tiles, tile, 0)
    for h in range(2):
        values_half(n_tiles - 1, N_KV - 1, 1, h)

    own = pl.ds(pl.multiple_of(qs, CHUNK), CHUNK)
    causal = dd_ref[0:CHUNK, :] <= 0
    for g in range(N_KV):
        qaug_ref[g, HEAD_DIM:HEAD_DIM + BIAS_ROWS, :] = jnp.zeros((BIAS_ROWS, 2 * nq), BF16)
        sd = jnp.where(causal, _dot(ks_ref[g, own, :], qaug_ref[g]), NEG)
        m_old = m_ref[g]
        m_new = jnp.maximum(m_old, jnp.max(sd, axis=0, keepdims=True))
        pd = jnp.exp2(sd - m_new).astype(BF16)
        acc = jnp.exp2(m_old - m_new) * acc_ref[g] + _dot(vs_ref[g, :, own], pd)
        ob_ref[g] = ob_ref[g] + gate(g, 1) * (acc[0:HEAD_DIM] * (1.0 / acc[HEAD_DIM:HEAD_DIM + 1]))

    nw = N_WBLK * CHUNK
    outs = [None] * N_KV

    def win_scores(g):
        kwin = jnp.concatenate([r[g] for r in kw_refs], axis=0)
        bias = jnp.concatenate(
            [jnp.where(i - (N_WBLK - 1) + j < 0, NEG, wb_ref[j * CHUNK:(j + 1) * CHUNK, :]) for j in range(N_WBLK)],
            axis=0)
        sw = _dot(kwin, qaug_ref[g]) + bias
        s_ref[g % 2, 0:nw, :] = sw
        tmax_ref[g % 2] = jnp.max(sw, axis=0, keepdims=True)

    def win_exps(g):
        p_ref[g % 2, 0:nw, :] = jnp.exp2(s_ref[g % 2, 0:nw, :] - tmax_ref[g % 2]).astype(BF16)

    def win_values(g):
        vwin = jnp.concatenate([r[g] for r in vw_refs], axis=1)
        ow = _dot(vwin, p_ref[g % 2, 0:nw, :])
        ob = ob_ref[g] + gate(g, 2) * (ow[0:HEAD_DIM] * (1.0 / ow[HEAD_DIM:HEAD_DIM + 1]))
        outs[g] = [ob[:, :nq], ob[:, nq:]]

    _staged([win_scores, win_exps, win_values], N_KV)
    o_ref[...] = jnp.concatenate(sum(outs, []), axis=0).T.astype(o_ref.dtype)


def _attn_prompt(qt, gt, kcc, vcc, ks_aug, vs_aug, kw_aug, vw_aug, ovt, tri, dc, dd, wb):
    b, _, s = qt.shape
    nqb = s // CHUNK
    nc = kcc.shape[1]
    first = WINDOW // CHUNK
    whole = lambda shape: pl.BlockSpec((None,) + shape, lambda bi, i: (bi,) + (0,) * len(shape),
                                       pipeline_mode=pl.Buffered(1))
    const = lambda a: pl.BlockSpec(a.shape, lambda bi, i: (0,) * a.ndim)

    def kw_spec(j):
        return pl.BlockSpec((None, N_KV, CHUNK, LANES), lambda bi, i: (bi, 0, jnp.maximum(i - first + j, 0), 0))

    def vw_spec(j):
        return pl.BlockSpec((None, N_KV, V_ROWS, CHUNK), lambda bi, i: (bi, 0, 0, jnp.maximum(i - first + j, 0)))

    in_specs = ([pl.BlockSpec((None, N_HEADS * HEAD_DIM, CHUNK), lambda bi, i: (bi, 0, i)),
                 pl.BlockSpec((None, 32, CHUNK), lambda bi, i: (bi, 0, i)),
                 pl.BlockSpec((None, nc, N_KV * LANES), lambda bi, i: (bi, 0, 0)),
                 pl.BlockSpec((None, N_KV, V_ROWS, nc), lambda bi, i: (bi, 0, 0, 0)),
                 whole((N_KV, s, LANES)), whole((N_KV, V_ROWS, s))]
                + [kw_spec(j) for j in range(N_WBLK)] + [vw_spec(j) for j in range(N_WBLK)]
                + [const(ovt), const(tri), const(dc), const(dd), const(wb)])
    return pl.pallas_call(
        _attn_prompt_kernel,
        grid=(b, nqb),
        in_specs=in_specs,
        out_specs=pl.BlockSpec((None, CHUNK, N_HEADS * HEAD_DIM), lambda bi, i: (bi, i, 0)),
        out_shape=jax.ShapeDtypeStruct((b, s, N_HEADS * HEAD_DIM), BF16),
        scratch_shapes=[pltpu.VMEM((N_KV, LANES, 2 * CHUNK), BF16),
                        pltpu.VMEM((N_KV, s // L_SLC, CHUNK), F32),
                        pltpu.VMEM((N_KV, 1, 2 * CHUNK), F32),
                        pltpu.VMEM((N_KV, V_ROWS, 2 * CHUNK), F32),
                        pltpu.VMEM((N_KV, HEAD_DIM, 2 * CHUNK), F32),
                        pltpu.VMEM((s // L_SLC, N_KV * CHUNK), F32),
                        pltpu.VMEM((2, TK, 2 * CHUNK), F32),
                        pltpu.VMEM((2, TK, 2 * CHUNK), BF16),
                        pltpu.VMEM((2, 1, 2 * CHUNK), F32),
                        pltpu.VMEM((2, 1, 2 * CHUNK), F32),
                        pltpu.VMEM((2, nc, CHUNK), BF16),
                        pltpu.VMEM((2, nc, CHUNK), BF16)],
        compiler_params=_params(("parallel", "arbitrary")),
        name="attn_prompt",
    )(qt, gt, kcc, vcc, ks_aug, vs_aug, *([kw_aug] * N_WBLK), *([vw_aug] * N_WBLK), ovt, tri, dc, dd, wb)


NQ_PAD = 8


def _softmax_rows(s, mask):
    s = jnp.where(mask, s, NEG)
    mx = jnp.max(s, axis=1, keepdims=True)
    p = jnp.where(mask, jnp.exp(s - mx), 0.0)
    return p / jnp.maximum(jnp.sum(p, axis=1, keepdims=True), 1e-20)


SAMPLES_PER_STEP = 2


def _attn_sample_kernel(pt_ref, *refs, n_pages, past_len, spb):
    del pt_ref
    per = 10 + 2 * n_pages
    slots = [refs[j * per:(j + 1) * per] for j in range(spb)]
    ovt_ref, tri_ref, eexp_ref, o_ref, s_scr = refs[spb * per:]
    qbd = [r[0][...] for r in slots]
    g_refs, kcc_refs, vcc_refs = ([r[k] for r in slots] for k in (1, 2, 3))
    kpages = [r[4:4 + n_pages] for r in slots]
    vpages = [r[4 + n_pages:4 + 2 * n_pages] for r in slots]
    kst, vst, kws, vws, kwt, vwt = ([r[4 + 2 * n_pages + k] for r in slots] for k in range(6))
    nrow = qbd[0].shape[0]
    half = nrow // 2
    page = kpages[0][0].shape[1]
    every = range(spb)

    def t_of(shape):
        return past_len + (lax.broadcasted_iota(jnp.int32, shape, 0) & (NQ_PAD - 1))

    nc = kcc_refs[0].shape[0]
    c_idx = lax.broadcasted_iota(jnp.int32, (nrow, nc), 1)
    cmask = c_idx * STRIDE + (L_CMP - 1) <= t_of((nrow, nc))
    p_c = [_softmax_rows(_dot_nt(qbd[j], kcc_refs[j][...]), cmask) for j in every]
    o_c = [_dot(p_c[j].astype(BF16), vcc_refs[j][...]) for j in every]

    psum = jnp.concatenate([p_c[j][:half] + p_c[j][half:] for j in every]
                           + [jnp.zeros((LANES - spb * half, nc), F32)], axis=0)
    imp = _split_dot_left(ovt_ref[...], psum.T)
    n_idx = lax.broadcasted_iota(jnp.int32, imp.shape, 0)
    cur = (past_len + (lax.broadcasted_iota(jnp.int32, imp.shape, 1) & (NQ_PAD - 1))) // L_SLC
    valid = n_idx <= cur
    forced = valid & ((n_idx == 0) | (n_idx >= cur - 1))
    sel = jnp.where(_select_blocks(imp, valid, forced, tri_ref), 1.0, 0.0).T.astype(BF16)
    sel_keys = [_dot(jnp.concatenate([sel[j * half:(j + 1) * half]] * 2, axis=0), eexp_ref[...]) for j in every]

    for pg in range(n_pages):
        for j in every:
            s_scr[j, :, pg * page:(pg + 1) * page] = _dot(qbd[j], kpages[j][pg][...].astype(BF16))
    for j in every:
        s_scr[j, :, n_pages * page:(n_pages + 1) * page] = _dot(qbd[j], kst[j][...])
    nk = (n_pages + 1) * page
    tok_ok = lax.broadcasted_iota(jnp.int32, (nrow, nk), 1) <= t_of((nrow, nk))
    p_s = [_softmax_rows(s_scr[j], (sel_keys[j] > 0.5) & tok_ok).astype(BF16) for j in every]
    o_s = [_dot_nt(p_s[j][:, n_pages * page:], vst[j][...]) for j in every]
    for pg in range(n_pages):
        for j in every:
            o_s[j] = o_s[j] + _dot_nt(p_s[j][:, pg * page:(pg + 1) * page], vpages[j][pg][...].astype(BF16))

    wb = kws[0].shape[1]
    nw = wb + page
    rel = t_of((nrow, nw)) - (past_len - wb + lax.broadcasted_iota(jnp.int32, (nrow, nw), 1))
    wmask = (rel >= 0) & (rel < WINDOW)
    sw = [jnp.concatenate([_dot(qbd[j], kws[j][...].astype(BF16)), _dot(qbd[j], kwt[j][...])], axis=1) for j in every]
    p_w = [_softmax_rows(sw[j], wmask).astype(BF16) for j in every]
    o_w = [_dot_nt(p_w[j][:, :wb], vws[j][...].astype(BF16)) + _dot_nt(p_w[j][:, wb:], vwt[j][...]) for j in every]

    row_g = (lax.broadcasted_iota(jnp.int32, o_c[0].shape, 0) // NQ_PAD) & (N_KV - 1)
    lane_g = lax.broadcasted_iota(jnp.int32, o_c[0].shape, 1) // HEAD_DIM
    per_r = N_KV * NQ_PAD
    for j in every:
        g = g_refs[j][...]
        o = g[:, 0:1] * o_c[j] + g[:, 1:2] * o_s[j] + g[:, 2:3] * o_w[j]
        o = jnp.where(row_g == lane_g, o, 0.0)
        for r in range(2):
            acc = o[r * per_r:r * per_r + NQ_PAD]
            for gg in range(1, N_KV):
                acc = acc + o[r * per_r + gg * NQ_PAD:r * per_r + (gg + 1) * NQ_PAD]
            o_ref[j, r] = acc


def _attn_sample(page_table, qbd, gsm, kcc, vcc, cache_k, cache_v, k_tail, v_tail, kw_state, vw_state,
                 kw_tail, vw_tail, ovt, tri, eexp, past_len):
    nb, n_pages = page_table.shape
    spb = SAMPLES_PER_STEP
    kvw = cache_k.shape[1]
    page = cache_k.shape[2]
    nrow = qbd.shape[1]
    const = lambda shape: pl.BlockSpec(shape, lambda b, pt: (0,) * len(shape))
    in_specs, args = [], []
    for j in range(spb):
        per_b = lambda shape, j=j: pl.BlockSpec((None,) + shape, lambda b, pt: (b * spb + j,) + (0,) * len(shape))
        page_spec = lambda pg, j=j: pl.BlockSpec((None, kvw, page), lambda b, pt: (pt[b * spb + j, pg], 0, 0))
        in_specs += ([per_b((nrow, kvw)), per_b((nrow, 8)), per_b(kcc.shape[1:]), per_b(vcc.shape[1:])]
                     + [page_spec(pg) for pg in range(n_pages)] + [page_spec(pg) for pg in range(n_pages)]
                     + [per_b((kvw, page)), per_b((kvw, page)), per_b(kw_state.shape[1:]),
                        per_b(vw_state.shape[1:]), per_b((kvw, page)), per_b((kvw, page))])
        args += ([qbd, gsm, kcc, vcc] + [cache_k] * n_pages + [cache_v] * n_pages
                 + [k_tail, v_tail, kw_state, vw_state, kw_tail, vw_tail])
    in_specs += [const(ovt.shape), const(tri.shape), const(eexp.shape)]
    grid_spec = pltpu.PrefetchScalarGridSpec(
        num_scalar_prefetch=1,
        grid=(nb // spb,),
        in_specs=in_specs,
        out_specs=pl.BlockSpec((None, spb, 2, NQ_PAD, kvw), lambda b, pt: (b, 0, 0, 0, 0)),
        scratch_shapes=[pltpu.VMEM((spb, nrow, (n_pages + 1) * page), F32)],
    )
    out = pl.pallas_call(
        functools.partial(_attn_sample_kernel, n_pages=n_pages, past_len=past_len, spb=spb),
        grid_spec=grid_spec,
        out_shape=jax.ShapeDtypeStruct((nb // spb, spb, 2, NQ_PAD, kvw), F32),
        compiler_params=_params(("parallel",)),
        name="attn_sample",
    )(page_table, *args, ovt, tri, eexp)
    return out.reshape(nb, 2, NQ_PAD, kvw)


FF_SPLIT = 2


def _finish_kernel(x_ref, a_ref, b_ref, gt1_ref, sh2_ref, sc2_ref, gt2_ref, wout_ref, gffn_ref, win_ref, wo2_ref,
                   o_ref):
    half = a_ref.shape[1]
    y = _dot(a_ref[...], wout_ref[0:half, :]) + _dot(b_ref[...], wout_ref[half:, :])
    x1 = x_ref[...] + gt1_ref[...] * y
    ms = jnp.mean(x1 * x1, axis=-1, keepdims=True)
    h = x1 * lax.rsqrt(ms + EPS) * gffn_ref[...]
    hb = (h * (1.0 + sc2_ref[...]) + sh2_ref[...]).astype(BF16)
    d_ff = wo2_ref.shape[0]
    step = d_ff // FF_SPLIT
    acc = None
    for c in range(FF_SPLIT):
        up = _dot(hb, win_ref[:, c * step:(c + 1) * step])
        gate = _dot(hb, win_ref[:, d_ff + c * step:d_ff + (c + 1) * step])
        z = (jax.nn.silu(up) * gate).astype(BF16)
        part = _dot(z, wo2_ref[c * step:(c + 1) * step, :])
        acc = part if acc is None else acc + part
    o_ref[...] = x1 + gt2_ref[...] * acc


def _finish(x, a, b, mods, per_row_mod, rows_per_mod, wout, gffn, win, wo2, tm):
    r, d = x.shape
    if per_row_mod:
        mod_spec = pl.BlockSpec((tm, d), lambda i: (i, 0))
    else:
        tiles_per_mod = rows_per_mod // tm
        mod_spec = pl.BlockSpec((None, 1, d), lambda i: (i // tiles_per_mod, 0, 0))
    single = lambda shape: pl.BlockSpec(shape, lambda i: (0,) * len(shape), pipeline_mode=pl.Buffered(1))
    row = lambda n: pl.BlockSpec((tm, n), lambda i: (i, 0))
    return pl.pallas_call(
        _finish_kernel,
        grid=(r // tm,),
        in_specs=[row(d), row(a.shape[1]), row(b.shape[1]), mod_spec, mod_spec, mod_spec, mod_spec,
                  single(wout.shape), single(gffn.shape), single(win.shape), single(wo2.shape)],
        out_specs=row(d),
        out_shape=jax.ShapeDtypeStruct((r, d), F32),
        compiler_params=_params(("parallel",)),
        name="finish",
    )(x, a, b, *mods, wout, gffn, win, wo2)


def _rope_tables(pos):
    half = HEAD_DIM // 2
    inv = ROPE_THETA ** (-jnp.arange(half, dtype=F32) / half)
    ang = pos.astype(F32)[:, None] * inv[None, :]
    cos, sin = jnp.cos(ang), jnp.sin(ang)
    return jnp.concatenate([cos] * 4, axis=1), jnp.concatenate([-sin, sin, -sin, sin], axis=1)


def _overlap(n_c, n_s):
    c_start = jnp.arange(n_c) * STRIDE
    blk = jnp.arange(n_s)
    return ((c_start[:, None] < (blk[None, :] + 1) * L_SLC)
            & (c_start[:, None] + L_CMP > blk[None, :] * L_SLC)).astype(BF16)


def _tail_page(new_rows, page):
    return jnp.pad(new_rows.transpose(0, 2, 1), ((0, 0), (0, 0), (0, page - new_rows.shape[1]))).astype(BF16)


def kernel(x_prompt, x_sample, cache_k_cmp, cache_v_cmp, cache_k_slc, cache_v_slc, state_k_win, state_v_win,
           page_table, c_prompt, c_sample, w_ada, b_ada, g_mix_norm, g_ffn_norm, w_in, g_sgu, w_sgu, b_sgu,
           g_q, g_k_cmp, g_k_slc, g_k_win, pe_k_cmp, pe_v_cmp, w_ck1, w_ck2, w_cv1, w_cv2, w_out, w_ffn_in,
           w_ffn_out):
    depth = w_in.shape[0]
    assert depth == 1, "single trunk layer"
    nb_p, seq, d = x_prompt.shape
    nb_s, n_new, _ = x_sample.shape
    n_pool, page = cache_k_cmp.shape[1], cache_k_cmp.shape[2]
    kvw = N_KV * HEAD_DIM
    n_pages = page_table.shape[1]
    past_len = n_pages * page
    wb_s = state_k_win.shape[2]
    l = 0

    in_cols = w_in.shape[2]
    w_in_pad = jnp.pad(w_in[l], ((0, 0), (0, 3200 - in_cols))).astype(BF16)
    tile_gain = lambda g, heads: jnp.tile(g, heads).reshape(1, heads * HEAD_DIM)
    gq_t, gkc_t = tile_gain(g_q[l], N_HEADS), tile_gain(g_k_cmp[l], N_KV)
    gks_t, gkw_t = tile_gain(g_k_slc[l], N_KV), tile_gain(g_k_win[l], N_KV)
    gsgu = g_sgu[l].reshape(1, -1)
    gmix = g_mix_norm[l].reshape(1, d)
    gffn = g_ffn_norm[l].reshape(1, d)
    gmat = (jnp.kron(jnp.eye(256 // HEAD_DIM, dtype=F32), jnp.ones((HEAD_DIM, HEAD_DIM), F32)) / HEAD_DIM).astype(BF16)
    w_tril = jnp.where(jnp.tril(jnp.ones((CHUNK, CHUNK), bool)), w_sgu[l], 0)
    wmix_p = w_tril.astype(BF16)
    bmix_p = jnp.repeat(b_sgu[l].T, HEAD_DIM, axis=1)
    eye_s = jnp.eye(CHUNK // n_new, dtype=F32)
    wmix_s = jax.vmap(lambda w: jnp.kron(eye_s, w[:n_new, :n_new]))(w_tril).astype(BF16)
    bmix_s = jnp.tile(jnp.repeat(b_sgu[l].T[:n_new], HEAD_DIM, axis=1), (CHUNK // n_new, 1))
    wout_b = w_out[l].astype(BF16)
    win_b = w_ffn_in[l].astype(BF16)
    wo2_b = w_ffn_out[l].astype(BF16)

    n_c = nb_p + nb_s
    n_c_pad = -(-n_c // 8) * 8
    c_all = jnp.pad(jnp.concatenate([c_prompt, c_sample], axis=0), ((0, n_c_pad - n_c), (0, 0)))
    ada = _ada(c_all, w_ada[l].astype(BF16), b_ada[l].reshape(1, -1))
    mods_p = [m.reshape(nb_p, 1, d) for m in jnp.split(ada[:nb_p], 6, axis=-1)]
    mods_s = [jnp.repeat(m, n_new, axis=0) for m in jnp.split(ada[nb_p:n_c], 6, axis=-1)]

    cos_p, sin_p = _rope_tables(jnp.arange(seq))
    cos_s, sin_s = _rope_tables(past_len + jnp.arange(n_new))
    reps = CHUNK // n_new
    cos_s, sin_s = jnp.tile(cos_s, (reps, 1)), jnp.tile(sin_s, (reps, 1))
    xp = x_prompt.reshape(nb_p * seq, d)
    xs = x_sample.reshape(nb_s * n_new, d)
    gains = (gsgu, gq_t, gkc_t, gks_t, gkw_t)
    blk_of_key = (jnp.arange(seq) // L_SLC) % BIAS_ROWS
    onehot = jnp.pad(jax.nn.one_hot(blk_of_key, LANES - HEAD_DIM, dtype=F32), ((0, 0), (HEAD_DIM, 0)))
    q_scale = HEAD_DIM ** -0.5 * math.log2(math.e)
    (a_p, vn_p, qt, gt, kct, vct, kst, vst, kwt, vwt, ks_aug, vs_aug, kw_aug, vw_aug) = _proj_cols(
        xp, nb_p, seq, mods_p[0], mods_p[1], gmix, w_in_pad, cos_p, sin_p, onehot, *gains, wmix_p, bmix_p, gmat,
        PROMPT_ROWS, q_scale)
    (a_s, vn_s, q_s, kc_s, vc_s, ks_s, vs_s, kw_s, vw_s, gate_s) = _proj_rows(
        xs, mods_s[0], mods_s[1], gmix, w_in_pad, cos_s, sin_s, *gains, wmix_s, bmix_s, gmat, CHUNK)

    n_chunk_p = seq // STRIDE
    n_blk_p = seq // L_SLC
    wk_p = _compress_weights(pe_k_cmp[l], w_ck1[l], w_ck2[l], LANES)
    wv_p = _compress_weights(pe_v_cmp[l], w_cv1[l], w_cv2[l], LANES)
    pages_p = [pl.BlockSpec((None, kvw, LANES), functools.partial(lambda pg, i: (i, 0, pg), pg))
               for pg in range(seq // LANES)]
    kcc_p = _compress_call(kct, pages_p, LANES, (nb_p,), wk_p)
    vcc_p = _compress_call(vct, pages_p, LANES, (nb_p,), wv_p, transposed_out=True)
    ovt = _overlap(n_chunk_p, n_blk_p).T
    tri = jnp.tril(jnp.ones((n_blk_p, n_blk_p), BF16))
    q_lane = jnp.arange(2 * CHUNK)[None, :] % CHUNK
    dc = (jnp.arange(n_chunk_p)[:, None] * STRIDE + (L_CMP - 1) - q_lane).astype(jnp.int32)
    dd = (jnp.arange(TK)[:, None] - q_lane).astype(jnp.int32)
    jq = jnp.arange(N_WBLK * CHUNK)[:, None] - q_lane
    wb = jnp.where((jq > 0) & (jq <= WINDOW), 0.0, NEG).astype(F32)
    b_p = _attn_prompt(qt, gt, kcc_p, vcc_p, ks_aug, vs_aug, kw_aug, vw_aug, ovt, tri, dc, dd, wb)
    y_p = _finish(xp, a_p, b_p.reshape(nb_p * seq, N_HEADS * HEAD_DIM), mods_p[2:], False, seq, wout_b, gffn,
                  win_b, wo2_b, PROMPT_ROWS)

    fm = lambda a: a.transpose(0, 2, 3, 1).reshape(a.shape[0], kvw, a.shape[1])
    ck, cv, cks, cvs = fm(cache_k_cmp[l]), fm(cache_v_cmp[l]), fm(cache_k_slc[l]), fm(cache_v_slc[l])
    wk_s = _compress_weights(pe_k_cmp[l], w_ck1[l], w_ck2[l], HEAD_DIM)
    wv_s = _compress_weights(pe_v_cmp[l], w_cv1[l], w_cv2[l], HEAD_DIM)
    spb = (seq // LANES) // n_pages
    assert nb_s % spb == 0 and page == LANES

    def page_spec(j):
        return pl.BlockSpec((None, kvw, page), lambda i, pt: (pt[i * spb + j // n_pages, j % n_pages], 0, 0))

    pages_s = [page_spec(j) for j in range(spb * n_pages)]
    n_c_s = past_len // STRIDE
    n_item = 2
    kcc_s = _compress_call(ck, pages_s, page, (nb_s // spb,), wk_s, prefetch=page_table, n_item=n_item)
    vcc_s = _compress_call(cv, pages_s, page, (nb_s // spb,), wv_s, prefetch=page_table, n_item=n_item)
    kcc_s, vcc_s = kcc_s.reshape(nb_s, n_c_s, kvw), vcc_s.reshape(nb_s, n_c_s, kvw)

    q5 = (q_s * (HEAD_DIM ** -0.5)).reshape(nb_s, n_new, N_KV, 2, HEAD_DIM).transpose(0, 3, 2, 1, 4)
    q5 = jnp.pad(q5, ((0, 0), (0, 0), (0, 0), (0, NQ_PAD - n_new), (0, 0)))
    qbd = jnp.einsum("brgqd,gh->brgqhd", q5, jnp.eye(N_KV, dtype=F32))
    qbd = qbd.reshape(nb_s, 2 * N_KV * NQ_PAD, kvw).astype(BF16)
    g5 = gate_s[:, :3 * N_HEADS].reshape(nb_s, n_new, N_KV, 2, 3).transpose(0, 3, 2, 1, 4)
    g5 = jnp.pad(g5, ((0, 0), (0, 0), (0, 0), (0, NQ_PAD - n_new), (0, 5)))
    gsm = g5.reshape(nb_s, 2 * N_KV * NQ_PAD, 8)
    new = lambda a: a.reshape(nb_s, n_new, kvw)
    n_keys = (n_pages + 1) * page
    ov_s = _overlap(n_c_s, LANES).T
    tri_s = jnp.tril(jnp.ones((LANES, LANES), BF16))
    eexp = (jnp.arange(n_keys)[None, :] // L_SLC == jnp.arange(LANES)[:, None]).astype(BF16)
    os_ = _attn_sample(page_table, qbd, gsm, kcc_s, vcc_s, cks, cvs, _tail_page(new(ks_s), page),
                       _tail_page(new(vs_s), page), fm(state_k_win[l]), fm(state_v_win[l]),
                       _tail_page(new(kw_s), page), _tail_page(new(vw_s), page), ov_s, tri_s, eexp, past_len)
    b_s = os_[:, :, :n_new].reshape(nb_s, 2, n_new, N_KV, HEAD_DIM).transpose(0, 2, 3, 1, 4)
    b_s = b_s.reshape(nb_s * n_new, N_HEADS * HEAD_DIM).astype(BF16)

    y_s = _finish(xs, a_s, b_s, mods_s[2:], True, 0, wout_b, gffn, win_b, wo2_b, CHUNK)

    wb_p = min(WINDOW, seq)
    assert seq - ((seq - 1) // CHUNK) * CHUNK == CHUNK, "the prompt ends on a full chunk"
    from_cols = lambda t: t.reshape(1, nb_p, N_KV, HEAD_DIM, seq).transpose(0, 1, 4, 2, 3)
    outs_p = [from_cols(kct), from_cols(vct), from_cols(kst), from_cols(vst),
              from_cols(kwt)[:, :, seq - wb_p:], from_cols(vwt)[:, :, seq - wb_p:], vn_p[None]]
    kv5 = lambda a, nb, t: a.reshape(1, nb, t, N_KV, HEAD_DIM)
    kw_all = jnp.concatenate([state_k_win[l], kv5(kw_s, nb_s, n_new)[0]], axis=1)
    vw_all = jnp.concatenate([state_v_win[l], kv5(vw_s, nb_s, n_new)[0]], axis=1)
    outs_s = [kv5(kc_s, nb_s, n_new), kv5(vc_s, nb_s, n_new), kv5(ks_s, nb_s, n_new), kv5(vs_s, nb_s, n_new),
              kw_all[None, :, n_new:], vw_all[None, :, n_new:], vn_s.reshape(1, nb_s, n_new, -1)]
    return (y_p.reshape(nb_p, seq, d), y_s.reshape(nb_s, n_new, d), *outs_p, *outs_s)
```

```python
import functools
import math

import jax
import jax.numpy as jnp
from jax import lax
from jax.experimental import pallas as pl
from jax.experimental.pallas import tpu as pltpu

F32 = jnp.float32
BF16 = jnp.bfloat16

CHUNK = 128
A_GROUPS = 8
HEAD_DIM = 64
N_HEADS = 8
N_KV = 4
L_CMP = 32
STRIDE = 16
CMP_HID = 256
L_SLC = 64
N_SEL = 16
WINDOW = 512
ROPE_THETA = 10000.0
EPS = 1e-6
NEG = -1e30
SEL_BIAS = -(2.0 ** 100)

LANES = 128
V_ROWS = HEAD_DIM + 16
VMEM_LIMIT = 52 * 1024 * 1024
PROMPT_ROWS = 512

_NT = (((1,), (1,)), ((), ()))


def _dot(a, b):
    return jnp.dot(a, b, preferred_element_type=F32)


def _dot_nt(a, b):
    return lax.dot_general(a, b, _NT, preferred_element_type=F32)


def _split_dot_left(coef, x):
    hi = x.astype(BF16)
    lo = (x - hi.astype(F32)).astype(BF16)
    return _dot(coef, hi) + _dot(coef, lo)


def _split_dot_right(x, coef):
    hi = x.astype(BF16)
    lo = (x - hi.astype(F32)).astype(BF16)
    return _dot(hi, coef) + _dot(lo, coef)


def _params(sem, flags=None):
    return pltpu.CompilerParams(dimension_semantics=sem, vmem_limit_bytes=VMEM_LIMIT, flags=flags)


def _staged(stages, n):
    depth = len(stages)
    for step in range(n + depth - 1):
        for k, stage in enumerate(stages):
            if 0 <= step - k < n:
                stage(step - k)


def _with_ones_row(vt):
    n = vt.shape[1]
    row = lax.broadcasted_iota(jnp.int32, (V_ROWS - HEAD_DIM, n), 0)
    return jnp.concatenate([vt, jnp.where(row == 0, 1.0, 0.0).astype(vt.dtype)], axis=0)


def _ada_kernel(c_ref, w_ref, b_ref, o_ref):
    c = c_ref[...]
    o_ref[...] = _dot(jax.nn.silu(c).astype(BF16), w_ref[...]) + b_ref[...]


def _ada(c, w, b):
    m, k = c.shape
    n = w.shape[1]
    tn = 1024
    return pl.pallas_call(
        _ada_kernel,
        grid=(n // tn,),
        in_specs=[pl.BlockSpec((m, k), lambda j: (0, 0)),
                  pl.BlockSpec((k, tn), lambda j: (0, j)),
                  pl.BlockSpec((1, tn), lambda j: (0, j))],
        out_specs=pl.BlockSpec((m, tn), lambda j: (0, j)),
        out_shape=jax.ShapeDtypeStruct((m, n), F32),
        compiler_params=_params(("parallel",)),
        name="ada",
    )(c, w, b)


def _group_mean_sq(y, g_ref):
    y2 = y * y
    cols = []
    for c in range(y.shape[1] // 256):
        cols.append(_split_dot_right(y2[:, 256 * c:256 * (c + 1)], g_ref[...]))
    return cols[0] if len(cols) == 1 else jnp.concatenate(cols, axis=1)


def _rope(x, cos, sin):
    n = x.shape[1]
    reps = n // LANES
    cos_t = cos if reps == 1 else jnp.concatenate([cos] * reps, axis=1)
    sin_t = sin if reps == 1 else jnp.concatenate([sin] * reps, axis=1)
    lane = lax.broadcasted_iota(jnp.int32, x.shape, 1)
    first_half = (lane & (HEAD_DIM - 1)) < (HEAD_DIM // 2)
    partner = jnp.where(first_half, pltpu.roll(x, n - HEAD_DIM // 2, 1), pltpu.roll(x, HEAD_DIM // 2, 1))
    return x * cos_t + partner * sin_t


def _proj_common(x_ref, sh_ref, sc_ref, gmix_ref, w_ref, cos_ref, sin_ref, gsgu_ref, gq_ref, gkc_ref, gks_ref,
                 gkw_ref, wmix_ref, bmix_ref, gmat_ref, a_ref, vn_ref):
    x = x_ref[...]
    tm = x.shape[0]
    ms = jnp.mean(x * x, axis=-1, keepdims=True)
    h = x * lax.rsqrt(ms + EPS) * gmix_ref[...]
    h = h * (1.0 + sc_ref[...]) + sh_ref[...]
    hb = h.astype(BF16)
    cos = cos_ref[...]
    sin = sin_ref[...]

    def seg(lo, hi):
        return _dot(hb, w_ref[:, lo:hi])

    def head_norm(y, g_ref):
        return y * lax.rsqrt(_group_mean_sq(y, gmat_ref) + EPS) * g_ref[...]

    u = jax.nn.gelu(seg(0, 512))
    v = jax.nn.gelu(seg(512, 1024))
    vn = head_norm(v, gsgu_ref)
    vn_ref[...] = vn[tm - CHUNK:]
    vb = vn.astype(BF16)
    lane = lax.broadcasted_iota(jnp.int32, (CHUNK, LANES), 1)
    low = lane < HEAD_DIM
    for ck in range(tm // CHUNK):
        rows = slice(ck * CHUNK, (ck + 1) * CHUNK)
        for pr in range(A_GROUPS // 2):
            cols = slice(pr * LANES, (pr + 1) * LANES)
            vp = vb[rows, cols]
            mixed = jnp.where(low, _dot(wmix_ref[2 * pr], vp), _dot(wmix_ref[2 * pr + 1], vp))
            mixed = mixed + bmix_ref[:, cols]
            a_ref[rows, cols] = (u[rows, cols] * mixed).astype(a_ref.dtype)

    q = _rope(head_norm(seg(1024, 1536), gq_ref), cos, sin)
    kc = _rope(head_norm(seg(1536, 1792), gkc_ref), cos, sin)
    vc = seg(1792, 2048)
    ks = _rope(head_norm(seg(2048, 2304), gks_ref), cos, sin)
    vs = seg(2304, 2560)
    kw = _rope(head_norm(seg(2560, 2816), gkw_ref), cos, sin)
    vw = seg(2816, 3072)
    gates = jax.nn.sigmoid(seg(3072, 3200))
    return q, kc, vc, ks, vs, kw, vw, gates


def _proj_rows_kernel(*refs):
    ins, (a_ref, vn_ref, q_ref, kc_ref, vc_ref, ks_ref, vs_ref, kw_ref, vw_ref, gate_ref) = refs[:15], refs[15:]
    outs = _proj_common(*ins, a_ref, vn_ref)
    for ref, val in zip((q_ref, kc_ref, vc_ref, ks_ref, vs_ref, kw_ref, vw_ref, gate_ref), outs):
        ref[...] = val


def _proj_cols_kernel(*refs, q_scale):
    ins, oh_ref = refs[:15], refs[15]
    (a_ref, vn_ref, qt_ref, gt_ref, kct_ref, vct_ref, kst_ref, vst_ref, kwt_ref, vwt_ref,
     ksa_ref, vsa_ref, kwa_ref, vwa_ref, stage_ref) = refs[16:]
    q, kc, vc, ks, vs, kw, vw, gates = _proj_common(*ins, a_ref, vn_ref)
    tm = q.shape[0]
    qt_ref[...] = (q * q_scale).T.astype(BF16)
    gt_ref[...] = gates.T[0:gt_ref.shape[0]]
    kct_ref[...] = kc.T
    kst_ref[...] = ks.T
    kwt_ref[...] = kw.T
    stage_ref[0] = vc
    stage_ref[1] = vs
    stage_ref[2] = vw
    vct_ref[...] = stage_ref[0].T
    vst = stage_ref[1].T
    vwt = stage_ref[2].T
    vst_ref[...] = vst
    vwt_ref[...] = vwt
    low = lax.broadcasted_iota(jnp.int32, (tm, LANES), 1) < HEAD_DIM
    onehot = oh_ref[...]
    for g in range(N_KV):
        pair = slice((g // 2) * LANES, (g // 2 + 1) * LANES)
        ks_g, kw_g = ks[:, pair], kw[:, pair]
        if g % 2 == 1:
            ks_g, kw_g = pltpu.roll(ks_g, HEAD_DIM, 1), pltpu.roll(kw_g, HEAD_DIM, 1)
        ksa_ref[g] = jnp.where(low, ks_g, onehot).astype(BF16)
        kwa_ref[g] = jnp.where(low, kw_g, 0.0).astype(BF16)
        rows = slice(g * HEAD_DIM, (g + 1) * HEAD_DIM)
        vsa_ref[g] = _with_ones_row(vst[rows]).astype(BF16)
        vwa_ref[g] = _with_ones_row(vwt[rows]).astype(BF16)


def _proj_in_specs(tm, d, per_row_mod, rows_per_mod, rope_tiles, w_pad, wmix, bmix, gmat):
    if per_row_mod:
        mod_spec = pl.BlockSpec((tm, d), lambda i: (i, 0))
    else:
        tiles_per_mod = rows_per_mod // tm
        mod_spec = pl.BlockSpec((None, 1, d), lambda i: (i // tiles_per_mod, 0, 0))
    const = lambda shape: pl.BlockSpec(shape, lambda i: (0,) * len(shape))
    rope_spec = pl.BlockSpec((tm, LANES), lambda i: (i % rope_tiles, 0))
    w_spec = pl.BlockSpec(w_pad.shape, lambda i: (0, 0), pipeline_mode=pl.Buffered(1))
    return [pl.BlockSpec((tm, d), lambda i: (i, 0)), mod_spec, mod_spec, const((1, d)), w_spec,
            rope_spec, rope_spec,
            const((1, 512)), const((1, 512)), const((1, 256)), const((1, 256)), const((1, 256)),
            const(wmix.shape), const(bmix.shape), const(gmat.shape)]


def _proj_rows(x, sh, sc, gmix, w_pad, cos, sin, gsgu, gq, gkc, gks, gkw, wmix, bmix, gmat, tm):
    r, d = x.shape
    row = lambda n: pl.BlockSpec((tm, n), lambda i: (i, 0))
    out_widths = [512, 512, 512, 256, 256, 256, 256, 256, 256, 128]
    out_dtypes = [BF16] + [F32] * 9
    return pl.pallas_call(
        _proj_rows_kernel,
        grid=(r // tm,),
        in_specs=_proj_in_specs(tm, d, True, 0, cos.shape[0] // tm, w_pad, wmix, bmix, gmat),
        out_specs=[row(n) for n in out_widths],
        out_shape=[jax.ShapeDtypeStruct((r, n), dt) for n, dt in zip(out_widths, out_dtypes)],
        compiler_params=_params(("parallel",)),
        name="proj_rows",
    )(x, sh, sc, gmix, w_pad, cos, sin, gsgu, gq, gkc, gks, gkw, wmix, bmix, gmat)


def _proj_cols(x, nb, seq, sh, sc, gmix, w_pad, cos, sin, onehot, gsgu, gq, gkc, gks, gkw, wmix, bmix, gmat, tm,
               q_scale):
    r, d = x.shape
    tpb = seq // tm
    kvw = N_KV * HEAD_DIM
    bi = lambda i: i // tpb
    ti = lambda i: i % tpb
    row = lambda n: pl.BlockSpec((tm, n), lambda i: (i, 0))
    colt = lambda n: pl.BlockSpec((None, n, tm), lambda i: (bi(i), 0, ti(i)))
    out_specs = [row(512),
                 pl.BlockSpec((None, CHUNK, 512), lambda i: (bi(i), 0, 0)),
                 colt(512), colt(32),
                 colt(kvw), colt(kvw), colt(kvw), colt(kvw), colt(kvw), colt(kvw),
                 pl.BlockSpec((None, N_KV, tm, LANES), lambda i: (bi(i), 0, ti(i), 0)),
                 pl.BlockSpec((None, N_KV, V_ROWS, tm), lambda i: (bi(i), 0, 0, ti(i))),
                 pl.BlockSpec((None, N_KV, tm, LANES), lambda i: (bi(i), 0, ti(i), 0)),
                 pl.BlockSpec((None, N_KV, V_ROWS, tm), lambda i: (bi(i), 0, 0, ti(i)))]
    sds = jax.ShapeDtypeStruct
    out_shape = [sds((r, 512), BF16), sds((nb, CHUNK, 512), F32), sds((nb, 512, seq), BF16),
                 sds((nb, 32, seq), F32)] + [sds((nb, kvw, seq), F32)] * 6 + [
                 sds((nb, N_KV, seq, LANES), BF16), sds((nb, N_KV, V_ROWS, seq), BF16),
                 sds((nb, N_KV, seq, LANES), BF16), sds((nb, N_KV, V_ROWS, seq), BF16)]
    in_specs = _proj_in_specs(tm, d, False, seq, tpb, w_pad, wmix, bmix, gmat)
    in_specs.append(pl.BlockSpec((tm, LANES), lambda i: (ti(i), 0)))
    return pl.pallas_call(
        functools.partial(_proj_cols_kernel, q_scale=q_scale),
        grid=(r // tm,),
        in_specs=in_specs,
        out_specs=out_specs,
        out_shape=out_shape,
        scratch_shapes=[pltpu.VMEM((3, tm, kvw), F32)],
        compiler_params=_params(("arbitrary",)),
        name="proj_cols",
    )(x, sh, sc, gmix, w_pad, cos, sin, gsgu, gq, gkc, gks, gkw, wmix, bmix, gmat, onehot)


def _compress_kernel(*refs, n_prefetch, n_page, n_item, transposed_out):
    refs = refs[n_prefetch:]
    page_refs = refs[:n_page]
    (w1_ref, w1a_ref, w1b_ref, pe_ref, w2_ref, o_ref,
     xs_ref, lhs_ref, a_ref, b_ref, hid_ref, stage_ref) = refs[n_page:]
    page = page_refs[0].shape[1]
    cpp = page // STRIDE
    ppi = n_page // n_item
    m = ppi * cpp
    rows = N_KV * m
    pitch = xs_ref.shape[2] // STRIDE
    low = lax.broadcasted_iota(jnp.int32, (m, LANES), 1) < HEAD_DIM
    pe_term = _dot(jnp.broadcast_to(pe_ref[...], (8, pe_ref.shape[1])).astype(BF16), w1_ref[...])[0:1]
    b_ref[rows:rows + 8, :] = jnp.zeros((8, CMP_HID), F32)

    def planes(it):
        for pg in range(ppi):
            xt = page_refs[it * ppi + pg][...].T
            for pp in range(N_KV // 2):
                for c in range(cpp):
                    xs_ref[it % 2, pp, pl.ds(pg * cpp + c, STRIDE, stride=pitch), :] = (
                        xt[c * STRIDE:(c + 1) * STRIDE, pp * LANES:(pp + 1) * LANES])

    def relayout(it):
        for rr in range(STRIDE // 2):
            cols = slice(rr * LANES, (rr + 1) * LANES)
            for pp in range(N_KV // 2):
                p0 = xs_ref[it % 2, pp, 2 * rr * pitch:2 * rr * pitch + m, :]
                p1 = xs_ref[it % 2, pp, (2 * rr + 1) * pitch:(2 * rr + 1) * pitch + m, :]
                r0 = pltpu.roll(p0, HEAD_DIM, 1)
                r1 = pltpu.roll(p1, HEAD_DIM, 1)
                g0 = 2 * pp
                lhs_ref[it % 2, g0 * m:(g0 + 1) * m, cols] = jnp.where(low, p0, r1).astype(BF16)
                lhs_ref[it % 2, (g0 + 1) * m:(g0 + 2) * m, cols] = jnp.where(low, r0, p1).astype(BF16)

    def mlp(it):
        a_ref[...] = _dot(lhs_ref[it % 2], w1a_ref[...])
        b_ref[0:rows, :] = _dot(lhs_ref[it % 2], w1b_ref[...])
        hid_ref[...] = jax.nn.gelu(a_ref[...] + b_ref[pl.ds(1, rows), :] + pe_term).astype(BF16)
        out = _dot(hid_ref[0:m, :], w2_ref[0])
        for g in range(1, N_KV):
            out = out + _dot(hid_ref[g * m:(g + 1) * m, :], w2_ref[g])
        if transposed_out:
            stage_ref[...] = out
            out_t = stage_ref[...].T
            for g in range(N_KV):
                o_ref[g] = _with_ones_row(out_t[g * LANES:g * LANES + HEAD_DIM]).astype(o_ref.dtype)
        else:
            o_ref[it * m:(it + 1) * m, :] = out.astype(o_ref.dtype)

    _staged([planes, relayout, mlp], n_item)


def _compress_call(page_array, page_specs, page, grid, weights, prefetch=None, transposed_out=False, n_item=1):
    n_page = len(page_specs)
    assert not (transposed_out and n_item > 1)
    m_all = n_page * page // STRIDE
    m = m_all // n_item
    n_out = weights[-1].shape[2]
    nsp = 0 if prefetch is None else 1
    const = lambda shape: pl.BlockSpec(shape, lambda *a: (0,) * len(shape))
    if transposed_out:
        out_spec = pl.BlockSpec((None, N_KV, V_ROWS, m_all), lambda i, *a: (i, 0, 0, 0))
        out_shape = jax.ShapeDtypeStruct((grid[0], N_KV, V_ROWS, m_all), BF16)
    else:
        out_spec = pl.BlockSpec((None, m_all, n_out), lambda i, *a: (i, 0, 0))
        out_shape = jax.ShapeDtypeStruct((grid[0], m_all, n_out), BF16)
    n_buf = min(n_item, 2)
    grid_spec = pltpu.PrefetchScalarGridSpec(
        num_scalar_prefetch=nsp,
        grid=grid,
        in_specs=list(page_specs) + [const(w.shape) for w in weights],
        out_specs=out_spec,
        scratch_shapes=[pltpu.VMEM((n_buf, N_KV // 2, STRIDE * (m + 8), LANES), F32),
                        pltpu.VMEM((n_buf, N_KV * m, STRIDE * HEAD_DIM), BF16),
                        pltpu.VMEM((N_KV * m, CMP_HID), F32),
                        pltpu.VMEM((N_KV * m + 8, CMP_HID), F32),
                        pltpu.VMEM((N_KV * m, CMP_HID), BF16),
                        pltpu.VMEM((m, n_out), F32)],
    )
    args = ([] if prefetch is None else [prefetch]) + [page_array] * n_page + list(weights)
    return pl.pallas_call(
        functools.partial(_compress_kernel, n_prefetch=nsp, n_page=n_page, n_item=n_item,
                          transposed_out=transposed_out),
        grid_spec=grid_spec,
        out_shape=out_shape,
        compiler_params=_params(("parallel",)),
        name="compress",
    )(*args)


def _compress_weights(pe, w1, w2, head_stride):
    half = STRIDE * HEAD_DIM
    w1b = w1.astype(BF16)
    w2p = jnp.pad(w2, ((0, 0), (0, head_stride - HEAD_DIM)))
    w2blk = jnp.einsum("gh,kd->gkhd", jnp.eye(N_KV, dtype=F32), w2p)
    w2blk = w2blk.reshape(N_KV, CMP_HID, N_KV * head_stride).astype(BF16)
    return w1b, w1b[:half], w1b[half:], pe.reshape(1, L_CMP * HEAD_DIM), w2blk


N_FORCED = 3


def _select_blocks(imp, valid, forced, tri):
    rest = jnp.where(valid & jnp.logical_not(forced), imp, -jnp.inf)
    k = N_SEL - jnp.sum(jnp.where(forced, 1.0, 0.0), axis=0, keepdims=True)

    def body(_, carry):
        thr, cnt = carry
        mx = jnp.max(jnp.where(rest < thr, rest, -jnp.inf), axis=0, keepdims=True)
        c = jnp.sum(jnp.where(rest >= mx, 1.0, 0.0), axis=0, keepdims=True)
        upd = cnt < k
        return jnp.where(upd, mx, thr), jnp.where(upd, c, cnt)

    thr, _ = lax.fori_loop(0, N_SEL - N_FORCED, body, (jnp.full(k.shape, jnp.inf, F32), jnp.zeros(k.shape, F32)))
    above = rest > thr
    tie = (rest == thr) & jnp.logical_not(forced)
    need = k - jnp.sum(jnp.where(above, 1.0, 0.0), axis=0, keepdims=True)
    rank = _dot(tri, jnp.where(tie, 1.0, 0.0).astype(BF16))
    return forced | above | (tie & (rank <= need))


TK = 1024
BIAS_ROWS = TK // L_SLC
N_WBLK = WINDOW // CHUNK + 1


def _attn_prompt_kernel(qt_ref, gt_ref, kcc_ref, vcc_ref, ks_ref, vs_ref, *refs):
    kw_refs, vw_refs = refs[:N_WBLK], refs[N_WBLK:2 * N_WBLK]
    (ovt_ref, tri_ref, dc_ref, dd_ref, wb_ref, o_ref,
     qaug_ref, sel_ref, m_ref, acc_ref, ob_ref, sc_ref, s_ref, p_ref, tmax_ref, alpha_ref, hi_ref,
     lo_ref) = refs[2 * N_WBLK:]
    i = pl.program_id(1)
    qs = i * CHUNK
    nq = CHUNK
    gt = gt_ref[...]

    def gate(g, branch):
        return jnp.concatenate([gt[6 * g + branch:6 * g + branch + 1],
                                gt[6 * g + 3 + branch:6 * g + 3 + branch + 1]], axis=1)

    def compressed_and_selection(nc, ns):
        n_idx = lax.broadcasted_iota(jnp.int32, (ns, nq), 0)
        cur = (qs + lax.broadcasted_iota(jnp.int32, (ns, nq), 1)) // L_SLC
        valid = n_idx <= cur
        forced = valid & ((n_idx == 0) | (n_idx >= cur - 1))

        def cmp_scores(g):
            base = 2 * g * HEAD_DIM
            qaug_ref[g, 0:HEAD_DIM, :] = jnp.concatenate(
                [qt_ref[base:base + HEAD_DIM, :], qt_ref[base + HEAD_DIM:base + 2 * HEAD_DIM, :]], axis=1)
            qaug_ref[g, HEAD_DIM:, :] = jnp.zeros((LANES - HEAD_DIM, 2 * nq), BF16)
            s = jnp.where(dc_ref[0:nc, :] <= qs,
                          _dot(kcc_ref[0:nc, g * LANES:(g + 1) * LANES], qaug_ref[g]), NEG)
            s_ref[g % 2, 0:nc, :] = s
            mx = jnp.max(s, axis=0, keepdims=True)
            tmax_ref[g % 2] = jnp.where(mx > 0.5 * NEG, mx, 0.0)
            m_ref[g] = jnp.full((1, 2 * nq), NEG, F32)
            acc_ref[g] = jnp.zeros((V_ROWS, 2 * nq), F32)

        def cmp_exps(g):
            p = jnp.exp2(s_ref[g % 2, 0:nc, :] - tmax_ref[g % 2])
            inv = 1.0 / jnp.maximum(jnp.sum(p, axis=0, keepdims=True), 1e-20)
            alpha_ref[g % 2] = inv
            p_ref[g % 2, 0:nc, :] = p.astype(BF16)
            psum = p[:, :nq] * inv[:, :nq] + p[:, nq:] * inv[:, nq:]
            hi = psum.astype(BF16)
            hi_ref[g % 2, 0:nc, :] = hi
            lo_ref[g % 2, 0:nc, :] = (psum - hi.astype(F32)).astype(BF16)

        def cmp_values(g):
            oc = _dot(vcc_ref[g, :, 0:nc], p_ref[g % 2, 0:nc, :])
            ob_ref[g] = gate(g, 0) * (oc[0:HEAD_DIM] * alpha_ref[g % 2])
            ov = ovt_ref[0:ns, 0:nc]
            sc_ref[0:ns, g * nq:(g + 1) * nq] = _dot(ov, hi_ref[g % 2, 0:nc, :]) + _dot(ov, lo_ref[g % 2, 0:nc, :])

        _staged([cmp_scores, cmp_exps, cmp_values], N_KV)

        sel = _select_blocks(sc_ref[0:ns, :], jnp.concatenate([valid] * N_KV, axis=1),
                             jnp.concatenate([forced] * N_KV, axis=1), tri_ref[0:ns, 0:ns])
        before = lax.broadcasted_iota(jnp.int32, sel.shape, 0) < 2 * i
        sel_bias = jnp.where(sel & before, 0.0, SEL_BIAS)
        for g in range(N_KV):
            sel_ref[g, 0:ns, :] = sel_bias[:, g * nq:(g + 1) * nq]

    nc_all, ns_all = kcc_ref.shape[0], sel_ref.shape[1]
    n_qblk = pl.num_programs(1)

    @pl.when(i < n_qblk // 2)
    def _():
        compressed_and_selection(nc_all // 2, ns_all // 2)

    @pl.when(i >= n_qblk // 2)
    def _():
        compressed_and_selection(nc_all, ns_all)

    n_tiles = (qs + TK - 1) // TK
    last_tile = jnp.maximum(n_tiles - 1, 0)
    half = TK // 2

    def scores_half(kt, g, buf, h):
        kc = jnp.clip(kt, 0, last_tile)
        k0 = pl.multiple_of(kc * TK, TK)
        if h == 0:
            bias = sel_ref[g, pl.ds(pl.multiple_of(kc * BIAS_ROWS, BIAS_ROWS), BIAS_ROWS), :]
            qaug_ref[g, HEAD_DIM:HEAD_DIM + BIAS_ROWS, :] = jnp.concatenate([bias, bias], axis=1).astype(BF16)
        rows = slice(h * half, (h + 1) * half)
        st = _dot(ks_ref[g, pl.ds(k0 + h * half, half), :], qaug_ref[g])
        s_ref[buf, rows, :] = st
        cmax = jnp.max(st, axis=0, keepdims=True)
        tmax_ref[buf] = cmax if h == 0 else jnp.maximum(tmax_ref[buf], cmax)

    def exps_half(g, buf, h):
        if h == 0:
            m_old = m_ref[g]
            m_new = jnp.maximum(m_old, tmax_ref[buf])
            m_ref[g] = m_new
            alpha_ref[buf] = jnp.exp2(m_old - m_new)
        m_new = m_ref[g]
        for c in range(2):
            rows = slice(h * half + c * (half // 2), h * half + (c + 1) * (half // 2))
            p_ref[buf, rows, :] = jnp.exp2(s_ref[buf, rows, :] - m_new).astype(BF16)

    def values_half(kt, g, buf, h):
        k0 = pl.multiple_of(jnp.clip(kt, 0, last_tile) * TK, TK)
        rows = slice(h * half, (h + 1) * half)
        part = _dot(vs_ref[g, :, pl.ds(k0 + h * half, half)], p_ref[buf, rows, :])
        acc_ref[g] = (alpha_ref[buf] * acc_ref[g] if h == 0 else acc_ref[g]) + part

    def substep(kt, g):
        g_next, kt_next = (g + 1) % N_KV, kt + (g + 1) // N_KV
        g_prev, kt_prev = (g - 1) % N_KV, kt - (1 if g == 0 else 0)
        other = (g + 1) % 2
        for h in range(2):
            scores_half(kt_next, g_next, other, h)
            exps_half(g, g % 2, h)
            values_half(kt_prev, g_prev, other, h)

    def tile(kt, carry):
        for g in range(N_KV):
            substep(kt, g)
        return carry

    p_ref[1] = jnp.zeros(p_ref.shape[1:], BF16)
    alpha_ref[1] = jnp.ones(alpha_ref.shape[1:], F32)
    for h in range(2):
        scores_half(0, 0, 0, h)
    lax.fori_loop(0, n_tiles, tile, 0)
    for h in range(2):
        values_half(n_tiles - 1, N_KV - 1, 1, h)

    own = pl.ds(pl.multiple_of(qs, CHUNK), CHUNK)
    causal = dd_ref[0:CHUNK, :] <= 0
    for g in range(N_KV):
        qaug_ref[g, HEAD_DIM:HEAD_DIM + BIAS_ROWS, :] = jnp.zeros((BIAS_ROWS, 2 * nq), BF16)
        sd = jnp.where(causal, _dot(ks_ref[g, own, :], qaug_ref[g]), NEG)
        m_old = m_ref[g]
        m_new = jnp.maximum(m_old, jnp.max(sd, axis=0, keepdims=True))
        pd = jnp.exp2(sd - m_new).astype(BF16)
        acc = jnp.exp2(m_old - m_new) * acc_ref[g] + _dot(vs_ref[g, :, own], pd)
        ob_ref[g] = ob_ref[g] + gate(g, 1) * (acc[0:HEAD_DIM] * (1.0 / acc[HEAD_DIM:HEAD_DIM + 1]))

    nw = N_WBLK * CHUNK
    outs = [None] * N_KV

    def win_scores(g):
        kwin = jnp.concatenate([r[g] for r in kw_refs], axis=0)
        bias = jnp.concatenate(
            [jnp.where(i - (N_WBLK - 1) + j < 0, NEG, wb_ref[j * CHUNK:(j + 1) * CHUNK, :]) for j in range(N_WBLK)],
            axis=0)
        sw = _dot(kwin, qaug_ref[g]) + bias
        s_ref[g % 2, 0:nw, :] = sw
        tmax_ref[g % 2] = jnp.max(sw, axis=0, keepdims=True)

    def win_exps(g):
        p_ref[g % 2, 0:nw, :] = jnp.exp2(s_ref[g % 2, 0:nw, :] - tmax_ref[g % 2]).astype(BF16)

    def win_values(g):
        vwin = jnp.concatenate([r[g] for r in vw_refs], axis=1)
        ow = _dot(vwin, p_ref[g % 2, 0:nw, :])
        ob = ob_ref[g] + gate(g, 2) * (ow[0:HEAD_DIM] * (1.0 / ow[HEAD_DIM:HEAD_DIM + 1]))
        outs[g] = [ob[:, :nq], ob[:, nq:]]

    _staged([win_scores, win_exps, win_values], N_KV)
    o_ref[...] = jnp.concatenate(sum(outs, []), axis=0).T.astype(o_ref.dtype)


def _attn_prompt(qt, gt, kcc, vcc, ks_aug, vs_aug, kw_aug, vw_aug, ovt, tri, dc, dd, wb):
    b, _, s = qt.shape
    nqb = s // CHUNK
    nc = kcc.shape[1]
    first = WINDOW // CHUNK
    whole = lambda shape: pl.BlockSpec((None,) + shape, lambda bi, i: (bi,) + (0,) * len(shape),
                                       pipeline_mode=pl.Buffered(1))
    const = lambda a: pl.BlockSpec(a.shape, lambda bi, i: (0,) * a.ndim)

    def kw_spec(j):
        return pl.BlockSpec((None, N_KV, CHUNK, LANES), lambda bi, i: (bi, 0, jnp.maximum(i - first + j, 0), 0))

    def vw_spec(j):
        return pl.BlockSpec((None, N_KV, V_ROWS, CHUNK), lambda bi, i: (bi, 0, 0, jnp.maximum(i - first + j, 0)))

    in_specs = ([pl.BlockSpec((None, N_HEADS * HEAD_DIM, CHUNK), lambda bi, i: (bi, 0, i)),
                 pl.BlockSpec((None, 32, CHUNK), lambda bi, i: (bi, 0, i)),
                 pl.BlockSpec((None, nc, N_KV * LANES), lambda bi, i: (bi, 0, 0)),
                 pl.BlockSpec((None, N_KV, V_ROWS, nc), lambda bi, i: (bi, 0, 0, 0)),
                 whole((N_KV, s, LANES)), whole((N_KV, V_ROWS, s))]
                + [kw_spec(j) for j in range(N_WBLK)] + [vw_spec(j) for j in range(N_WBLK)]
                + [const(ovt), const(tri), const(dc), const(dd), const(wb)])
    return pl.pallas_call(
        _attn_prompt_kernel,
        grid=(b, nqb),
        in_specs=in_specs,
        out_specs=pl.BlockSpec((None, CHUNK, N_HEADS * HEAD_DIM), lambda bi, i: (bi, i, 0)),
        out_shape=jax.ShapeDtypeStruct((b, s, N_HEADS * HEAD_DIM), BF16),
        scratch_shapes=[pltpu.VMEM((N_KV, LANES, 2 * CHUNK), BF16),
                        pltpu.VMEM((N_KV, s // L_SLC, CHUNK), F32),
                        pltpu.VMEM((N_KV, 1, 2 * CHUNK), F32),
                        pltpu.VMEM((N_KV, V_ROWS, 2 * CHUNK), F32),
                        pltpu.VMEM((N_KV, HEAD_DIM, 2 * CHUNK), F32),
                        pltpu.VMEM((s // L_SLC, N_KV * CHUNK), F32),
                        pltpu.VMEM((2, TK, 2 * CHUNK), F32),
                        pltpu.VMEM((2, TK, 2 * CHUNK), BF16),
                        pltpu.VMEM((2, 1, 2 * CHUNK), F32),
                        pltpu.VMEM((2, 1, 2 * CHUNK), F32),
                        pltpu.VMEM((2, nc, CHUNK), BF16),
                        pltpu.VMEM((2, nc, CHUNK), BF16)],
        compiler_params=_params(("parallel", "arbitrary")),
        name="attn_prompt",
    )(qt, gt, kcc, vcc, ks_aug, vs_aug, *([kw_aug] * N_WBLK), *([vw_aug] * N_WBLK), ovt, tri, dc, dd, wb)


NQ_PAD = 8


def _softmax_rows(s, mask):
    s = jnp.where(mask, s, NEG)
    mx = jnp.max(s, axis=1, keepdims=True)
    p = jnp.where(mask, jnp.exp(s - mx), 0.0)
    return p / jnp.maximum(jnp.sum(p, axis=1, keepdims=True), 1e-20)


SAMPLES_PER_STEP = 2


def _attn_sample_kernel(pt_ref, *refs, n_pages, past_len, spb):
    del pt_ref
    per = 10 + 2 * n_pages
    slots = [refs[j * per:(j + 1) * per] for j in range(spb)]
    ovt_ref, tri_ref, eexp_ref, o_ref, s_scr = refs[spb * per:]
    qbd = [r[0][...] for r in slots]
    g_refs, kcc_refs, vcc_refs = ([r[k] for r in slots] for k in (1, 2, 3))
    kpages = [r[4:4 + n_pages] for r in slots]
    vpages = [r[4 + n_pages:4 + 2 * n_pages] for r in slots]
    kst, vst, kws, vws, kwt, vwt = ([r[4 + 2 * n_pages + k] for r in slots] for k in range(6))
    nrow = qbd[0].shape[0]
    half = nrow // 2
    page = kpages[0][0].shape[1]
    every = range(spb)

    def t_of(shape):
        return past_len + (lax.broadcasted_iota(jnp.int32, shape, 0) & (NQ_PAD - 1))

    nc = kcc_refs[0].shape[0]
    c_idx = lax.broadcasted_iota(jnp.int32, (nrow, nc), 1)
    cmask = c_idx * STRIDE + (L_CMP - 1) <= t_of((nrow, nc))
    p_c = [_softmax_rows(_dot_nt(qbd[j], kcc_refs[j][...]), cmask) for j in every]
    o_c = [_dot(p_c[j].astype(BF16), vcc_refs[j][...]) for j in every]

    psum = jnp.concatenate([p_c[j][:half] + p_c[j][half:] for j in every]
                           + [jnp.zeros((LANES - spb * half, nc), F32)], axis=0)
    imp = _split_dot_left(ovt_ref[...], psum.T)
    n_idx = lax.broadcasted_iota(jnp.int32, imp.shape, 0)
    cur = (past_len + (lax.broadcasted_iota(jnp.int32, imp.shape, 1) & (NQ_PAD - 1))) // L_SLC
    valid = n_idx <= cur
    forced = valid & ((n_idx == 0) | (n_idx >= cur - 1))
    sel = jnp.where(_select_blocks(imp, valid, forced, tri_ref[...]), 1.0, 0.0).T.astype(BF16)
    sel_keys = [_dot(jnp.concatenate([sel[j * half:(j + 1) * half]] * 2, axis=0), eexp_ref[...]) for j in every]

    for pg in range(n_pages):
        for j in every:
            s_scr[j, :, pg * page:(pg + 1) * page] = _dot(qbd[j], kpages[j][pg][...].astype(BF16))
    for j in every:
        s_scr[j, :, n_pages * page:(n_pages + 1) * page] = _dot(qbd[j], kst[j][...])
    nk = (n_pages + 1) * page
    tok_ok = lax.broadcasted_iota(jnp.int32, (nrow, nk), 1) <= t_of((nrow, nk))
    p_s = [_softmax_rows(s_scr[j], (sel_keys[j] > 0.5) & tok_ok).astype(BF16) for j in every]
    o_s = [_dot_nt(p_s[j][:, n_pages * page:], vst[j][...]) for j in every]
    for pg in range(n_pages):
        for j in every:
            o_s[j] = o_s[j] + _dot_nt(p_s[j][:, pg * page:(pg + 1) * page], vpages[j][pg][...].astype(BF16))

    wb = kws[0].shape[1]
    nw = wb + page
    rel = t_of((nrow, nw)) - (past_len - wb + lax.broadcasted_iota(jnp.int32, (nrow, nw), 1))
    wmask = (rel >= 0) & (rel < WINDOW)
    sw = [jnp.concatenate([_dot(qbd[j], kws[j][...].astype(BF16)), _dot(qbd[j], kwt[j][...])], axis=1) for j in every]
    p_w = [_softmax_rows(sw[j], wmask).astype(BF16) for j in every]
    o_w = [_dot_nt(p_w[j][:, :wb], vws[j][...].astype(BF16)) + _dot_nt(p_w[j][:, wb:], vwt[j][...]) for j in every]

    row_g = (lax.broadcasted_iota(jnp.int32, o_c[0].shape, 0) // NQ_PAD) & (N_KV - 1)
    lane_g = lax.broadcasted_iota(jnp.int32, o_c[0].shape, 1) // HEAD_DIM
    per_r = N_KV * NQ_PAD
    for j in every:
        g = g_refs[j][...]
        o = g[:, 0:1] * o_c[j] + g[:, 1:2] * o_s[j] + g[:, 2:3] * o_w[j]
        o = jnp.where(row_g == lane_g, o, 0.0)
        for r in range(2):
            acc = o[r * per_r:r * per_r + NQ_PAD]
            for gg in range(1, N_KV):
                acc = acc + o[r * per_r + gg * NQ_PAD:r * per_r + (gg + 1) * NQ_PAD]
            o_ref[j, r] = acc


def _attn_sample(page_table, qbd, gsm, kcc, vcc, cache_k, cache_v, k_tail, v_tail, kw_state, vw_state,
                 kw_tail, vw_tail, ovt, tri, eexp, past_len):
    nb, n_pages = page_table.shape
    spb = SAMPLES_PER_STEP
    kvw = cache_k.shape[1]
    page = cache_k.shape[2]
    nrow = qbd.shape[1]
    const = lambda shape: pl.BlockSpec(shape, lambda b, pt: (0,) * len(shape))
    in_specs, args = [], []
    for j in range(spb):
        per_b = lambda shape, j=j: pl.BlockSpec((None,) + shape, lambda b, pt: (b * spb + j,) + (0,) * len(shape))
        page_spec = lambda pg, j=j: pl.BlockSpec((None, kvw, page), lambda b, pt: (pt[b * spb + j, pg], 0, 0))
        in_specs += ([per_b((nrow, kvw)), per_b((nrow, 8)), per_b(kcc.shape[1:]), per_b(vcc.shape[1:])]
                     + [page_spec(pg) for pg in range(n_pages)] + [page_spec(pg) for pg in range(n_pages)]
                     + [per_b((kvw, page)), per_b((kvw, page)), per_b(kw_state.shape[1:]),
                        per_b(vw_state.shape[1:]), per_b((kvw, page)), per_b((kvw, page))])
        args += ([qbd, gsm, kcc, vcc] + [cache_k] * n_pages + [cache_v] * n_pages
                 + [k_tail, v_tail, kw_state, vw_state, kw_tail, vw_tail])
    in_specs += [const(ovt.shape), const(tri.shape), const(eexp.shape)]
    grid_spec = pltpu.PrefetchScalarGridSpec(
        num_scalar_prefetch=1,
        grid=(nb // spb,),
        in_specs=in_specs,
        out_specs=pl.BlockSpec((None, spb, 2, NQ_PAD, kvw), lambda b, pt: (b, 0, 0, 0, 0)),
        scratch_shapes=[pltpu.VMEM((spb, nrow, (n_pages + 1) * page), F32)],
    )
    out = pl.pallas_call(
        functools.partial(_attn_sample_kernel, n_pages=n_pages, past_len=past_len, spb=spb),
        grid_spec=grid_spec,
        out_shape=jax.ShapeDtypeStruct((nb // spb, spb, 2, NQ_PAD, kvw), F32),
        compiler_params=_params(("parallel",)),
        name="attn_sample",
    )(page_table, *args, ovt, tri, eexp)
    return out.reshape(nb, 2, NQ_PAD, kvw)


FF_SPLIT = 2


def _finish_kernel(x_ref, a_ref, b_ref, gt1_ref, sh2_ref, sc2_ref, gt2_ref, wout_ref, gffn_ref, win_ref, wo2_ref,
                   o_ref):
    half = a_ref.shape[1]
    y = _dot(a_ref[...], wout_ref[0:half, :]) + _dot(b_ref[...], wout_ref[half:, :])
    x1 = x_ref[...] + gt1_ref[...] * y
    ms = jnp.mean(x1 * x1, axis=-1, keepdims=True)
    h = x1 * lax.rsqrt(ms + EPS) * gffn_ref[...]
    hb = (h * (1.0 + sc2_ref[...]) + sh2_ref[...]).astype(BF16)
    d_ff = wo2_ref.shape[0]
    step = d_ff // FF_SPLIT
    acc = None
    for c in range(FF_SPLIT):
        up = _dot(hb, win_ref[:, c * step:(c + 1) * step])
        gate = _dot(hb, win_ref[:, d_ff + c * step:d_ff + (c + 1) * step])
        z = (jax.nn.silu(up) * gate).astype(BF16)
        part = _dot(z, wo2_ref[c * step:(c + 1) * step, :])
        acc = part if acc is None else acc + part
    o_ref[...] = x1 + gt2_ref[...] * acc


def _finish(x, a, b, mods, per_row_mod, rows_per_mod, wout, gffn, win, wo2, tm):
    r, d = x.shape
    if per_row_mod:
        mod_spec = pl.BlockSpec((tm, d), lambda i: (i, 0))
    else:
        tiles_per_mod = rows_per_mod // tm
        mod_spec = pl.BlockSpec((None, 1, d), lambda i: (i // tiles_per_mod, 0, 0))
    single = lambda shape: pl.BlockSpec(shape, lambda i: (0,) * len(shape), pipeline_mode=pl.Buffered(1))
    row = lambda n: pl.BlockSpec((tm, n), lambda i: (i, 0))
    return pl.pallas_call(
        _finish_kernel,
        grid=(r // tm,),
        in_specs=[row(d), row(a.shape[1]), row(b.shape[1]), mod_spec, mod_spec, mod_spec, mod_spec,
                  single(wout.shape), single(gffn.shape), single(win.shape), single(wo2.shape)],
        out_specs=row(d),
        out_shape=jax.ShapeDtypeStruct((r, d), F32),
        compiler_params=_params(("parallel",)),
        name="finish",
    )(x, a, b, *mods, wout, gffn, win, wo2)


def _rope_tables(pos):
    half = HEAD_DIM // 2
    inv = ROPE_THETA ** (-jnp.arange(half, dtype=F32) / half)
    ang = pos.astype(F32)[:, None] * inv[None, :]
    cos, sin = jnp.cos(ang), jnp.sin(ang)
    return jnp.concatenate([cos] * 4, axis=1), jnp.concatenate([-sin, sin, -sin, sin], axis=1)


def _overlap(n_c, n_s):
    c_start = jnp.arange(n_c) * STRIDE
    blk = jnp.arange(n_s)
    return ((c_start[:, None] < (blk[None, :] + 1) * L_SLC)
            & (c_start[:, None] + L_CMP > blk[None, :] * L_SLC)).astype(BF16)


def _tail_page(new_rows, page):
    return jnp.pad(new_rows.transpose(0, 2, 1), ((0, 0), (0, 0), (0, page - new_rows.shape[1]))).astype(BF16)


def kernel(x_prompt, x_sample, cache_k_cmp, cache_v_cmp, cache_k_slc, cache_v_slc, state_k_win, state_v_win,
           page_table, c_prompt, c_sample, w_ada, b_ada, g_mix_norm, g_ffn_norm, w_in, g_sgu, w_sgu, b_sgu,
           g_q, g_k_cmp, g_k_slc, g_k_win, pe_k_cmp, pe_v_cmp, w_ck1, w_ck2, w_cv1, w_cv2, w_out, w_ffn_in,
           w_ffn_out):
    depth = w_in.shape[0]
    assert depth == 1, "single trunk layer"
    nb_p, seq, d = x_prompt.shape
    nb_s, n_new, _ = x_sample.shape
    n_pool, page = cache_k_cmp.shape[1], cache_k_cmp.shape[2]
    kvw = N_KV * HEAD_DIM
    n_pages = page_table.shape[1]
    past_len = n_pages * page
    wb_s = state_k_win.shape[2]
    l = 0

    in_cols = w_in.shape[2]
    w_in_pad = jnp.pad(w_in[l], ((0, 0), (0, 3200 - in_cols))).astype(BF16)
    tile_gain = lambda g, heads: jnp.tile(g, heads).reshape(1, heads * HEAD_DIM)
    gq_t, gkc_t = tile_gain(g_q[l], N_HEADS), tile_gain(g_k_cmp[l], N_KV)
    gks_t, gkw_t = tile_gain(g_k_slc[l], N_KV), tile_gain(g_k_win[l], N_KV)
    gsgu = g_sgu[l].reshape(1, -1)
    gmix = g_mix_norm[l].reshape(1, d)
    gffn = g_ffn_norm[l].reshape(1, d)
    gmat = (jnp.kron(jnp.eye(256 // HEAD_DIM, dtype=F32), jnp.ones((HEAD_DIM, HEAD_DIM), F32)) / HEAD_DIM).astype(BF16)
    w_tril = jnp.where(jnp.tril(jnp.ones((CHUNK, CHUNK), bool)), w_sgu[l], 0)
    wmix_p = w_tril.astype(BF16)
    bmix_p = jnp.repeat(b_sgu[l].T, HEAD_DIM, axis=1)
    eye_s = jnp.eye(CHUNK // n_new, dtype=F32)
    wmix_s = jax.vmap(lambda w: jnp.kron(eye_s, w[:n_new, :n_new]))(w_tril).astype(BF16)
    bmix_s = jnp.tile(jnp.repeat(b_sgu[l].T[:n_new], HEAD_DIM, axis=1), (CHUNK // n_new, 1))
    wout_b = w_out[l].astype(BF16)
    win_b = w_ffn_in[l].astype(BF16)
    wo2_b = w_ffn_out[l].astype(BF16)

    n_c = nb_p + nb_s
    n_c_pad = -(-n_c // 8) * 8
    c_all = jnp.pad(jnp.concatenate([c_prompt, c_sample], axis=0), ((0, n_c_pad - n_c), (0, 0)))
    ada = _ada(c_all, w_ada[l].astype(BF16), b_ada[l].reshape(1, -1))
    mods_p = [m.reshape(nb_p, 1, d) for m in jnp.split(ada[:nb_p], 6, axis=-1)]
    mods_s = [jnp.repeat(m, n_new, axis=0) for m in jnp.split(ada[nb_p:n_c], 6, axis=-1)]

    cos_p, sin_p = _rope_tables(jnp.arange(seq))
    cos_s, sin_s = _rope_tables(past_len + jnp.arange(n_new))
    reps = CHUNK // n_new
    cos_s, sin_s = jnp.tile(cos_s, (reps, 1)), jnp.tile(sin_s, (reps, 1))
    xp = x_prompt.reshape(nb_p * seq, d)
    xs = x_sample.reshape(nb_s * n_new, d)
    gains = (gsgu, gq_t, gkc_t, gks_t, gkw_t)
    blk_of_key = (jnp.arange(seq) // L_SLC) % BIAS_ROWS
    onehot = jnp.pad(jax.nn.one_hot(blk_of_key, LANES - HEAD_DIM, dtype=F32), ((0, 0), (HEAD_DIM, 0)))
    q_scale = HEAD_DIM ** -0.5 * math.log2(math.e)
    (a_p, vn_p, qt, gt, kct, vct, kst, vst, kwt, vwt, ks_aug, vs_aug, kw_aug, vw_aug) = _proj_cols(
        xp, nb_p, seq, mods_p[0], mods_p[1], gmix, w_in_pad, cos_p, sin_p, onehot, *gains, wmix_p, bmix_p, gmat,
        PROMPT_ROWS, q_scale)
    (a_s, vn_s, q_s, kc_s, vc_s, ks_s, vs_s, kw_s, vw_s, gate_s) = _proj_rows(
        xs, mods_s[0], mods_s[1], gmix, w_in_pad, cos_s, sin_s, *gains, wmix_s, bmix_s, gmat, CHUNK)

    n_chunk_p = seq // STRIDE
    n_blk_p = seq // L_SLC
    wk_p = _compress_weights(pe_k_cmp[l], w_ck1[l], w_ck2[l], LANES)
    wv_p = _compress_weights(pe_v_cmp[l], w_cv1[l], w_cv2[l], LANES)
    pages_p = [pl.BlockSpec((None, kvw, LANES), functools.partial(lambda pg, i: (i, 0, pg), pg))
               for pg in range(seq // LANES)]
    kcc_p = _compress_call(kct, pages_p, LANES, (nb_p,), wk_p)
    vcc_p = _compress_call(vct, pages_p, LANES, (nb_p,), wv_p, transposed_out=True)
    ovt = _overlap(n_chunk_p, n_blk_p).T
    tri = jnp.tril(jnp.ones((n_blk_p, n_blk_p), BF16))
    q_lane = jnp.arange(2 * CHUNK)[None, :] % CHUNK
    dc = (jnp.arange(n_chunk_p)[:, None] * STRIDE + (L_CMP - 1) - q_lane).astype(jnp.int32)
    dd = (jnp.arange(TK)[:, None] - q_lane).astype(jnp.int32)
    jq = jnp.arange(N_WBLK * CHUNK)[:, None] - q_lane
    wb = jnp.where((jq > 0) & (jq <= WINDOW), 0.0, NEG).astype(F32)
    b_p = _attn_prompt(qt, gt, kcc_p, vcc_p, ks_aug, vs_aug, kw_aug, vw_aug, ovt, tri, dc, dd, wb)
    y_p = _finish(xp, a_p, b_p.reshape(nb_p * seq, N_HEADS * HEAD_DIM), mods_p[2:], False, seq, wout_b, gffn,
                  win_b, wo2_b, PROMPT_ROWS)

    fm = lambda a: a.transpose(0, 2, 3, 1).reshape(a.shape[0], kvw, a.shape[1])
    ck, cv, cks, cvs = fm(cache_k_cmp[l]), fm(cache_v_cmp[l]), fm(cache_k_slc[l]), fm(cache_v_slc[l])
    wk_s = _compress_weights(pe_k_cmp[l], w_ck1[l], w_ck2[l], HEAD_DIM)
    wv_s = _compress_weights(pe_v_cmp[l], w_cv1[l], w_cv2[l], HEAD_DIM)
    spb = (seq // LANES) // n_pages
    assert nb_s % spb == 0 and page == LANES

    def page_spec(j):
        return pl.BlockSpec((None, kvw, page), lambda i, pt: (pt[i * spb + j // n_pages, j % n_pages], 0, 0))

    pages_s = [page_spec(j) for j in range(spb * n_pages)]
    n_c_s = past_len // STRIDE
    n_item = 2
    kcc_s = _compress_call(ck, pages_s, page, (nb_s // spb,), wk_s, prefetch=page_table, n_item=n_item)
    vcc_s = _compress_call(cv, pages_s, page, (nb_s // spb,), wv_s, prefetch=page_table, n_item=n_item)
    kcc_s, vcc_s = kcc_s.reshape(nb_s, n_c_s, kvw), vcc_s.reshape(nb_s, n_c_s, kvw)

    q5 = (q_s * (HEAD_DIM ** -0.5)).reshape(nb_s, n_new, N_KV, 2, HEAD_DIM).transpose(0, 3, 2, 1, 4)
    q5 = jnp.pad(q5, ((0, 0), (0, 0), (0, 0), (0, NQ_PAD - n_new), (0, 0)))
    qbd = jnp.einsum("brgqd,gh->brgqhd", q5, jnp.eye(N_KV, dtype=F32))
    qbd = qbd.reshape(nb_s, 2 * N_KV * NQ_PAD, kvw).astype(BF16)
    g5 = gate_s[:, :3 * N_HEADS].reshape(nb_s, n_new, N_KV, 2, 3).transpose(0, 3, 2, 1, 4)
    g5 = jnp.pad(g5, ((0, 0), (0, 0), (0, 0), (0, NQ_PAD - n_new), (0, 5)))
    gsm = g5.reshape(nb_s, 2 * N_KV * NQ_PAD, 8)
    new = lambda a: a.reshape(nb_s, n_new, kvw)
    n_keys = (n_pages + 1) * page
    ov_s = _overlap(n_c_s, LANES).T
    tri_s = jnp.tril(jnp.ones((LANES, LANES), BF16))
    eexp = (jnp.arange(n_keys)[None, :] // L_SLC == jnp.arange(LANES)[:, None]).astype(BF16)
    os_ = _attn_sample(page_table, qbd, gsm, kcc_s, vcc_s, cks, cvs, _tail_page(new(ks_s), page),
                       _tail_page(new(vs_s), page), fm(state_k_win[l]), fm(state_v_win[l]),
                       _tail_page(new(kw_s), page), _tail_page(new(vw_s), page), ov_s, tri_s, eexp, past_len)
    b_s = os_[:, :, :n_new].reshape(nb_s, 2, n_new, N_KV, HEAD_DIM).transpose(0, 2, 3, 1, 4)
    b_s = b_s.reshape(nb_s * n_new, N_HEADS * HEAD_DIM).astype(BF16)

    y_s = _finish(xs, a_s, b_s, mods_s[2:], True, 0, wout_b, gffn, win_b, wo2_b, CHUNK)

    wb_p = min(WINDOW, seq)
    assert seq - ((seq - 1) // CHUNK) * CHUNK == CHUNK, "the prompt ends on a full chunk"
    from_cols = lambda t: t.reshape(1, nb_p, N_KV, HEAD_DIM, seq).transpose(0, 1, 4, 2, 3)
    outs_p = [from_cols(kct), from_cols(vct), from_cols(kst), from_cols(vst),
              from_cols(kwt)[:, :, seq - wb_p:], from_cols(vwt)[:, :, seq - wb_p:], vn_p[None]]
    kv5 = lambda a, nb, t: a.reshape(1, nb, t, N_KV, HEAD_DIM)
    kw_all = jnp.concatenate([state_k_win[l], kv5(kw_s, nb_s, n_new)[0]], axis=1)
    vw_all = jnp.concatenate([state_v_win[l], kv5(vw_s, nb_s, n_new)[0]], axis=1)
    outs_s = [kv5(kc_s, nb_s, n_new), kv5(vc_s, nb_s, n_new), kv5(ks_s, nb_s, n_new), kv5(vs_s, nb_s, n_new),
              kw_all[None, :, n_new:], vw_all[None, :, n_new:], vn_s.reshape(1, nb_s, n_new, -1)]
    return (y_p.reshape(nb_p, seq, d), y_s.reshape(nb_s, n_new, d), *outs_p, *outs_s)
```

```python
import functools
import math

import jax
import jax.numpy as jnp
from jax import lax
from jax.experimental import pallas as pl
from jax.experimental.pallas import tpu as pltpu

F32 = jnp.float32
BF16 = jnp.bfloat16

CHUNK = 128
A_GROUPS = 8
HEAD_DIM = 64
N_HEADS = 8
N_KV = 4
L_CMP = 32
STRIDE = 16
CMP_HID = 256
L_SLC = 64
N_SEL = 16
WINDOW = 512
ROPE_THETA = 10000.0
EPS = 1e-6
NEG = -1e30
SEL_BIAS = -(2.0 ** 100)

LANES = 128
V_ROWS = HEAD_DIM + 16
VMEM_LIMIT = 52 * 1024 * 1024
PROMPT_ROWS = 512

_NT = (((1,), (1,)), ((), ()))


def _dot(a, b):
    return jnp.dot(a, b, preferred_element_type=F32)


def _dot_nt(a, b):
    return lax.dot_general(a, b, _NT, preferred_element_type=F32)


def _split_dot_left(coef, x):
    hi = x.astype(BF16)
    lo = (x - hi.astype(F32)).astype(BF16)
    return _dot(coef, hi) + _dot(coef, lo)


def _split_dot_right(x, coef):
    hi = x.astype(BF16)
    lo = (x - hi.astype(F32)).astype(BF16)
    return _dot(hi, coef) + _dot(lo, coef)


def _params(sem, flags=None):
    return pltpu.CompilerParams(dimension_semantics=sem, vmem_limit_bytes=VMEM_LIMIT, flags=flags)


def _staged(stages, n):
    depth = len(stages)
    for step in range(n + depth - 1):
        for k, stage in enumerate(stages):
            if 0 <= step - k < n:
                stage(step - k)


def _with_ones_row(vt):
    n = vt.shape[1]
    row = lax.broadcasted_iota(jnp.int32, (V_ROWS - HEAD_DIM, n), 0)
    return jnp.concatenate([vt, jnp.where(row == 0, 1.0, 0.0).astype(vt.dtype)], axis=0)


def _ada_kernel(c_ref, w_ref, b_ref, o_ref):
    c = c_ref[...]
    o_ref[...] = _dot(jax.nn.silu(c).astype(BF16), w_ref[...]) + b_ref[...]


def _ada(c, w, b):
    m, k = c.shape
    n = w.shape[1]
    tn = 1024
    return pl.pallas_call(
        _ada_kernel,
        grid=(n // tn,),
        in_specs=[pl.BlockSpec((m, k), lambda j: (0, 0)),
                  pl.BlockSpec((k, tn), lambda j: (0, j)),
                  pl.BlockSpec((1, tn), lambda j: (0, j))],
        out_specs=pl.BlockSpec((m, tn), lambda j: (0, j)),
        out_shape=jax.ShapeDtypeStruct((m, n), F32),
        compiler_params=_params(("parallel",)),
        name="ada",
    )(c, w, b)


def _group_mean_sq(y, g_ref):
    y2 = y * y
    cols = []
    for c in range(y.shape[1] // 256):
        cols.append(_split_dot_right(y2[:, 256 * c:256 * (c + 1)], g_ref[...]))
    return cols[0] if len(cols) == 1 else jnp.concatenate(cols, axis=1)


def _rope(x, cos, sin):
    n = x.shape[1]
    reps = n // LANES
    cos_t = cos if reps == 1 else jnp.concatenate([cos] * reps, axis=1)
    sin_t = sin if reps == 1 else jnp.concatenate([sin] * reps, axis=1)
    lane = lax.broadcasted_iota(jnp.int32, x.shape, 1)
    first_half = (lane & (HEAD_DIM - 1)) < (HEAD_DIM // 2)
    partner = jnp.where(first_half, pltpu.roll(x, n - HEAD_DIM // 2, 1), pltpu.roll(x, HEAD_DIM // 2, 1))
    return x * cos_t + partner * sin_t


def _proj_common(x_ref, sh_ref, sc_ref, gmix_ref, w_ref, cos_ref, sin_ref, gsgu_ref, gq_ref, gkc_ref, gks_ref,
                 gkw_ref, wmix_ref, bmix_ref, gmat_ref, a_ref, vn_ref):
    x = x_ref[...]
    tm = x.shape[0]
    ms = jnp.mean(x * x, axis=-1, keepdims=True)
    h = x * lax.rsqrt(ms + EPS) * gmix_ref[...]
    h = h * (1.0 + sc_ref[...]) + sh_ref[...]
    hb = h.astype(BF16)
    cos = cos_ref[...]
    sin = sin_ref[...]

    def seg(lo, hi):
        return _dot(hb, w_ref[:, lo:hi])

    def head_norm(y, g_ref):
        return y * lax.rsqrt(_group_mean_sq(y, gmat_ref) + EPS) * g_ref[...]

    u = jax.nn.gelu(seg(0, 512))
    v = jax.nn.gelu(seg(512, 1024))
    vn = head_norm(v, gsgu_ref)
    vn_ref[...] = vn[tm - CHUNK:]
    vb = vn.astype(BF16)
    lane = lax.broadcasted_iota(jnp.int32, (CHUNK, LANES), 1)
    low = lane < HEAD_DIM
    for ck in range(tm // CHUNK):
        rows = slice(ck * CHUNK, (ck + 1) * CHUNK)
        for pr in range(A_GROUPS // 2):
            cols = slice(pr * LANES, (pr + 1) * LANES)
            vp = vb[rows, cols]
            mixed = jnp.where(low, _dot(wmix_ref[2 * pr], vp), _dot(wmix_ref[2 * pr + 1], vp))
            mixed = mixed + bmix_ref[:, cols]
            a_ref[rows, cols] = (u[rows, cols] * mixed).astype(a_ref.dtype)

    q = _rope(head_norm(seg(1024, 1536), gq_ref), cos, sin)
    kc = _rope(head_norm(seg(1536, 1792), gkc_ref), cos, sin)
    vc = seg(1792, 2048)
    ks = _rope(head_norm(seg(2048, 2304), gks_ref), cos, sin)
    vs = seg(2304, 2560)
    kw = _rope(head_norm(seg(2560, 2816), gkw_ref), cos, sin)
    vw = seg(2816, 3072)
    gates = jax.nn.sigmoid(seg(3072, 3200))
    return q, kc, vc, ks, vs, kw, vw, gates


def _proj_rows_kernel(*refs):
    ins, (a_ref, vn_ref, q_ref, kc_ref, vc_ref, ks_ref, vs_ref, kw_ref, vw_ref, gate_ref) = refs[:15], refs[15:]
    outs = _proj_common(*ins, a_ref, vn_ref)
    for ref, val in zip((q_ref, kc_ref, vc_ref, ks_ref, vs_ref, kw_ref, vw_ref, gate_ref), outs):
        ref[...] = val


def _proj_cols_kernel(*refs, q_scale):
    ins, oh_ref = refs[:15], refs[15]
    (a_ref, vn_ref, qt_ref, gt_ref, kct_ref, vct_ref, kst_ref, vst_ref, kwt_ref, vwt_ref,
     ksa_ref, vsa_ref, kwa_ref, vwa_ref, stage_ref) = refs[16:]
    q, kc, vc, ks, vs, kw, vw, gates = _proj_common(*ins, a_ref, vn_ref)
    tm = q.shape[0]
    qt_ref[...] = (q * q_scale).T.astype(BF16)
    gt_ref[...] = gates.T[0:gt_ref.shape[0]]
    kct_ref[...] = kc.T
    kst_ref[...] = ks.T
    kwt_ref[...] = kw.T
    stage_ref[0] = vc
    stage_ref[1] = vs
    stage_ref[2] = vw
    vct_ref[...] = stage_ref[0].T
    vst = stage_ref[1].T
    vwt = stage_ref[2].T
    vst_ref[...] = vst
    vwt_ref[...] = vwt
    low = lax.broadcasted_iota(jnp.int32, (tm, LANES), 1) < HEAD_DIM
    onehot = oh_ref[...]
    for g in range(N_KV):
        pair = slice((g // 2) * LANES, (g // 2 + 1) * LANES)
        ks_g, kw_g = ks[:, pair], kw[:, pair]
        if g % 2 == 1:
            ks_g, kw_g = pltpu.roll(ks_g, HEAD_DIM, 1), pltpu.roll(kw_g, HEAD_DIM, 1)
        ksa_ref[g] = jnp.where(low, ks_g, onehot).astype(BF16)
        kwa_ref[g] = jnp.where(low, kw_g, 0.0).astype(BF16)
        rows = slice(g * HEAD_DIM, (g + 1) * HEAD_DIM)
        vsa_ref[g] = _with_ones_row(vst[rows]).astype(BF16)
        vwa_ref[g] = _with_ones_row(vwt[rows]).astype(BF16)


def _proj_in_specs(tm, d, per_row_mod, rows_per_mod, rope_tiles, w_pad, wmix, bmix, gmat):
    if per_row_mod:
        mod_spec = pl.BlockSpec((tm, d), lambda i: (i, 0))
    else:
        tiles_per_mod = rows_per_mod // tm
        mod_spec = pl.BlockSpec((None, 1, d), lambda i: (i // tiles_per_mod, 0, 0))
    const = lambda shape: pl.BlockSpec(shape, lambda i: (0,) * len(shape))
    rope_spec = pl.BlockSpec((tm, LANES), lambda i: (i % rope_tiles, 0))
    w_spec = pl.BlockSpec(w_pad.shape, lambda i: (0, 0), pipeline_mode=pl.Buffered(1))
    return [pl.BlockSpec((tm, d), lambda i: (i, 0)), mod_spec, mod_spec, const((1, d)), w_spec,
            rope_spec, rope_spec,
            const((1, 512)), const((1, 512)), const((1, 256)), const((1, 256)), const((1, 256)),
            const(wmix.shape), const(bmix.shape), const(gmat.shape)]


def _proj_rows(x, sh, sc, gmix, w_pad, cos, sin, gsgu, gq, gkc, gks, gkw, wmix, bmix, gmat, tm):
    r, d = x.shape
    row = lambda n: pl.BlockSpec((tm, n), lambda i: (i, 0))
    out_widths = [512, 512, 512, 256, 256, 256, 256, 256, 256, 128]
    out_dtypes = [BF16] + [F32] * 9
    return pl.pallas_call(
        _proj_rows_kernel,
        grid=(r // tm,),
        in_specs=_proj_in_specs(tm, d, True, 0, cos.shape[0] // tm, w_pad, wmix, bmix, gmat),
        out_specs=[row(n) for n in out_widths],
        out_shape=[jax.ShapeDtypeStruct((r, n), dt) for n, dt in zip(out_widths, out_dtypes)],
        compiler_params=_params(("parallel",)),
        name="proj_rows",
    )(x, sh, sc, gmix, w_pad, cos, sin, gsgu, gq, gkc, gks, gkw, wmix, bmix, gmat)


def _proj_cols(x, nb, seq, sh, sc, gmix, w_pad, cos, sin, onehot, gsgu, gq, gkc, gks, gkw, wmix, bmix, gmat, tm,
               q_scale):
    r, d = x.shape
    tpb = seq // tm
    kvw = N_KV * HEAD_DIM
    bi = lambda i: i // tpb
    ti = lambda i: i % tpb
    row = lambda n: pl.BlockSpec((tm, n), lambda i: (i, 0))
    colt = lambda n: pl.BlockSpec((None, n, tm), lambda i: (bi(i), 0, ti(i)))
    out_specs = [row(512),
                 pl.BlockSpec((None, CHUNK, 512), lambda i: (bi(i), 0, 0)),
                 colt(512), colt(32),
                 colt(kvw), colt(kvw), colt(kvw), colt(kvw), colt(kvw), colt(kvw),
                 pl.BlockSpec((None, N_KV, tm, LANES), lambda i: (bi(i), 0, ti(i), 0)),
                 pl.BlockSpec((None, N_KV, V_ROWS, tm), lambda i: (bi(i), 0, 0, ti(i))),
                 pl.BlockSpec((None, N_KV, tm, LANES), lambda i: (bi(i), 0, ti(i), 0)),
                 pl.BlockSpec((None, N_KV, V_ROWS, tm), lambda i: (bi(i), 0, 0, ti(i)))]
    sds = jax.ShapeDtypeStruct
    out_shape = [sds((r, 512), BF16), sds((nb, CHUNK, 512), F32), sds((nb, 512, seq), BF16),
                 sds((nb, 32, seq), F32)] + [sds((nb, kvw, seq), F32)] * 6 + [
                 sds((nb, N_KV, seq, LANES), BF16), sds((nb, N_KV, V_ROWS, seq), BF16),
                 sds((nb, N_KV, seq, LANES), BF16), sds((nb, N_KV, V_ROWS, seq), BF16)]
    in_specs = _proj_in_specs(tm, d, False, seq, tpb, w_pad, wmix, bmix, gmat)
    in_specs.append(pl.BlockSpec((tm, LANES), lambda i: (ti(i), 0)))
    return pl.pallas_call(
        functools.partial(_proj_cols_kernel, q_scale=q_scale),
        grid=(r // tm,),
        in_specs=in_specs,
        out_specs=out_specs,
        out_shape=out_shape,
        scratch_shapes=[pltpu.VMEM((3, tm, kvw), F32)],
        compiler_params=_params(("arbitrary",)),
        name="proj_cols",
    )(x, sh, sc, gmix, w_pad, cos, sin, gsgu, gq, gkc, gks, gkw, wmix, bmix, gmat, onehot)


def _compress_kernel(*refs, n_prefetch, n_page, n_item, transposed_out):
    refs = refs[n_prefetch:]
    page_refs = refs[:n_page]
    (w1_ref, w1a_ref, w1b_ref, pe_ref, w2_ref, o_ref,
     xs_ref, lhs_ref, a_ref, b_ref, hid_ref, stage_ref) = refs[n_page:]
    page = page_refs[0].shape[1]
    cpp = page // STRIDE
    ppi = n_page // n_item
    m = ppi * cpp
    rows = N_KV * m
    pitch = xs_ref.shape[2] // STRIDE
    low = lax.broadcasted_iota(jnp.int32, (m, LANES), 1) < HEAD_DIM
    pe_term = _dot(jnp.broadcast_to(pe_ref[...], (8, pe_ref.shape[1])).astype(BF16), w1_ref[...])[0:1]
    b_ref[rows:rows + 8, :] = jnp.zeros((8, CMP_HID), F32)

    def planes(it):
        for pg in range(ppi):
            xt = page_refs[it * ppi + pg][...].T
            for pp in range(N_KV // 2):
                for c in range(cpp):
                    xs_ref[it % 2, pp, pl.ds(pg * cpp + c, STRIDE, stride=pitch), :] = (
                        xt[c * STRIDE:(c + 1) * STRIDE, pp * LANES:(pp + 1) * LANES])

    def relayout(it):
        for rr in range(STRIDE // 2):
            cols = slice(rr * LANES, (rr + 1) * LANES)
            for pp in range(N_KV // 2):
                p0 = xs_ref[it % 2, pp, 2 * rr * pitch:2 * rr * pitch + m, :]
                p1 = xs_ref[it % 2, pp, (2 * rr + 1) * pitch:(2 * rr + 1) * pitch + m, :]
                r0 = pltpu.roll(p0, HEAD_DIM, 1)
                r1 = pltpu.roll(p1, HEAD_DIM, 1)
                g0 = 2 * pp
                lhs_ref[it % 2, g0 * m:(g0 + 1) * m, cols] = jnp.where(low, p0, r1).astype(BF16)
                lhs_ref[it % 2, (g0 + 1) * m:(g0 + 2) * m, cols] = jnp.where(low, r0, p1).astype(BF16)

    def mlp(it):
        a_ref[...] = _dot(lhs_ref[it % 2], w1a_ref[...])
        b_ref[0:rows, :] = _dot(lhs_ref[it % 2], w1b_ref[...])
        hid_ref[...] = jax.nn.gelu(a_ref[...] + b_ref[pl.ds(1, rows), :] + pe_term).astype(BF16)
        out = _dot(hid_ref[0:m, :], w2_ref[0])
        for g in range(1, N_KV):
            out = out + _dot(hid_ref[g * m:(g + 1) * m, :], w2_ref[g])
        if transposed_out:
            stage_ref[...] = out
            out_t = stage_ref[...].T
            for g in range(N_KV):
                o_ref[g] = _with_ones_row(out_t[g * LANES:g * LANES + HEAD_DIM]).astype(o_ref.dtype)
        else:
            o_ref[it * m:(it + 1) * m, :] = out.astype(o_ref.dtype)

    _staged([planes, relayout, mlp], n_item)


def _compress_call(page_array, page_specs, page, grid, weights, prefetch=None, transposed_out=False, n_item=1):
    n_page = len(page_specs)
    assert not (transposed_out and n_item > 1)
    m_all = n_page * page // STRIDE
    m = m_all // n_item
    n_out = weights[-1].shape[2]
    nsp = 0 if prefetch is None else 1
    const = lambda shape: pl.BlockSpec(shape, lambda *a: (0,) * len(shape))
    if transposed_out:
        out_spec = pl.BlockSpec((None, N_KV, V_ROWS, m_all), lambda i, *a: (i, 0, 0, 0))
        out_shape = jax.ShapeDtypeStruct((grid[0], N_KV, V_ROWS, m_all), BF16)
    else:
        out_spec = pl.BlockSpec((None, m_all, n_out), lambda i, *a: (i, 0, 0))
        out_shape = jax.ShapeDtypeStruct((grid[0], m_all, n_out), BF16)
    n_buf = min(n_item, 2)
    grid_spec = pltpu.PrefetchScalarGridSpec(
        num_scalar_prefetch=nsp,
        grid=grid,
        in_specs=list(page_specs) + [const(w.shape) for w in weights],
        out_specs=out_spec,
        scratch_shapes=[pltpu.VMEM((n_buf, N_KV // 2, STRIDE * (m + 8), LANES), F32),
                        pltpu.VMEM((n_buf, N_KV * m, STRIDE * HEAD_DIM), BF16),
                        pltpu.VMEM((N_KV * m, CMP_HID), F32),
                        pltpu.VMEM((N_KV * m + 8, CMP_HID), F32),
                        pltpu.VMEM((N_KV * m, CMP_HID), BF16),
                        pltpu.VMEM((m, n_out), F32)],
    )
    args = ([] if prefetch is None else [prefetch]) + [page_array] * n_page + list(weights)
    return pl.pallas_call(
        functools.partial(_compress_kernel, n_prefetch=nsp, n_page=n_page, n_item=n_item,
                          transposed_out=transposed_out),
        grid_spec=grid_spec,
        out_shape=out_shape,
        compiler_params=_params(("parallel",)),
        name="compress",
    )(*args)


def _compress_weights(pe, w1, w2, head_stride):
    half = STRIDE * HEAD_DIM
    w1b = w1.astype(BF16)
    w2p = jnp.pad(w2, ((0, 0), (0, head_stride - HEAD_DIM)))
    w2blk = jnp.einsum("gh,kd->gkhd", jnp.eye(N_KV, dtype=F32), w2p)
    w2blk = w2blk.reshape(N_KV, CMP_HID, N_KV * head_stride).astype(BF16)
    return w1b, w1b[:half], w1b[half:], pe.reshape(1, L_CMP * HEAD_DIM), w2blk


N_FORCED = 3


def _select_blocks(imp, valid, forced, tri):
    rest = jnp.where(valid & jnp.logical_not(forced), imp, -jnp.inf)
    k = N_SEL - jnp.sum(jnp.where(forced, 1.0, 0.0), axis=0, keepdims=True)

    def body(_, carry):
        thr, cnt = carry
        mx = jnp.max(jnp.where(rest < thr, rest, -jnp.inf), axis=0, keepdims=True)
        c = jnp.sum(jnp.where(rest >= mx, 1.0, 0.0), axis=0, keepdims=True)
        upd = cnt < k
        return jnp.where(upd, mx, thr), jnp.where(upd, c, cnt)

    thr, _ = lax.fori_loop(0, N_SEL - N_FORCED, body, (jnp.full(k.shape, jnp.inf, F32), jnp.zeros(k.shape, F32)))
    above = rest > thr
    tie = (rest == thr) & jnp.logical_not(forced)
    need = k - jnp.sum(jnp.where(above, 1.0, 0.0), axis=0, keepdims=True)
    rank = _dot(tri, jnp.where(tie, 1.0, 0.0).astype(BF16))
    return forced | above | (tie & (rank <= need))


TK = 1024
BIAS_ROWS = TK // L_SLC
N_WBLK = WINDOW // CHUNK + 1
REACH_VARIANTS = 4


def _attn_prompt_kernel(qt_ref, gt_ref, kcc_ref, vcc_ref, ks_ref, vs_ref, *refs):
    kw_refs, vw_refs = refs[:N_WBLK], refs[N_WBLK:2 * N_WBLK]
    (ovt_ref, tri_ref, dc_ref, dd_ref, wb_ref, o_ref,
     qaug_ref, sel_ref, m_ref, acc_ref, ob_ref, sc_ref, s_ref, p_ref, tmax_ref, alpha_ref, hi_ref,
     lo_ref) = refs[2 * N_WBLK:]
    i = pl.program_id(1)
    qs = i * CHUNK
    nq = CHUNK
    gt = gt_ref[...]

    def gate(g, branch):
        return jnp.concatenate([gt[6 * g + branch:6 * g + branch + 1],
                                gt[6 * g + 3 + branch:6 * g + 3 + branch + 1]], axis=1)

    def compressed_and_selection(nc, ns):
        n_idx = lax.broadcasted_iota(jnp.int32, (ns, nq), 0)
        cur = (qs + lax.broadcasted_iota(jnp.int32, (ns, nq), 1)) // L_SLC
        valid = n_idx <= cur
        forced = valid & ((n_idx == 0) | (n_idx >= cur - 1))

        def cmp_scores(g):
            base = 2 * g * HEAD_DIM
            qaug_ref[g, 0:HEAD_DIM, :] = jnp.concatenate(
                [qt_ref[base:base + HEAD_DIM, :], qt_ref[base + HEAD_DIM:base + 2 * HEAD_DIM, :]], axis=1)
            qaug_ref[g, HEAD_DIM:, :] = jnp.zeros((LANES - HEAD_DIM, 2 * nq), BF16)
            s = jnp.where(dc_ref[0:nc, :] <= qs,
                          _dot(kcc_ref[0:nc, g * LANES:(g + 1) * LANES], qaug_ref[g]), NEG)
            s_ref[g % 2, 0:nc, :] = s
            mx = jnp.max(s, axis=0, keepdims=True)
            tmax_ref[g % 2] = jnp.where(mx > 0.5 * NEG, mx, 0.0)
            m_ref[g] = jnp.full((1, 2 * nq), NEG, F32)
            acc_ref[g] = jnp.zeros((V_ROWS, 2 * nq), F32)

        def cmp_exps(g):
            p = jnp.exp2(s_ref[g % 2, 0:nc, :] - tmax_ref[g % 2])
            inv = 1.0 / jnp.maximum(jnp.sum(p, axis=0, keepdims=True), 1e-20)
            alpha_ref[g % 2] = inv
            p_ref[g % 2, 0:nc, :] = p.astype(BF16)
            psum = p[:, :nq] * inv[:, :nq] + p[:, nq:] * inv[:, nq:]
            hi = psum.astype(BF16)
            hi_ref[g % 2, 0:nc, :] = hi
            lo_ref[g % 2, 0:nc, :] = (psum - hi.astype(F32)).astype(BF16)

        def cmp_values(g):
            oc = _dot(vcc_ref[g, :, 0:nc], p_ref[g % 2, 0:nc, :])
            ob_ref[g] = gate(g, 0) * (oc[0:HEAD_DIM] * alpha_ref[g % 2])
            ov = ovt_ref[0:ns, 0:nc]
            sc_ref[0:ns, g * nq:(g + 1) * nq] = _dot(ov, hi_ref[g % 2, 0:nc, :]) + _dot(ov, lo_ref[g % 2, 0:nc, :])

        _staged([cmp_scores, cmp_exps, cmp_values], N_KV)

        sel = _select_blocks(sc_ref[0:ns, :], jnp.concatenate([valid] * N_KV, axis=1),
                             jnp.concatenate([forced] * N_KV, axis=1), tri_ref[0:ns, 0:ns])
        before = lax.broadcasted_iota(jnp.int32, sel.shape, 0) < 2 * i
        sel_bias = jnp.where(sel & before, 0.0, SEL_BIAS)
        for g in range(N_KV):
            sel_ref[g, 0:ns, :] = sel_bias[:, g * nq:(g + 1) * nq]

    nc_all, ns_all = kcc_ref.shape[0], sel_ref.shape[1]
    per_quarter = pl.num_programs(1) // REACH_VARIANTS
    for quarter in range(REACH_VARIANTS):
        reach = quarter + 1

        @pl.when((i >= quarter * per_quarter) & (i < reach * per_quarter))
        def _():
            compressed_and_selection(nc_all * reach // REACH_VARIANTS, ns_all * reach // REACH_VARIANTS)

    n_tiles = (qs + TK - 1) // TK
    last_tile = jnp.maximum(n_tiles - 1, 0)
    half = TK // 2

    def scores_half(kt, g, buf, h):
        kc = jnp.clip(kt, 0, last_tile)
        k0 = pl.multiple_of(kc * TK, TK)
        if h == 0:
            bias = sel_ref[g, pl.ds(pl.multiple_of(kc * BIAS_ROWS, BIAS_ROWS), BIAS_ROWS), :]
            qaug_ref[g, HEAD_DIM:HEAD_DIM + BIAS_ROWS, :] = jnp.concatenate([bias, bias], axis=1).astype(BF16)
        rows = slice(h * half, (h + 1) * half)
        st = _dot(ks_ref[g, pl.ds(k0 + h * half, half), :], qaug_ref[g])
        s_ref[buf, rows, :] = st
        cmax = jnp.max(st, axis=0, keepdims=True)
        tmax_ref[buf] = cmax if h == 0 else jnp.maximum(tmax_ref[buf], cmax)

    def exps_half(g, buf, h):
        if h == 0:
            m_old = m_ref[g]
            m_new = jnp.maximum(m_old, tmax_ref[buf])
            m_ref[g] = m_new
            alpha_ref[buf] = jnp.exp2(m_old - m_new)
        m_new = m_ref[g]
        for c in range(2):
            rows = slice(h * half + c * (half // 2), h * half + (c + 1) * (half // 2))
            p_ref[buf, rows, :] = jnp.exp2(s_ref[buf, rows, :] - m_new).astype(BF16)

    def values_half(kt, g, buf, h):
        k0 = pl.multiple_of(jnp.clip(kt, 0, last_tile) * TK, TK)
        rows = slice(h * half, (h + 1) * half)
        part = _dot(vs_ref[g, :, pl.ds(k0 + h * half, half)], p_ref[buf, rows, :])
        acc_ref[g] = (alpha_ref[buf] * acc_ref[g] if h == 0 else acc_ref[g]) + part

    def substep(kt, g):
        g_next, kt_next = (g + 1) % N_KV, kt + (g + 1) // N_KV
        g_prev, kt_prev = (g - 1) % N_KV, kt - (1 if g == 0 else 0)
        other = (g + 1) % 2
        for h in range(2):
            scores_half(kt_next, g_next, other, h)
            exps_half(g, g % 2, h)
            values_half(kt_prev, g_prev, other, h)

    def tile(kt, carry):
        for g in range(N_KV):
            substep(kt, g)
        return carry

    p_ref[1] = jnp.zeros(p_ref.shape[1:], BF16)
    alpha_ref[1] = jnp.ones(alpha_ref.shape[1:], F32)
    for h in range(2):
        scores_half(0, 0, 0, h)
    lax.fori_loop(0, n_tiles, tile, 0)
    for h in range(2):
        values_half(n_tiles - 1, N_KV - 1, 1, h)

    own = pl.ds(pl.multiple_of(qs, CHUNK), CHUNK)
    causal = dd_ref[0:CHUNK, :] <= 0

    def own_scores(g):
        qaug_ref[g, HEAD_DIM:HEAD_DIM + BIAS_ROWS, :] = jnp.zeros((BIAS_ROWS, 2 * nq), BF16)
        sd = jnp.where(causal, _dot(ks_ref[g, own, :], qaug_ref[g]), NEG)
        s_ref[g % 2, 0:CHUNK, :] = sd
        tmax_ref[g % 2] = jnp.max(sd, axis=0, keepdims=True)

    def own_exps(g):
        m_old = m_ref[g]
        m_new = jnp.maximum(m_old, tmax_ref[g % 2])
        alpha_ref[g % 2] = jnp.exp2(m_old - m_new)
        p_ref[g % 2, 0:CHUNK, :] = jnp.exp2(s_ref[g % 2, 0:CHUNK, :] - m_new).astype(BF16)

    def own_values(g):
        acc = alpha_ref[g % 2] * acc_ref[g] + _dot(vs_ref[g, :, own], p_ref[g % 2, 0:CHUNK, :])
        ob_ref[g] = ob_ref[g] + gate(g, 1) * (acc[0:HEAD_DIM] * (1.0 / acc[HEAD_DIM:HEAD_DIM + 1]))

    _staged([own_scores, own_exps, own_values], N_KV)

    nw = N_WBLK * CHUNK
    outs = [None] * N_KV

    def win_scores(g):
        kwin = jnp.concatenate([r[g] for r in kw_refs], axis=0)
        bias = jnp.concatenate(
            [jnp.where(i - (N_WBLK - 1) + j < 0, NEG, wb_ref[j * CHUNK:(j + 1) * CHUNK, :]) for j in range(N_WBLK)],
            axis=0)
        sw = _dot(kwin, qaug_ref[g]) + bias
        s_ref[g % 2, 0:nw, :] = sw
        tmax_ref[g % 2] = jnp.max(sw, axis=0, keepdims=True)

    def win_exps(g):
        p_ref[g % 2, 0:nw, :] = jnp.exp2(s_ref[g % 2, 0:nw, :] - tmax_ref[g % 2]).astype(BF16)

    def win_values(g):
        vwin = jnp.concatenate([r[g] for r in vw_refs], axis=1)
        ow = _dot(vwin, p_ref[g % 2, 0:nw, :])
        ob = ob_ref[g] + gate(g, 2) * (ow[0:HEAD_DIM] * (1.0 / ow[HEAD_DIM:HEAD_DIM + 1]))
        outs[g] = [ob[:, :nq], ob[:, nq:]]

    _staged([win_scores, win_exps, win_values], N_KV)
    o_ref[...] = jnp.concatenate(sum(outs, []), axis=0).T.astype(o_ref.dtype)


def _attn_prompt(qt, gt, kcc, vcc, ks_aug, vs_aug, kw_aug, vw_aug, ovt, tri, dc, dd, wb):
    b, _, s = qt.shape
    nqb = s // CHUNK
    nc = kcc.shape[1]
    first = WINDOW // CHUNK
    whole = lambda shape: pl.BlockSpec((None,) + shape, lambda bi, i: (bi,) + (0,) * len(shape),
                                       pipeline_mode=pl.Buffered(1))
    const = lambda a: pl.BlockSpec(a.shape, lambda bi, i: (0,) * a.ndim)

    def kw_spec(j):
        return pl.BlockSpec((None, N_KV, CHUNK, LANES), lambda bi, i: (bi, 0, jnp.maximum(i - first + j, 0), 0))

    def vw_spec(j):
        return pl.BlockSpec((None, N_KV, V_ROWS, CHUNK), lambda bi, i: (bi, 0, 0, jnp.maximum(i - first + j, 0)))

    in_specs = ([pl.BlockSpec((None, N_HEADS * HEAD_DIM, CHUNK), lambda bi, i: (bi, 0, i)),
                 pl.BlockSpec((None, 32, CHUNK), lambda bi, i: (bi, 0, i)),
                 pl.BlockSpec((None, nc, N_KV * LANES), lambda bi, i: (bi, 0, 0)),
                 pl.BlockSpec((None, N_KV, V_ROWS, nc), lambda bi, i: (bi, 0, 0, 0)),
                 whole((N_KV, s, LANES)), whole((N_KV, V_ROWS, s))]
                + [kw_spec(j) for j in range(N_WBLK)] + [vw_spec(j) for j in range(N_WBLK)]
                + [const(ovt), const(tri), const(dc), const(dd), const(wb)])
    return pl.pallas_call(
        _attn_prompt_kernel,
        grid=(b, nqb),
        in_specs=in_specs,
        out_specs=pl.BlockSpec((None, CHUNK, N_HEADS * HEAD_DIM), lambda bi, i: (bi, i, 0)),
        out_shape=jax.ShapeDtypeStruct((b, s, N_HEADS * HEAD_DIM), BF16),
        scratch_shapes=[pltpu.VMEM((N_KV, LANES, 2 * CHUNK), BF16),
                        pltpu.VMEM((N_KV, s // L_SLC, CHUNK), F32),
                        pltpu.VMEM((N_KV, 1, 2 * CHUNK), F32),
                        pltpu.VMEM((N_KV, V_ROWS, 2 * CHUNK), F32),
                        pltpu.VMEM((N_KV, HEAD_DIM, 2 * CHUNK), F32),
                        pltpu.VMEM((s // L_SLC, N_KV * CHUNK), F32),
                        pltpu.VMEM((2, TK, 2 * CHUNK), F32),
                        pltpu.VMEM((2, TK, 2 * CHUNK), BF16),
                        pltpu.VMEM((2, 1, 2 * CHUNK), F32),
                        pltpu.VMEM((2, 1, 2 * CHUNK), F32),
                        pltpu.VMEM((2, nc, CHUNK), BF16),
                        pltpu.VMEM((2, nc, CHUNK), BF16)],
        compiler_params=_params(("parallel", "arbitrary")),
        name="attn_prompt",
    )(qt, gt, kcc, vcc, ks_aug, vs_aug, *([kw_aug] * N_WBLK), *([vw_aug] * N_WBLK), ovt, tri, dc, dd, wb)


NQ_PAD = 8


def _softmax_rows(s, mask):
    s = jnp.where(mask, s, NEG)
    mx = jnp.max(s, axis=1, keepdims=True)
    p = jnp.where(mask, jnp.exp(s - mx), 0.0)
    return p / jnp.maximum(jnp.sum(p, axis=1, keepdims=True), 1e-20)


SAMPLES_PER_STEP = 2


def _attn_sample_kernel(pt_ref, *refs, n_pages, past_len, spb):
    del pt_ref
    per = 10 + 2 * n_pages
    slots = [refs[j * per:(j + 1) * per] for j in range(spb)]
    ovt_ref, tri_ref, eexp_ref, o_ref, s_scr = refs[spb * per:]
    qbd = [r[0][...] for r in slots]
    g_refs, kcc_refs, vcc_refs = ([r[k] for r in slots] for k in (1, 2, 3))
    kpages = [r[4:4 + n_pages] for r in slots]
    vpages = [r[4 + n_pages:4 + 2 * n_pages] for r in slots]
    kst, vst, kws, vws, kwt, vwt = ([r[4 + 2 * n_pages + k] for r in slots] for k in range(6))
    nrow = qbd[0].shape[0]
    half = nrow // 2
    page = kpages[0][0].shape[1]
    every = range(spb)

    def t_of(shape):
        return past_len + (lax.broadcasted_iota(jnp.int32, shape, 0) & (NQ_PAD - 1))

    nc = kcc_refs[0].shape[0]
    c_idx = lax.broadcasted_iota(jnp.int32, (nrow, nc), 1)
    cmask = c_idx * STRIDE + (L_CMP - 1) <= t_of((nrow, nc))
    p_c = [_softmax_rows(_dot_nt(qbd[j], kcc_refs[j][...]), cmask) for j in every]
    o_c = [_dot(p_c[j].astype(BF16), vcc_refs[j][...]) for j in every]

    psum = jnp.concatenate([p_c[j][:half] + p_c[j][half:] for j in every]
                           + [jnp.zeros((LANES - spb * half, nc), F32)], axis=0)
    imp = _split_dot_left(ovt_ref[...], psum.T)
    n_idx = lax.broadcasted_iota(jnp.int32, imp.shape, 0)
    cur = (past_len + (lax.broadcasted_iota(jnp.int32, imp.shape, 1) & (NQ_PAD - 1))) // L_SLC
    valid = n_idx <= cur
    forced = valid & ((n_idx == 0) | (n_idx >= cur - 1))
    sel = jnp.where(_select_blocks(imp, valid, forced, tri_ref[...]), 1.0, 0.0).T.astype(BF16)
    sel_keys = [_dot(jnp.concatenate([sel[j * half:(j + 1) * half]] * 2, axis=0), eexp_ref[...]) for j in every]

    for pg in range(n_pages):
        for j in every:
            s_scr[j, :, pg * page:(pg + 1) * page] = _dot(qbd[j], kpages[j][pg][...].astype(BF16))
    for j in every:
        s_scr[j, :, n_pages * page:(n_pages + 1) * page] = _dot(qbd[j], kst[j][...])
    nk = (n_pages + 1) * page
    tok_ok = lax.broadcasted_iota(jnp.int32, (nrow, nk), 1) <= t_of((nrow, nk))
    p_s = [_softmax_rows(s_scr[j], (sel_keys[j] > 0.5) & tok_ok).astype(BF16) for j in every]
    o_s = [_dot_nt(p_s[j][:, n_pages * page:], vst[j][...]) for j in every]
    for pg in range(n_pages):
        for j in every:
            o_s[j] = o_s[j] + _dot_nt(p_s[j][:, pg * page:(pg + 1) * page], vpages[j][pg][...].astype(BF16))

    wb = kws[0].shape[1]
    nw = wb + page
    rel = t_of((nrow, nw)) - (past_len - wb + lax.broadcasted_iota(jnp.int32, (nrow, nw), 1))
    wmask = (rel >= 0) & (rel < WINDOW)
    sw = [jnp.concatenate([_dot(qbd[j], kws[j][...].astype(BF16)), _dot(qbd[j], kwt[j][...])], axis=1) for j in every]
    p_w = [_softmax_rows(sw[j], wmask).astype(BF16) for j in every]
    o_w = [_dot_nt(p_w[j][:, :wb], vws[j][...].astype(BF16)) + _dot_nt(p_w[j][:, wb:], vwt[j][...]) for j in every]

    row_g = (lax.broadcasted_iota(jnp.int32, o_c[0].shape, 0) // NQ_PAD) & (N_KV - 1)
    lane_g = lax.broadcasted_iota(jnp.int32, o_c[0].shape, 1) // HEAD_DIM
    per_r = N_KV * NQ_PAD
    for j in every:
        g = g_refs[j][...]
        o = g[:, 0:1] * o_c[j] + g[:, 1:2] * o_s[j] + g[:, 2:3] * o_w[j]
        o = jnp.where(row_g == lane_g, o, 0.0)
        for r in range(2):
            acc = o[r * per_r:r * per_r + NQ_PAD]
            for gg in range(1, N_KV):
                acc = acc + o[r * per_r + gg * NQ_PAD:r * per_r + (gg + 1) * NQ_PAD]
            o_ref[j, r] = acc


def _attn_sample(page_table, qbd, gsm, kcc, vcc, cache_k, cache_v, k_tail, v_tail, kw_state, vw_state,
                 kw_tail, vw_tail, ovt, tri, eexp, past_len):
    nb, n_pages = page_table.shape
    spb = SAMPLES_PER_STEP
    kvw = cache_k.shape[1]
    page = cache_k.shape[2]
    nrow = qbd.shape[1]
    const = lambda shape: pl.BlockSpec(shape, lambda b, pt: (0,) * len(shape))
    in_specs, args = [], []
    for j in range(spb):
        per_b = lambda shape, j=j: pl.BlockSpec((None,) + shape, lambda b, pt: (b * spb + j,) + (0,) * len(shape))
        page_spec = lambda pg, j=j: pl.BlockSpec((None, kvw, page), lambda b, pt: (pt[b * spb + j, pg], 0, 0))
        in_specs += ([per_b((nrow, kvw)), per_b((nrow, 8)), per_b(kcc.shape[1:]), per_b(vcc.shape[1:])]
                     + [page_spec(pg) for pg in range(n_pages)] + [page_spec(pg) for pg in range(n_pages)]
                     + [per_b((kvw, page)), per_b((kvw, page)), per_b(kw_state.shape[1:]),
                        per_b(vw_state.shape[1:]), per_b((kvw, page)), per_b((kvw, page))])
        args += ([qbd, gsm, kcc, vcc] + [cache_k] * n_pages + [cache_v] * n_pages
                 + [k_tail, v_tail, kw_state, vw_state, kw_tail, vw_tail])
    in_specs += [const(ovt.shape), const(tri.shape), const(eexp.shape)]
    grid_spec = pltpu.PrefetchScalarGridSpec(
        num_scalar_prefetch=1,
        grid=(nb // spb,),
        in_specs=in_specs,
        out_specs=pl.BlockSpec((None, spb, 2, NQ_PAD, kvw), lambda b, pt: (b, 0, 0, 0, 0)),
        scratch_shapes=[pltpu.VMEM((spb, nrow, (n_pages + 1) * page), F32)],
    )
    out = pl.pallas_call(
        functools.partial(_attn_sample_kernel, n_pages=n_pages, past_len=past_len, spb=spb),
        grid_spec=grid_spec,
        out_shape=jax.ShapeDtypeStruct((nb // spb, spb, 2, NQ_PAD, kvw), F32),
        compiler_params=_params(("parallel",)),
        name="attn_sample",
    )(page_table, *args, ovt, tri, eexp)
    return out.reshape(nb, 2, NQ_PAD, kvw)


FF_SPLIT = 2


def _finish_kernel(x_ref, a_ref, b_ref, gt1_ref, sh2_ref, sc2_ref, gt2_ref, wout_ref, gffn_ref, win_ref, wo2_ref,
                   o_ref):
    half = a_ref.shape[1]
    y = _dot(a_ref[...], wout_ref[0:half, :]) + _dot(b_ref[...], wout_ref[half:, :])
    x1 = x_ref[...] + gt1_ref[...] * y
    ms = jnp.mean(x1 * x1, axis=-1, keepdims=True)
    h = x1 * lax.rsqrt(ms + EPS) * gffn_ref[...]
    hb = (h * (1.0 + sc2_ref[...]) + sh2_ref[...]).astype(BF16)
    d_ff = wo2_ref.shape[0]
    step = d_ff // FF_SPLIT
    acc = None
    for c in range(FF_SPLIT):
        up = _dot(hb, win_ref[:, c * step:(c + 1) * step])
        gate = _dot(hb, win_ref[:, d_ff + c * step:d_ff + (c + 1) * step])
        z = (jax.nn.silu(up) * gate).astype(BF16)
        part = _dot(z, wo2_ref[c * step:(c + 1) * step, :])
        acc = part if acc is None else acc + part
    o_ref[...] = x1 + gt2_ref[...] * acc


def _finish(x, a, b, mods, per_row_mod, rows_per_mod, wout, gffn, win, wo2, tm):
    r, d = x.shape
    if per_row_mod:
        mod_spec = pl.BlockSpec((tm, d), lambda i: (i, 0))
    else:
        tiles_per_mod = rows_per_mod // tm
        mod_spec = pl.BlockSpec((None, 1, d), lambda i: (i // tiles_per_mod, 0, 0))
    single = lambda shape: pl.BlockSpec(shape, lambda i: (0,) * len(shape), pipeline_mode=pl.Buffered(1))
    row = lambda n: pl.BlockSpec((tm, n), lambda i: (i, 0))
    return pl.pallas_call(
        _finish_kernel,
        grid=(r // tm,),
        in_specs=[row(d), row(a.shape[1]), row(b.shape[1]), mod_spec, mod_spec, mod_spec, mod_spec,
                  single(wout.shape), single(gffn.shape), single(win.shape), single(wo2.shape)],
        out_specs=row(d),
        out_shape=jax.ShapeDtypeStruct((r, d), F32),
        compiler_params=_params(("parallel",)),
        name="finish",
    )(x, a, b, *mods, wout, gffn, win, wo2)


def _rope_tables(pos):
    half = HEAD_DIM // 2
    inv = ROPE_THETA ** (-jnp.arange(half, dtype=F32) / half)
    ang = pos.astype(F32)[:, None] * inv[None, :]
    cos, sin = jnp.cos(ang), jnp.sin(ang)
    return jnp.concatenate([cos] * 4, axis=1), jnp.concatenate([-sin, sin, -sin, sin], axis=1)


def _overlap(n_c, n_s):
    c_start = jnp.arange(n_c) * STRIDE
    blk = jnp.arange(n_s)
    return ((c_start[:, None] < (blk[None, :] + 1) * L_SLC)
            & (c_start[:, None] + L_CMP > blk[None, :] * L_SLC)).astype(BF16)


def _tail_page(new_rows, page):
    return jnp.pad(new_rows.transpose(0, 2, 1), ((0, 0), (0, 0), (0, page - new_rows.shape[1]))).astype(BF16)


def kernel(x_prompt, x_sample, cache_k_cmp, cache_v_cmp, cache_k_slc, cache_v_slc, state_k_win, state_v_win,
           page_table, c_prompt, c_sample, w_ada, b_ada, g_mix_norm, g_ffn_norm, w_in, g_sgu, w_sgu, b_sgu,
           g_q, g_k_cmp, g_k_slc, g_k_win, pe_k_cmp, pe_v_cmp, w_ck1, w_ck2, w_cv1, w_cv2, w_out, w_ffn_in,
           w_ffn_out):
    depth = w_in.shape[0]
    assert depth == 1, "single trunk layer"
    nb_p, seq, d = x_prompt.shape
    nb_s, n_new, _ = x_sample.shape
    n_pool, page = cache_k_cmp.shape[1], cache_k_cmp.shape[2]
    kvw = N_KV * HEAD_DIM
    n_pages = page_table.shape[1]
    past_len = n_pages * page
    wb_s = state_k_win.shape[2]
    l = 0

    in_cols = w_in.shape[2]
    w_in_pad = jnp.pad(w_in[l], ((0, 0), (0, 3200 - in_cols))).astype(BF16)
    tile_gain = lambda g, heads: jnp.tile(g, heads).reshape(1, heads * HEAD_DIM)
    gq_t, gkc_t = tile_gain(g_q[l], N_HEADS), tile_gain(g_k_cmp[l], N_KV)
    gks_t, gkw_t = tile_gain(g_k_slc[l], N_KV), tile_gain(g_k_win[l], N_KV)
    gsgu = g_sgu[l].reshape(1, -1)
    gmix = g_mix_norm[l].reshape(1, d)
    gffn = g_ffn_norm[l].reshape(1, d)
    gmat = (jnp.kron(jnp.eye(256 // HEAD_DIM, dtype=F32), jnp.ones((HEAD_DIM, HEAD_DIM), F32)) / HEAD_DIM).astype(BF16)
    w_tril = jnp.where(jnp.tril(jnp.ones((CHUNK, CHUNK), bool)), w_sgu[l], 0)
    wmix_p = w_tril.astype(BF16)
    bmix_p = jnp.repeat(b_sgu[l].T, HEAD_DIM, axis=1)
    eye_s = jnp.eye(CHUNK // n_new, dtype=F32)
    wmix_s = jax.vmap(lambda w: jnp.kron(eye_s, w[:n_new, :n_new]))(w_tril).astype(BF16)
    bmix_s = jnp.tile(jnp.repeat(b_sgu[l].T[:n_new], HEAD_DIM, axis=1), (CHUNK // n_new, 1))
    wout_b = w_out[l].astype(BF16)
    win_b = w_ffn_in[l].astype(BF16)
    wo2_b = w_ffn_out[l].astype(BF16)

    n_c = nb_p + nb_s
    n_c_pad = -(-n_c // 8) * 8
    c_all = jnp.pad(jnp.concatenate([c_prompt, c_sample], axis=0), ((0, n_c_pad - n_c), (0, 0)))
    ada = _ada(c_all, w_ada[l].astype(BF16), b_ada[l].reshape(1, -1))
    mods_p = [m.reshape(nb_p, 1, d) for m in jnp.split(ada[:nb_p], 6, axis=-1)]
    mods_s = [jnp.repeat(m, n_new, axis=0) for m in jnp.split(ada[nb_p:n_c], 6, axis=-1)]

    cos_p, sin_p = _rope_tables(jnp.arange(seq))
    cos_s, sin_s = _rope_tables(past_len + jnp.arange(n_new))
    reps = CHUNK // n_new
    cos_s, sin_s = jnp.tile(cos_s, (reps, 1)), jnp.tile(sin_s, (reps, 1))
    xp = x_prompt.reshape(nb_p * seq, d)
    xs = x_sample.reshape(nb_s * n_new, d)
    gains = (gsgu, gq_t, gkc_t, gks_t, gkw_t)
    blk_of_key = (jnp.arange(seq) // L_SLC) % BIAS_ROWS
    onehot = jnp.pad(jax.nn.one_hot(blk_of_key, LANES - HEAD_DIM, dtype=F32), ((0, 0), (HEAD_DIM, 0)))
    q_scale = HEAD_DIM ** -0.5 * math.log2(math.e)
    (a_p, vn_p, qt, gt, kct, vct, kst, vst, kwt, vwt, ks_aug, vs_aug, kw_aug, vw_aug) = _proj_cols(
        xp, nb_p, seq, mods_p[0], mods_p[1], gmix, w_in_pad, cos_p, sin_p, onehot, *gains, wmix_p, bmix_p, gmat,
        PROMPT_ROWS, q_scale)
    (a_s, vn_s, q_s, kc_s, vc_s, ks_s, vs_s, kw_s, vw_s, gate_s) = _proj_rows(
        xs, mods_s[0], mods_s[1], gmix, w_in_pad, cos_s, sin_s, *gains, wmix_s, bmix_s, gmat, CHUNK)

    n_chunk_p = seq // STRIDE
    n_blk_p = seq // L_SLC
    wk_p = _compress_weights(pe_k_cmp[l], w_ck1[l], w_ck2[l], LANES)
    wv_p = _compress_weights(pe_v_cmp[l], w_cv1[l], w_cv2[l], LANES)
    pages_p = [pl.BlockSpec((None, kvw, LANES), functools.partial(lambda pg, i: (i, 0, pg), pg))
               for pg in range(seq // LANES)]
    kcc_p = _compress_call(kct, pages_p, LANES, (nb_p,), wk_p)
    vcc_p = _compress_call(vct, pages_p, LANES, (nb_p,), wv_p, transposed_out=True)
    ovt = _overlap(n_chunk_p, n_blk_p).T
    tri = jnp.tril(jnp.ones((n_blk_p, n_blk_p), BF16))
    q_lane = jnp.arange(2 * CHUNK)[None, :] % CHUNK
    dc = (jnp.arange(n_chunk_p)[:, None] * STRIDE + (L_CMP - 1) - q_lane).astype(jnp.int32)
    dd = (jnp.arange(TK)[:, None] - q_lane).astype(jnp.int32)
    jq = jnp.arange(N_WBLK * CHUNK)[:, None] - q_lane
    wb = jnp.where((jq > 0) & (jq <= WINDOW), 0.0, NEG).astype(F32)
    b_p = _attn_prompt(qt, gt, kcc_p, vcc_p, ks_aug, vs_aug, kw_aug, vw_aug, ovt, tri, dc, dd, wb)
    y_p = _finish(xp, a_p, b_p.reshape(nb_p * seq, N_HEADS * HEAD_DIM), mods_p[2:], False, seq, wout_b, gffn,
                  win_b, wo2_b, PROMPT_ROWS)

    fm = lambda a: a.transpose(0, 2, 3, 1).reshape(a.shape[0], kvw, a.shape[1])
    ck, cv, cks, cvs = fm(cache_k_cmp[l]), fm(cache_v_cmp[l]), fm(cache_k_slc[l]), fm(cache_v_slc[l])
    wk_s = _compress_weights(pe_k_cmp[l], w_ck1[l], w_ck2[l], HEAD_DIM)
    wv_s = _compress_weights(pe_v_cmp[l], w_cv1[l], w_cv2[l], HEAD_DIM)
    spb = (seq // LANES) // n_pages
    assert nb_s % spb == 0 and page == LANES

    def page_spec(j):
        return pl.BlockSpec((None, kvw, page), lambda i, pt: (pt[i * spb + j // n_pages, j % n_pages], 0, 0))

    pages_s = [page_spec(j) for j in range(spb * n_pages)]
    n_c_s = past_len // STRIDE
    n_item = 2
    kcc_s = _compress_call(ck, pages_s, page, (nb_s // spb,), wk_s, prefetch=page_table, n_item=n_item)
    vcc_s = _compress_call(cv, pages_s, page, (nb_s // spb,), wv_s, prefetch=page_table, n_item=n_item)
    kcc_s, vcc_s = kcc_s.reshape(nb_s, n_c_s, kvw), vcc_s.reshape(nb_s, n_c_s, kvw)

    q5 = (q_s * (HEAD_DIM ** -0.5)).reshape(nb_s, n_new, N_KV, 2, HEAD_DIM).transpose(0, 3, 2, 1, 4)
    q5 = jnp.pad(q5, ((0, 0), (0, 0), (0, 0), (0, NQ_PAD - n_new), (0, 0)))
    qbd = jnp.einsum("brgqd,gh->brgqhd", q5, jnp.eye(N_KV, dtype=F32))
    qbd = qbd.reshape(nb_s, 2 * N_KV * NQ_PAD, kvw).astype(BF16)
    g5 = gate_s[:, :3 * N_HEADS].reshape(nb_s, n_new, N_KV, 2, 3).transpose(0, 3, 2, 1, 4)
    g5 = jnp.pad(g5, ((0, 0), (0, 0), (0, 0), (0, NQ_PAD - n_new), (0, 5)))
    gsm = g5.reshape(nb_s, 2 * N_KV * NQ_PAD, 8)
    new = lambda a: a.reshape(nb_s, n_new, kvw)
    n_keys = (n_pages + 1) * page
    ov_s = _overlap(n_c_s, LANES).T
    tri_s = jnp.tril(jnp.ones((LANES, LANES), BF16))
    eexp = (jnp.arange(n_keys)[None, :] // L_SLC == jnp.arange(LANES)[:, None]).astype(BF16)
    os_ = _attn_sample(page_table, qbd, gsm, kcc_s, vcc_s, cks, cvs, _tail_page(new(ks_s), page),
                       _tail_page(new(vs_s), page), fm(state_k_win[l]), fm(state_v_win[l]),
                       _tail_page(new(kw_s), page), _tail_page(new(vw_s), page), ov_s, tri_s, eexp, past_len)
    b_s = os_[:, :, :n_new].reshape(nb_s, 2, n_new, N_KV, HEAD_DIM).transpose(0, 2, 3, 1, 4)
    b_s = b_s.reshape(nb_s * n_new, N_HEADS * HEAD_DIM).astype(BF16)

    y_s = _finish(xs, a_s, b_s, mods_s[2:], True, 0, wout_b, gffn, win_b, wo2_b, CHUNK)

    wb_p = min(WINDOW, seq)
    assert seq - ((seq - 1) // CHUNK) * CHUNK == CHUNK, "the prompt ends on a full chunk"
    from_cols = lambda t: t.reshape(1, nb_p, N_KV, HEAD_DIM, seq).transpose(0, 1, 4, 2, 3)
    outs_p = [from_cols(kct), from_cols(vct), from_cols(kst), from_cols(vst),
              from_cols(kwt)[:, :, seq - wb_p:], from_cols(vwt)[:, :, seq - wb_p:], vn_p[None]]
    kv5 = lambda a, nb, t: a.reshape(1, nb, t, N_KV, HEAD_DIM)
    kw_all = jnp.concatenate([state_k_win[l], kv5(kw_s, nb_s, n_new)[0]], axis=1)
    vw_all = jnp.concatenate([state_v_win[l], kv5(vw_s, nb_s, n_new)[0]], axis=1)
    outs_s = [kv5(kc_s, nb_s, n_new), kv5(vc_s, nb_s, n_new), kv5(ks_s, nb_s, n_new), kv5(vs_s, nb_s, n_new),
              kw_all[None, :, n_new:], vw_all[None, :, n_new:], vn_s.reshape(1, nb_s, n_new, -1)]
    return (y_p.reshape(nb_p, seq, d), y_s.reshape(nb_s, n_new, d), *outs_p, *outs_s)
```

```python
import functools
import math

import jax
import jax.numpy as jnp
from jax import lax
from jax.experimental import pallas as pl
from jax.experimental.pallas import tpu as pltpu

F32 = jnp.float32
BF16 = jnp.bfloat16

CHUNK = 128
A_GROUPS = 8
HEAD_DIM = 64
N_HEADS = 8
N_KV = 4
L_CMP = 32
STRIDE = 16
CMP_HID = 256
L_SLC = 64
N_SEL = 16
WINDOW = 512
ROPE_THETA = 10000.0
EPS = 1e-6
NEG = -1e30
SEL_BIAS = -(2.0 ** 100)

LANES = 128
V_ROWS = HEAD_DIM + 16
VMEM_LIMIT = 52 * 1024 * 1024
PROMPT_ROWS = 512

_NT = (((1,), (1,)), ((), ()))


def _dot(a, b):
    return jnp.dot(a, b, preferred_element_type=F32)


def _dot_nt(a, b):
    return lax.dot_general(a, b, _NT, preferred_element_type=F32)


def _split_dot_left(coef, x):
    hi = x.astype(BF16)
    lo = (x - hi.astype(F32)).astype(BF16)
    return _dot(coef, hi) + _dot(coef, lo)


def _split_dot_right(x, coef):
    hi = x.astype(BF16)
    lo = (x - hi.astype(F32)).astype(BF16)
    return _dot(hi, coef) + _dot(lo, coef)


def _params(sem, flags=None):
    return pltpu.CompilerParams(dimension_semantics=sem, vmem_limit_bytes=VMEM_LIMIT, flags=flags)


def _staged(stages, n):
    depth = len(stages)
    for step in range(n + depth - 1):
        for k, stage in enumerate(stages):
            if 0 <= step - k < n:
                stage(step - k)


def _with_ones_row(vt):
    n = vt.shape[1]
    row = lax.broadcasted_iota(jnp.int32, (V_ROWS - HEAD_DIM, n), 0)
    return jnp.concatenate([vt, jnp.where(row == 0, 1.0, 0.0).astype(vt.dtype)], axis=0)


def _ada_kernel(c_ref, w_ref, b_ref, o_ref):
    c = c_ref[...]
    o_ref[...] = _dot(jax.nn.silu(c).astype(BF16), w_ref[...]) + b_ref[...]


def _ada(c, w, b):
    m, k = c.shape
    n = w.shape[1]
    tn = 1024
    return pl.pallas_call(
        _ada_kernel,
        grid=(n // tn,),
        in_specs=[pl.BlockSpec((m, k), lambda j: (0, 0)),
                  pl.BlockSpec((k, tn), lambda j: (0, j)),
                  pl.BlockSpec((1, tn), lambda j: (0, j))],
        out_specs=pl.BlockSpec((m, tn), lambda j: (0, j)),
        out_shape=jax.ShapeDtypeStruct((m, n), F32),
        compiler_params=_params(("parallel",)),
        name="ada",
    )(c, w, b)


def _group_mean_sq(y, g_ref):
    y2 = y * y
    cols = []
    for c in range(y.shape[1] // 256):
        cols.append(_split_dot_right(y2[:, 256 * c:256 * (c + 1)], g_ref[...]))
    return cols[0] if len(cols) == 1 else jnp.concatenate(cols, axis=1)


def _rope(x, cos, sin):
    n = x.shape[1]
    reps = n // LANES
    cos_t = cos if reps == 1 else jnp.concatenate([cos] * reps, axis=1)
    sin_t = sin if reps == 1 else jnp.concatenate([sin] * reps, axis=1)
    lane = lax.broadcasted_iota(jnp.int32, x.shape, 1)
    first_half = (lane & (HEAD_DIM - 1)) < (HEAD_DIM // 2)
    partner = jnp.where(first_half, pltpu.roll(x, n - HEAD_DIM // 2, 1), pltpu.roll(x, HEAD_DIM // 2, 1))
    return x * cos_t + partner * sin_t


def _proj_common(x_ref, sh_ref, sc_ref, gmix_ref, w_ref, cos_ref, sin_ref, gsgu_ref, gq_ref, gkc_ref, gks_ref,
                 gkw_ref, wmix_ref, bmix_ref, gmat_ref, a_ref, vn_ref):
    x = x_ref[...]
    tm = x.shape[0]
    ms = jnp.mean(x * x, axis=-1, keepdims=True)
    h = x * lax.rsqrt(ms + EPS) * gmix_ref[...]
    h = h * (1.0 + sc_ref[...]) + sh_ref[...]
    hb = h.astype(BF16)
    cos = cos_ref[...]
    sin = sin_ref[...]

    def seg(lo, hi):
        return _dot(hb, w_ref[:, lo:hi])

    def head_norm(y, g_ref):
        return y * lax.rsqrt(_group_mean_sq(y, gmat_ref) + EPS) * g_ref[...]

    u = jax.nn.gelu(seg(0, 512))
    v = jax.nn.gelu(seg(512, 1024))
    vn = head_norm(v, gsgu_ref)
    vn_ref[...] = vn[tm - CHUNK:]
    vb = vn.astype(BF16)
    lane = lax.broadcasted_iota(jnp.int32, (CHUNK, LANES), 1)
    low = lane < HEAD_DIM
    for ck in range(tm // CHUNK):
        rows = slice(ck * CHUNK, (ck + 1) * CHUNK)
        for pr in range(A_GROUPS // 2):
            cols = slice(pr * LANES, (pr + 1) * LANES)
            vp = vb[rows, cols]
            mixed = jnp.where(low, _dot(wmix_ref[2 * pr], vp), _dot(wmix_ref[2 * pr + 1], vp))
            mixed = mixed + bmix_ref[:, cols]
            a_ref[rows, cols] = (u[rows, cols] * mixed).astype(a_ref.dtype)

    q = _rope(head_norm(seg(1024, 1536), gq_ref), cos, sin)
    kc = _rope(head_norm(seg(1536, 1792), gkc_ref), cos, sin)
    vc = seg(1792, 2048)
    ks = _rope(head_norm(seg(2048, 2304), gks_ref), cos, sin)
    vs = seg(2304, 2560)
    kw = _rope(head_norm(seg(2560, 2816), gkw_ref), cos, sin)
    vw = seg(2816, 3072)
    gates = jax.nn.sigmoid(seg(3072, 3200))
    return q, kc, vc, ks, vs, kw, vw, gates


def _proj_rows_kernel(*refs):
    ins, (a_ref, vn_ref, q_ref, kc_ref, vc_ref, ks_ref, vs_ref, kw_ref, vw_ref, gate_ref) = refs[:15], refs[15:]
    outs = _proj_common(*ins, a_ref, vn_ref)
    for ref, val in zip((q_ref, kc_ref, vc_ref, ks_ref, vs_ref, kw_ref, vw_ref, gate_ref), outs):
        ref[...] = val


def _proj_cols_kernel(*refs, q_scale):
    ins, oh_ref = refs[:15], refs[15]
    (a_ref, vn_ref, qt_ref, gt_ref, kct_ref, vct_ref, kst_ref, vst_ref, kwt_ref, vwt_ref,
     ksa_ref, vsa_ref, kwa_ref, vwa_ref, stage_ref) = refs[16:]
    q, kc, vc, ks, vs, kw, vw, gates = _proj_common(*ins, a_ref, vn_ref)
    tm = q.shape[0]
    qt_ref[...] = (q * q_scale).T.astype(BF16)
    gt_ref[...] = gates.T[0:gt_ref.shape[0]]
    kct_ref[...] = kc.T
    kst_ref[...] = ks.T
    kwt_ref[...] = kw.T
    stage_ref[0] = vc
    stage_ref[1] = vs
    stage_ref[2] = vw
    vct_ref[...] = stage_ref[0].T
    vst = stage_ref[1].T
    vwt = stage_ref[2].T
    vst_ref[...] = vst
    vwt_ref[...] = vwt
    low = lax.broadcasted_iota(jnp.int32, (tm, LANES), 1) < HEAD_DIM
    onehot = oh_ref[...]
    for g in range(N_KV):
        pair = slice((g // 2) * LANES, (g // 2 + 1) * LANES)
        ks_g, kw_g = ks[:, pair], kw[:, pair]
        if g % 2 == 1:
            ks_g, kw_g = pltpu.roll(ks_g, HEAD_DIM, 1), pltpu.roll(kw_g, HEAD_DIM, 1)
        ksa_ref[g] = jnp.where(low, ks_g, onehot).astype(BF16)
        kwa_ref[g] = jnp.where(low, kw_g, 0.0).astype(BF16)
        rows = slice(g * HEAD_DIM, (g + 1) * HEAD_DIM)
        vsa_ref[g] = _with_ones_row(vst[rows]).astype(BF16)
        vwa_ref[g] = _with_ones_row(vwt[rows]).astype(BF16)


def _proj_in_specs(tm, d, per_row_mod, rows_per_mod, rope_tiles, w_pad, wmix, bmix, gmat):
    if per_row_mod:
        mod_spec = pl.BlockSpec((tm, d), lambda i: (i, 0))
    else:
        tiles_per_mod = rows_per_mod // tm
        mod_spec = pl.BlockSpec((None, 1, d), lambda i: (i // tiles_per_mod, 0, 0))
    const = lambda shape: pl.BlockSpec(shape, lambda i: (0,) * len(shape))
    rope_spec = pl.BlockSpec((tm, LANES), lambda i: (i % rope_tiles, 0))
    w_spec = pl.BlockSpec(w_pad.shape, lambda i: (0, 0), pipeline_mode=pl.Buffered(1))
    return [pl.BlockSpec((tm, d), lambda i: (i, 0)), mod_spec, mod_spec, const((1, d)), w_spec,
            rope_spec, rope_spec,
            const((1, 512)), const((1, 512)), const((1, 256)), const((1, 256)), const((1, 256)),
            const(wmix.shape), const(bmix.shape), const(gmat.shape)]


def _proj_rows(x, sh, sc, gmix, w_pad, cos, sin, gsgu, gq, gkc, gks, gkw, wmix, bmix, gmat, tm):
    r, d = x.shape
    row = lambda n: pl.BlockSpec((tm, n), lambda i: (i, 0))
    out_widths = [512, 512, 512, 256, 256, 256, 256, 256, 256, 128]
    out_dtypes = [BF16] + [F32] * 9
    return pl.pallas_call(
        _proj_rows_kernel,
        grid=(r // tm,),
        in_specs=_proj_in_specs(tm, d, True, 0, cos.shape[0] // tm, w_pad, wmix, bmix, gmat),
        out_specs=[row(n) for n in out_widths],
        out_shape=[jax.ShapeDtypeStruct((r, n), dt) for n, dt in zip(out_widths, out_dtypes)],
        compiler_params=_params(("parallel",)),
        name="proj_rows",
    )(x, sh, sc, gmix, w_pad, cos, sin, gsgu, gq, gkc, gks, gkw, wmix, bmix, gmat)


def _proj_cols(x, nb, seq, sh, sc, gmix, w_pad, cos, sin, onehot, gsgu, gq, gkc, gks, gkw, wmix, bmix, gmat, tm,
               q_scale):
    r, d = x.shape
    tpb = seq // tm
    kvw = N_KV * HEAD_DIM
    bi = lambda i: i // tpb
    ti = lambda i: i % tpb
    row = lambda n: pl.BlockSpec((tm, n), lambda i: (i, 0))
    colt = lambda n: pl.BlockSpec((None, n, tm), lambda i: (bi(i), 0, ti(i)))
    out_specs = [row(512),
                 pl.BlockSpec((None, CHUNK, 512), lambda i: (bi(i), 0, 0)),
                 colt(512), colt(32),
                 colt(kvw), colt(kvw), colt(kvw), colt(kvw), colt(kvw), colt(kvw),
                 pl.BlockSpec((None, N_KV, tm, LANES), lambda i: (bi(i), 0, ti(i), 0)),
                 pl.BlockSpec((None, N_KV, V_ROWS, tm), lambda i: (bi(i), 0, 0, ti(i))),
                 pl.BlockSpec((None, N_KV, tm, LANES), lambda i: (bi(i), 0, ti(i), 0)),
                 pl.BlockSpec((None, N_KV, V_ROWS, tm), lambda i: (bi(i), 0, 0, ti(i)))]
    sds = jax.ShapeDtypeStruct
    out_shape = [sds((r, 512), BF16), sds((nb, CHUNK, 512), F32), sds((nb, 512, seq), BF16),
                 sds((nb, 32, seq), F32)] + [sds((nb, kvw, seq), F32)] * 6 + [
                 sds((nb, N_KV, seq, LANES), BF16), sds((nb, N_KV, V_ROWS, seq), BF16),
                 sds((nb, N_KV, seq, LANES), BF16), sds((nb, N_KV, V_ROWS, seq), BF16)]
    in_specs = _proj_in_specs(tm, d, False, seq, tpb, w_pad, wmix, bmix, gmat)
    in_specs.append(pl.BlockSpec((tm, LANES), lambda i: (ti(i), 0)))
    return pl.pallas_call(
        functools.partial(_proj_cols_kernel, q_scale=q_scale),
        grid=(r // tm,),
        in_specs=in_specs,
        out_specs=out_specs,
        out_shape=out_shape,
        scratch_shapes=[pltpu.VMEM((3, tm, kvw), F32)],
        compiler_params=_params(("arbitrary",)),
        name="proj_cols",
    )(x, sh, sc, gmix, w_pad, cos, sin, gsgu, gq, gkc, gks, gkw, wmix, bmix, gmat, onehot)


def _compress_kernel(*refs, n_prefetch, n_page, n_item, transposed_out):
    refs = refs[n_prefetch:]
    page_refs = refs[:n_page]
    (w1_ref, w1a_ref, w1b_ref, pe_ref, w2_ref, o_ref,
     xs_ref, lhs_ref, a_ref, b_ref, hid_ref, stage_ref) = refs[n_page:]
    page = page_refs[0].shape[1]
    cpp = page // STRIDE
    ppi = n_page // n_item
    m = ppi * cpp
    rows = N_KV * m
    pitch = xs_ref.shape[2] // STRIDE
    low = lax.broadcasted_iota(jnp.int32, (m, LANES), 1) < HEAD_DIM
    pe_term = _dot(jnp.broadcast_to(pe_ref[...], (8, pe_ref.shape[1])).astype(BF16), w1_ref[...])[0:1]
    b_ref[rows:rows + 8, :] = jnp.zeros((8, CMP_HID), F32)

    def planes(it):
        for pg in range(ppi):
            xt = page_refs[it * ppi + pg][...].T
            for pp in range(N_KV // 2):
                for c in range(cpp):
                    xs_ref[it % 2, pp, pl.ds(pg * cpp + c, STRIDE, stride=pitch), :] = (
                        xt[c * STRIDE:(c + 1) * STRIDE, pp * LANES:(pp + 1) * LANES])

    def relayout(it):
        for rr in range(STRIDE // 2):
            cols = slice(rr * LANES, (rr + 1) * LANES)
            for pp in range(N_KV // 2):
                p0 = xs_ref[it % 2, pp, 2 * rr * pitch:2 * rr * pitch + m, :]
                p1 = xs_ref[it % 2, pp, (2 * rr + 1) * pitch:(2 * rr + 1) * pitch + m, :]
                r0 = pltpu.roll(p0, HEAD_DIM, 1)
                r1 = pltpu.roll(p1, HEAD_DIM, 1)
                g0 = 2 * pp
                lhs_ref[it % 2, g0 * m:(g0 + 1) * m, cols] = jnp.where(low, p0, r1).astype(BF16)
                lhs_ref[it % 2, (g0 + 1) * m:(g0 + 2) * m, cols] = jnp.where(low, r0, p1).astype(BF16)

    def mlp(it):
        a_ref[...] = _dot(lhs_ref[it % 2], w1a_ref[...])
        b_ref[0:rows, :] = _dot(lhs_ref[it % 2], w1b_ref[...])
        hid_ref[...] = jax.nn.gelu(a_ref[...] + b_ref[pl.ds(1, rows), :] + pe_term).astype(BF16)
        out = _dot(hid_ref[0:m, :], w2_ref[0])
        for g in range(1, N_KV):
            out = out + _dot(hid_ref[g * m:(g + 1) * m, :], w2_ref[g])
        if transposed_out:
            stage_ref[...] = out
            out_t = stage_ref[...].T
            for g in range(N_KV):
                o_ref[g] = _with_ones_row(out_t[g * LANES:g * LANES + HEAD_DIM]).astype(o_ref.dtype)
        else:
            o_ref[it * m:(it + 1) * m, :] = out.astype(o_ref.dtype)

    _staged([planes, relayout, mlp], n_item)


def _compress_call(page_array, page_specs, page, grid, weights, prefetch=None, transposed_out=False, n_item=1):
    n_page = len(page_specs)
    assert not (transposed_out and n_item > 1)
    m_all = n_page * page // STRIDE
    m = m_all // n_item
    n_out = weights[-1].shape[2]
    nsp = 0 if prefetch is None else 1
    const = lambda shape: pl.BlockSpec(shape, lambda *a: (0,) * len(shape))
    if transposed_out:
        out_spec = pl.BlockSpec((None, N_KV, V_ROWS, m_all), lambda i, *a: (i, 0, 0, 0))
        out_shape = jax.ShapeDtypeStruct((grid[0], N_KV, V_ROWS, m_all), BF16)
    else:
        out_spec = pl.BlockSpec((None, m_all, n_out), lambda i, *a: (i, 0, 0))
        out_shape = jax.ShapeDtypeStruct((grid[0], m_all, n_out), BF16)
    n_buf = min(n_item, 2)
    grid_spec = pltpu.PrefetchScalarGridSpec(
        num_scalar_prefetch=nsp,
        grid=grid,
        in_specs=list(page_specs) + [const(w.shape) for w in weights],
        out_specs=out_spec,
        scratch_shapes=[pltpu.VMEM((n_buf, N_KV // 2, STRIDE * (m + 8), LANES), F32),
                        pltpu.VMEM((n_buf, N_KV * m, STRIDE * HEAD_DIM), BF16),
                        pltpu.VMEM((N_KV * m, CMP_HID), F32),
                        pltpu.VMEM((N_KV * m + 8, CMP_HID), F32),
                        pltpu.VMEM((N_KV * m, CMP_HID), BF16),
                        pltpu.VMEM((m, n_out), F32)],
    )
    args = ([] if prefetch is None else [prefetch]) + [page_array] * n_page + list(weights)
    return pl.pallas_call(
        functools.partial(_compress_kernel, n_prefetch=nsp, n_page=n_page, n_item=n_item,
                          transposed_out=transposed_out),
        grid_spec=grid_spec,
        out_shape=out_shape,
        compiler_params=_params(("parallel",)),
        name="compress",
    )(*args)


def _compress_weights(pe, w1, w2, head_stride):
    half = STRIDE * HEAD_DIM
    w1b = w1.astype(BF16)
    w2p = jnp.pad(w2, ((0, 0), (0, head_stride - HEAD_DIM)))
    w2blk = jnp.einsum("gh,kd->gkhd", jnp.eye(N_KV, dtype=F32), w2p)
    w2blk = w2blk.reshape(N_KV, CMP_HID, N_KV * head_stride).astype(BF16)
    return w1b, w1b[:half], w1b[half:], pe.reshape(1, L_CMP * HEAD_DIM), w2blk


N_FORCED = 3


def _select_blocks(imp, valid, forced, tri):
    rest = jnp.where(valid & jnp.logical_not(forced), imp, -jnp.inf)
    k = N_SEL - jnp.sum(jnp.where(forced, 1.0, 0.0), axis=0, keepdims=True)

    def body(_, carry):
        thr, cnt = carry
        mx = jnp.max(jnp.where(rest < thr, rest, -jnp.inf), axis=0, keepdims=True)
        c = jnp.sum(jnp.where(rest >= mx, 1.0, 0.0), axis=0, keepdims=True)
        upd = cnt < k
        return jnp.where(upd, mx, thr), jnp.where(upd, c, cnt)

    thr, _ = lax.fori_loop(0, N_SEL - N_FORCED, body, (jnp.full(k.shape, jnp.inf, F32), jnp.zeros(k.shape, F32)))
    above = rest > thr
    tie = (rest == thr) & jnp.logical_not(forced)
    need = k - jnp.sum(jnp.where(above, 1.0, 0.0), axis=0, keepdims=True)
    rank = _dot(tri, jnp.where(tie, 1.0, 0.0).astype(BF16))
    return forced | above | (tie & (rank <= need))


TK = 1024
BIAS_ROWS = TK // L_SLC
N_WBLK = WINDOW // CHUNK + 1
REACH_VARIANTS = 4


def _attn_prompt_kernel(qt_ref, gt_ref, kcc_ref, vcc_ref, ks_ref, vs_ref, *refs):
    kw_refs, vw_refs = refs[:N_WBLK], refs[N_WBLK:2 * N_WBLK]
    (ovt_ref, tri_ref, dc_ref, dd_ref, wb_ref, o_ref,
     qaug_ref, sel_ref, m_ref, acc_ref, ob_ref, sc_ref, s_ref, p_ref, tmax_ref, alpha_ref, hi_ref,
     lo_ref) = refs[2 * N_WBLK:]
    i = pl.program_id(1)
    qs = i * CHUNK
    nq = CHUNK
    gt = gt_ref[...]

    def gate(g, branch):
        return jnp.concatenate([gt[6 * g + branch:6 * g + branch + 1],
                                gt[6 * g + 3 + branch:6 * g + 3 + branch + 1]], axis=1)

    def compressed_and_selection(nc, ns):
        n_idx = lax.broadcasted_iota(jnp.int32, (ns, nq), 0)
        cur = (qs + lax.broadcasted_iota(jnp.int32, (ns, nq), 1)) // L_SLC
        valid = n_idx <= cur
        forced = valid & ((n_idx == 0) | (n_idx >= cur - 1))

        def cmp_scores(g):
            base = 2 * g * HEAD_DIM
            qaug_ref[g, 0:HEAD_DIM, :] = jnp.concatenate(
                [qt_ref[base:base + HEAD_DIM, :], qt_ref[base + HEAD_DIM:base + 2 * HEAD_DIM, :]], axis=1)
            qaug_ref[g, HEAD_DIM:, :] = jnp.zeros((LANES - HEAD_DIM, 2 * nq), BF16)
            s = jnp.where(dc_ref[0:nc, :] <= qs,
                          _dot(kcc_ref[0:nc, g * LANES:(g + 1) * LANES], qaug_ref[g]), NEG)
            s_ref[g % 2, 0:nc, :] = s
            mx = jnp.max(s, axis=0, keepdims=True)
            tmax_ref[g % 2] = jnp.where(mx > 0.5 * NEG, mx, 0.0)
            m_ref[g] = jnp.full((1, 2 * nq), NEG, F32)
            acc_ref[g] = jnp.zeros((V_ROWS, 2 * nq), F32)

        def cmp_exps(g):
            p = jnp.exp2(s_ref[g % 2, 0:nc, :] - tmax_ref[g % 2])
            inv = 1.0 / jnp.maximum(jnp.sum(p, axis=0, keepdims=True), 1e-20)
            alpha_ref[g % 2] = inv
            p_ref[g % 2, 0:nc, :] = p.astype(BF16)
            psum = p[:, :nq] * inv[:, :nq] + p[:, nq:] * inv[:, nq:]
            hi = psum.astype(BF16)
            hi_ref[g % 2, 0:nc, :] = hi
            lo_ref[g % 2, 0:nc, :] = (psum - hi.astype(F32)).astype(BF16)

        def cmp_values(g):
            oc = _dot(vcc_ref[g, :, 0:nc], p_ref[g % 2, 0:nc, :])
            ob_ref[g] = gate(g, 0) * (oc[0:HEAD_DIM] * alpha_ref[g % 2])
            ov = ovt_ref[0:ns, 0:nc]
            sc_ref[0:ns, g * nq:(g + 1) * nq] = _dot(ov, hi_ref[g % 2, 0:nc, :]) + _dot(ov, lo_ref[g % 2, 0:nc, :])

        _staged([cmp_scores, cmp_exps, cmp_values], N_KV)

        sel = _select_blocks(sc_ref[0:ns, :], jnp.concatenate([valid] * N_KV, axis=1),
                             jnp.concatenate([forced] * N_KV, axis=1), tri_ref[0:ns, 0:ns])
        before = lax.broadcasted_iota(jnp.int32, sel.shape, 0) < 2 * i
        sel_bias = jnp.where(sel & before, 0.0, SEL_BIAS)
        for g in range(N_KV):
            sel_ref[g, 0:ns, :] = sel_bias[:, g * nq:(g + 1) * nq]

    nc_all, ns_all = kcc_ref.shape[0], sel_ref.shape[1]
    per_quarter = pl.num_programs(1) // REACH_VARIANTS
    for quarter in range(REACH_VARIANTS):
        reach = quarter + 1

        @pl.when((i >= quarter * per_quarter) & (i < reach * per_quarter))
        def _():
            compressed_and_selection(nc_all * reach // REACH_VARIANTS, ns_all * reach // REACH_VARIANTS)

    n_tiles = (qs + TK - 1) // TK
    last_tile = jnp.maximum(n_tiles - 1, 0)
    half = TK // 2

    def scores_half(kt, g, buf, h):
        kc = jnp.clip(kt, 0, last_tile)
        k0 = pl.multiple_of(kc * TK, TK)
        if h == 0:
            bias = sel_ref[g, pl.ds(pl.multiple_of(kc * BIAS_ROWS, BIAS_ROWS), BIAS_ROWS), :]
            qaug_ref[g, HEAD_DIM:HEAD_DIM + BIAS_ROWS, :] = jnp.concatenate([bias, bias], axis=1).astype(BF16)
        rows = slice(h * half, (h + 1) * half)
        st = _dot(ks_ref[g, pl.ds(k0 + h * half, half), :], qaug_ref[g])
        s_ref[buf, rows, :] = st
        cmax = jnp.max(st, axis=0, keepdims=True)
        tmax_ref[buf] = cmax if h == 0 else jnp.maximum(tmax_ref[buf], cmax)

    def exps_half(g, buf, h):
        if h == 0:
            m_old = m_ref[g]
            m_new = jnp.maximum(m_old, tmax_ref[buf])
            m_ref[g] = m_new
            alpha_ref[buf] = jnp.exp2(m_old - m_new)
        m_new = m_ref[g]
        for c in range(2):
            rows = slice(h * half + c * (half // 2), h * half + (c + 1) * (half // 2))
            p_ref[buf, rows, :] = jnp.exp2(s_ref[buf, rows, :] - m_new).astype(BF16)

    def values_half(kt, g, buf, h):
        k0 = pl.multiple_of(jnp.clip(kt, 0, last_tile) * TK, TK)
        rows = slice(h * half, (h + 1) * half)
        part = _dot(vs_ref[g, :, pl.ds(k0 + h * half, half)], p_ref[buf, rows, :])
        acc_ref[g] = (alpha_ref[buf] * acc_ref[g] if h == 0 else acc_ref[g]) + part

    def substep(kt, g):
        g_next, kt_next = (g + 1) % N_KV, kt + (g + 1) // N_KV
        g_prev, kt_prev = (g - 1) % N_KV, kt - (1 if g == 0 else 0)
        other = (g + 1) % 2
        for h in range(2):
            scores_half(kt_next, g_next, other, h)
            exps_half(g, g % 2, h)
            values_half(kt_prev, g_prev, other, h)

    def tile(kt, carry):
        for g in range(N_KV):
            substep(kt, g)
        return carry

    p_ref[1] = jnp.zeros(p_ref.shape[1:], BF16)
    alpha_ref[1] = jnp.ones(alpha_ref.shape[1:], F32)
    for h in range(2):
        scores_half(0, 0, 0, h)
    lax.fori_loop(0, n_tiles, tile, 0)
    for h in range(2):
        values_half(n_tiles - 1, N_KV - 1, 1, h)

    own = pl.ds(pl.multiple_of(qs, CHUNK), CHUNK)
    causal = dd_ref[0:CHUNK, :] <= 0

    def own_scores(g):
        qaug_ref[g, HEAD_DIM:HEAD_DIM + BIAS_ROWS, :] = jnp.zeros((BIAS_ROWS, 2 * nq), BF16)
        sd = jnp.where(causal, _dot(ks_ref[g, own, :], qaug_ref[g]), NEG)
        s_ref[g % 2, 0:CHUNK, :] = sd
        tmax_ref[g % 2] = jnp.max(sd, axis=0, keepdims=True)

    def own_exps(g):
        m_old = m_ref[g]
        m_new = jnp.maximum(m_old, tmax_ref[g % 2])
        alpha_ref[g % 2] = jnp.exp2(m_old - m_new)
        p_ref[g % 2, 0:CHUNK, :] = jnp.exp2(s_ref[g % 2, 0:CHUNK, :] - m_new).astype(BF16)

    def own_values(g):
        acc = alpha_ref[g % 2] * acc_ref[g] + _dot(vs_ref[g, :, own], p_ref[g % 2, 0:CHUNK, :])
        ob_ref[g] = ob_ref[g] + gate(g, 1) * (acc[0:HEAD_DIM] * (1.0 / acc[HEAD_DIM:HEAD_DIM + 1]))

    _staged([own_scores, own_exps, own_values], N_KV)

    nw = N_WBLK * CHUNK
    outs = [None] * N_KV

    def win_scores(g):
        kwin = jnp.concatenate([r[g] for r in kw_refs], axis=0)
        bias = jnp.concatenate(
            [jnp.where(i - (N_WBLK - 1) + j < 0, NEG, wb_ref[j * CHUNK:(j + 1) * CHUNK, :]) for j in range(N_WBLK)],
            axis=0)
        sw = _dot(kwin, qaug_ref[g]) + bias
        s_ref[g % 2, 0:nw, :] = sw
        tmax_ref[g % 2] = jnp.max(sw, axis=0, keepdims=True)

    def win_exps(g):
        p_ref[g % 2, 0:nw, :] = jnp.exp2(s_ref[g % 2, 0:nw, :] - tmax_ref[g % 2]).astype(BF16)

    def win_values(g):
        vwin = jnp.concatenate([r[g] for r in vw_refs], axis=1)
        ow = _dot(vwin, p_ref[g % 2, 0:nw, :])
        ob = ob_ref[g] + gate(g, 2) * (ow[0:HEAD_DIM] * (1.0 / ow[HEAD_DIM:HEAD_DIM + 1]))
        outs[g] = [ob[:, :nq], ob[:, nq:]]

    _staged([win_scores, win_exps, win_values], N_KV)
    o_ref[...] = jnp.concatenate(sum(outs, []), axis=0).T.astype(o_ref.dtype)


def _attn_prompt(qt, gt, kcc, vcc, ks_aug, vs_aug, kw_aug, vw_aug, ovt, tri, dc, dd, wb):
    b, _, s = qt.shape
    nqb = s // CHUNK
    nc = kcc.shape[1]
    first = WINDOW // CHUNK
    whole = lambda shape: pl.BlockSpec((None,) + shape, lambda bi, i: (bi,) + (0,) * len(shape),
                                       pipeline_mode=pl.Buffered(1))
    const = lambda a: pl.BlockSpec(a.shape, lambda bi, i: (0,) * a.ndim)

    def kw_spec(j):
        return pl.BlockSpec((None, N_KV, CHUNK, LANES), lambda bi, i: (bi, 0, jnp.maximum(i - first + j, 0), 0))

    def vw_spec(j):
        return pl.BlockSpec((None, N_KV, V_ROWS, CHUNK), lambda bi, i: (bi, 0, 0, jnp.maximum(i - first + j, 0)))

    in_specs = ([pl.BlockSpec((None, N_HEADS * HEAD_DIM, CHUNK), lambda bi, i: (bi, 0, i)),
                 pl.BlockSpec((None, 32, CHUNK), lambda bi, i: (bi, 0, i)),
                 pl.BlockSpec((None, nc, N_KV * LANES), lambda bi, i: (bi, 0, 0)),
                 pl.BlockSpec((None, N_KV, V_ROWS, nc), lambda bi, i: (bi, 0, 0, 0)),
                 whole((N_KV, s, LANES)), whole((N_KV, V_ROWS, s))]
                + [kw_spec(j) for j in range(N_WBLK)] + [vw_spec(j) for j in range(N_WBLK)]
                + [const(ovt), const(tri), const(dc), const(dd), const(wb)])
    return pl.pallas_call(
        _attn_prompt_kernel,
        grid=(b, nqb),
        in_specs=in_specs,
        out_specs=pl.BlockSpec((None, CHUNK, N_HEADS * HEAD_DIM), lambda bi, i: (bi, i, 0)),
        out_shape=jax.ShapeDtypeStruct((b, s, N_HEADS * HEAD_DIM), BF16),
        scratch_shapes=[pltpu.VMEM((N_KV, LANES, 2 * CHUNK), BF16),
                        pltpu.VMEM((N_KV, s // L_SLC, CHUNK), F32),
                        pltpu.VMEM((N_KV, 1, 2 * CHUNK), F32),
                        pltpu.VMEM((N_KV, V_ROWS, 2 * CHUNK), F32),
                        pltpu.VMEM((N_KV, HEAD_DIM, 2 * CHUNK), F32),
                        pltpu.VMEM((s // L_SLC, N_KV * CHUNK), F32),
                        pltpu.VMEM((2, TK, 2 * CHUNK), F32),
                        pltpu.VMEM((2, TK, 2 * CHUNK), BF16),
                        pltpu.VMEM((2, 1, 2 * CHUNK), F32),
                        pltpu.VMEM((2, 1, 2 * CHUNK), F32),
                        pltpu.VMEM((2, nc, CHUNK), BF16),
                        pltpu.VMEM((2, nc, CHUNK), BF16)],
        compiler_params=_params(("parallel", "arbitrary")),
        name="attn_prompt",
    )(qt, gt, kcc, vcc, ks_aug, vs_aug, *([kw_aug] * N_WBLK), *([vw_aug] * N_WBLK), ovt, tri, dc, dd, wb)


NQ_PAD = 8


def _softmax_rows(s, mask):
    s = jnp.where(mask, s, NEG)
    mx = jnp.max(s, axis=1, keepdims=True)
    p = jnp.where(mask, jnp.exp(s - mx), 0.0)
    return p / jnp.maximum(jnp.sum(p, axis=1, keepdims=True), 1e-20)


SAMPLES_PER_STEP = 4


def _attn_sample_kernel(pt_ref, *refs, n_pages, past_len, spb):
    del pt_ref
    per = 10 + 2 * n_pages
    slots = [refs[j * per:(j + 1) * per] for j in range(spb)]
    ovt_ref, tri_ref, eexp_ref, o_ref, s_scr = refs[spb * per:]
    qbd = [r[0][...] for r in slots]
    g_refs, kcc_refs, vcc_refs = ([r[k] for r in slots] for k in (1, 2, 3))
    kpages = [r[4:4 + n_pages] for r in slots]
    vpages = [r[4 + n_pages:4 + 2 * n_pages] for r in slots]
    kst, vst, kws, vws, kwt, vwt = ([r[4 + 2 * n_pages + k] for r in slots] for k in range(6))
    nrow = qbd[0].shape[0]
    half = nrow // 2
    page = kpages[0][0].shape[1]
    every = range(spb)

    def t_of(shape):
        return past_len + (lax.broadcasted_iota(jnp.int32, shape, 0) & (NQ_PAD - 1))

    nc = kcc_refs[0].shape[0]
    c_idx = lax.broadcasted_iota(jnp.int32, (nrow, nc), 1)
    cmask = c_idx * STRIDE + (L_CMP - 1) <= t_of((nrow, nc))
    p_c = [_softmax_rows(_dot_nt(qbd[j], kcc_refs[j][...]), cmask) for j in every]
    o_c = [_dot(p_c[j].astype(BF16), vcc_refs[j][...]) for j in every]

    pad_rows = LANES - spb * half
    psum = jnp.concatenate([p_c[j][:half] + p_c[j][half:] for j in every]
                           + ([jnp.zeros((pad_rows, nc), F32)] if pad_rows else []), axis=0)
    imp = _split_dot_left(ovt_ref[...], psum.T)
    n_idx = lax.broadcasted_iota(jnp.int32, imp.shape, 0)
    cur = (past_len + (lax.broadcasted_iota(jnp.int32, imp.shape, 1) & (NQ_PAD - 1))) // L_SLC
    valid = n_idx <= cur
    forced = valid & ((n_idx == 0) | (n_idx >= cur - 1))
    sel = jnp.where(_select_blocks(imp, valid, forced, tri_ref[...]), 1.0, 0.0).T.astype(BF16)
    sel_keys = [_dot(jnp.concatenate([sel[j * half:(j + 1) * half]] * 2, axis=0), eexp_ref[...]) for j in every]

    for pg in range(n_pages):
        for j in every:
            s_scr[j, :, pg * page:(pg + 1) * page] = _dot(qbd[j], kpages[j][pg][...].astype(BF16))
    for j in every:
        s_scr[j, :, n_pages * page:(n_pages + 1) * page] = _dot(qbd[j], kst[j][...])
    nk = (n_pages + 1) * page
    tok_ok = lax.broadcasted_iota(jnp.int32, (nrow, nk), 1) <= t_of((nrow, nk))
    p_s = [_softmax_rows(s_scr[j], (sel_keys[j] > 0.5) & tok_ok).astype(BF16) for j in every]
    o_s = [_dot_nt(p_s[j][:, n_pages * page:], vst[j][...]) for j in every]
    for pg in range(n_pages):
        for j in every:
            o_s[j] = o_s[j] + _dot_nt(p_s[j][:, pg * page:(pg + 1) * page], vpages[j][pg][...].astype(BF16))

    wb = kws[0].shape[1]
    nw = wb + page
    rel = t_of((nrow, nw)) - (past_len - wb + lax.broadcasted_iota(jnp.int32, (nrow, nw), 1))
    wmask = (rel >= 0) & (rel < WINDOW)
    sw = [jnp.concatenate([_dot(qbd[j], kws[j][...].astype(BF16)), _dot(qbd[j], kwt[j][...])], axis=1) for j in every]
    p_w = [_softmax_rows(sw[j], wmask).astype(BF16) for j in every]
    o_w = [_dot_nt(p_w[j][:, :wb], vws[j][...].astype(BF16)) + _dot_nt(p_w[j][:, wb:], vwt[j][...]) for j in every]

    row_g = (lax.broadcasted_iota(jnp.int32, o_c[0].shape, 0) // NQ_PAD) & (N_KV - 1)
    lane_g = lax.broadcasted_iota(jnp.int32, o_c[0].shape, 1) // HEAD_DIM
    per_r = N_KV * NQ_PAD
    for j in every:
        g = g_refs[j][...]
        o = g[:, 0:1] * o_c[j] + g[:, 1:2] * o_s[j] + g[:, 2:3] * o_w[j]
        o = jnp.where(row_g == lane_g, o, 0.0)
        for r in range(2):
            acc = o[r * per_r:r * per_r + NQ_PAD]
            for gg in range(1, N_KV):
                acc = acc + o[r * per_r + gg * NQ_PAD:r * per_r + (gg + 1) * NQ_PAD]
            o_ref[j, r] = acc


def _attn_sample(page_table, qbd, gsm, kcc, vcc, cache_k, cache_v, k_tail, v_tail, kw_state, vw_state,
                 kw_tail, vw_tail, ovt, tri, eexp, past_len):
    nb, n_pages = page_table.shape
    spb = SAMPLES_PER_STEP
    kvw = cache_k.shape[1]
    page = cache_k.shape[2]
    nrow = qbd.shape[1]
    const = lambda shape: pl.BlockSpec(shape, lambda b, pt: (0,) * len(shape))
    in_specs, args = [], []
    for j in range(spb):
        per_b = lambda shape, j=j: pl.BlockSpec((None,) + shape, lambda b, pt: (b * spb + j,) + (0,) * len(shape))
        page_spec = lambda pg, j=j: pl.BlockSpec((None, kvw, page), lambda b, pt: (pt[b * spb + j, pg], 0, 0))
        in_specs += ([per_b((nrow, kvw)), per_b((nrow, 8)), per_b(kcc.shape[1:]), per_b(vcc.shape[1:])]
                     + [page_spec(pg) for pg in range(n_pages)] + [page_spec(pg) for pg in range(n_pages)]
                     + [per_b((kvw, page)), per_b((kvw, page)), per_b(kw_state.shape[1:]),
                        per_b(vw_state.shape[1:]), per_b((kvw, page)), per_b((kvw, page))])
        args += ([qbd, gsm, kcc, vcc] + [cache_k] * n_pages + [cache_v] * n_pages
                 + [k_tail, v_tail, kw_state, vw_state, kw_tail, vw_tail])
    in_specs += [const(ovt.shape), const(tri.shape), const(eexp.shape)]
    grid_spec = pltpu.PrefetchScalarGridSpec(
        num_scalar_prefetch=1,
        grid=(nb // spb,),
        in_specs=in_specs,
        out_specs=pl.BlockSpec((None, spb, 2, NQ_PAD, kvw), lambda b, pt: (b, 0, 0, 0, 0)),
        scratch_shapes=[pltpu.VMEM((spb, nrow, (n_pages + 1) * page), F32)],
    )
    out = pl.pallas_call(
        functools.partial(_attn_sample_kernel, n_pages=n_pages, past_len=past_len, spb=spb),
        grid_spec=grid_spec,
        out_shape=jax.ShapeDtypeStruct((nb // spb, spb, 2, NQ_PAD, kvw), F32),
        compiler_params=_params(("parallel",)),
        name="attn_sample",
    )(page_table, *args, ovt, tri, eexp)
    return out.reshape(nb, 2, NQ_PAD, kvw)


FF_SPLIT = 2


def _finish_kernel(x_ref, a_ref, b_ref, gt1_ref, sh2_ref, sc2_ref, gt2_ref, wout_ref, gffn_ref, win_ref, wo2_ref,
                   o_ref):
    half = a_ref.shape[1]
    y = _dot(a_ref[...], wout_ref[0:half, :]) + _dot(b_ref[...], wout_ref[half:, :])
    x1 = x_ref[...] + gt1_ref[...] * y
    ms = jnp.mean(x1 * x1, axis=-1, keepdims=True)
    h = x1 * lax.rsqrt(ms + EPS) * gffn_ref[...]
    hb = (h * (1.0 + sc2_ref[...]) + sh2_ref[...]).astype(BF16)
    d_ff = wo2_ref.shape[0]
    step = d_ff // FF_SPLIT
    acc = None
    for c in range(FF_SPLIT):
        up = _dot(hb, win_ref[:, c * step:(c + 1) * step])
        gate = _dot(hb, win_ref[:, d_ff + c * step:d_ff + (c + 1) * step])
        z = (jax.nn.silu(up) * gate).astype(BF16)
        part = _dot(z, wo2_ref[c * step:(c + 1) * step, :])
        acc = part if acc is None else acc + part
    o_ref[...] = x1 + gt2_ref[...] * acc


def _finish(x, a, b, mods, per_row_mod, rows_per_mod, wout, gffn, win, wo2, tm):
    r, d = x.shape
    if per_row_mod:
        mod_spec = pl.BlockSpec((tm, d), lambda i: (i, 0))
    else:
        tiles_per_mod = rows_per_mod // tm
        mod_spec = pl.BlockSpec((None, 1, d), lambda i: (i // tiles_per_mod, 0, 0))
    single = lambda shape: pl.BlockSpec(shape, lambda i: (0,) * len(shape), pipeline_mode=pl.Buffered(1))
    row = lambda n: pl.BlockSpec((tm, n), lambda i: (i, 0))
    return pl.pallas_call(
        _finish_kernel,
        grid=(r // tm,),
        in_specs=[row(d), row(a.shape[1]), row(b.shape[1]), mod_spec, mod_spec, mod_spec, mod_spec,
                  single(wout.shape), single(gffn.shape), single(win.shape), single(wo2.shape)],
        out_specs=row(d),
        out_shape=jax.ShapeDtypeStruct((r, d), F32),
        compiler_params=_params(("parallel",)),
        name="finish",
    )(x, a, b, *mods, wout, gffn, win, wo2)


def _rope_tables(pos):
    half = HEAD_DIM // 2
    inv = ROPE_THETA ** (-jnp.arange(half, dtype=F32) / half)
    ang = pos.astype(F32)[:, None] * inv[None, :]
    cos, sin = jnp.cos(ang), jnp.sin(ang)
    return jnp.concatenate([cos] * 4, axis=1), jnp.concatenate([-sin, sin, -sin, sin], axis=1)


def _overlap(n_c, n_s):
    c_start = jnp.arange(n_c) * STRIDE
    blk = jnp.arange(n_s)
    return ((c_start[:, None] < (blk[None, :] + 1) * L_SLC)
            & (c_start[:, None] + L_CMP > blk[None, :] * L_SLC)).astype(BF16)


def _tail_page(new_rows, page):
    return jnp.pad(new_rows.transpose(0, 2, 1), ((0, 0), (0, 0), (0, page - new_rows.shape[1]))).astype(BF16)


def kernel(x_prompt, x_sample, cache_k_cmp, cache_v_cmp, cache_k_slc, cache_v_slc, state_k_win, state_v_win,
           page_table, c_prompt, c_sample, w_ada, b_ada, g_mix_norm, g_ffn_norm, w_in, g_sgu, w_sgu, b_sgu,
           g_q, g_k_cmp, g_k_slc, g_k_win, pe_k_cmp, pe_v_cmp, w_ck1, w_ck2, w_cv1, w_cv2, w_out, w_ffn_in,
           w_ffn_out):
    depth = w_in.shape[0]
    assert depth == 1, "single trunk layer"
    nb_p, seq, d = x_prompt.shape
    nb_s, n_new, _ = x_sample.shape
    n_pool, page = cache_k_cmp.shape[1], cache_k_cmp.shape[2]
    kvw = N_KV * HEAD_DIM
    n_pages = page_table.shape[1]
    past_len = n_pages * page
    wb_s = state_k_win.shape[2]
    l = 0

    in_cols = w_in.shape[2]
    w_in_pad = jnp.pad(w_in[l], ((0, 0), (0, 3200 - in_cols))).astype(BF16)
    tile_gain = lambda g, heads: jnp.tile(g, heads).reshape(1, heads * HEAD_DIM)
    gq_t, gkc_t = tile_gain(g_q[l], N_HEADS), tile_gain(g_k_cmp[l], N_KV)
    gks_t, gkw_t = tile_gain(g_k_slc[l], N_KV), tile_gain(g_k_win[l], N_KV)
    gsgu = g_sgu[l].reshape(1, -1)
    gmix = g_mix_norm[l].reshape(1, d)
    gffn = g_ffn_norm[l].reshape(1, d)
    gmat = (jnp.kron(jnp.eye(256 // HEAD_DIM, dtype=F32), jnp.ones((HEAD_DIM, HEAD_DIM), F32)) / HEAD_DIM).astype(BF16)
    w_tril = jnp.where(jnp.tril(jnp.ones((CHUNK, CHUNK), bool)), w_sgu[l], 0)
    wmix_p = w_tril.astype(BF16)
    bmix_p = jnp.repeat(b_sgu[l].T, HEAD_DIM, axis=1)
    eye_s = jnp.eye(CHUNK // n_new, dtype=F32)
    wmix_s = jax.vmap(lambda w: jnp.kron(eye_s, w[:n_new, :n_new]))(w_tril).astype(BF16)
    bmix_s = jnp.tile(jnp.repeat(b_sgu[l].T[:n_new], HEAD_DIM, axis=1), (CHUNK // n_new, 1))
    wout_b = w_out[l].astype(BF16)
    win_b = w_ffn_in[l].astype(BF16)
    wo2_b = w_ffn_out[l].astype(BF16)

    n_c = nb_p + nb_s
    n_c_pad = -(-n_c // 8) * 8
    c_all = jnp.pad(jnp.concatenate([c_prompt, c_sample], axis=0), ((0, n_c_pad - n_c), (0, 0)))
    ada = _ada(c_all, w_ada[l].astype(BF16), b_ada[l].reshape(1, -1))
    mods_p = [m.reshape(nb_p, 1, d) for m in jnp.split(ada[:nb_p], 6, axis=-1)]
    mods_s = [jnp.repeat(m, n_new, axis=0) for m in jnp.split(ada[nb_p:n_c], 6, axis=-1)]

    cos_p, sin_p = _rope_tables(jnp.arange(seq))
    cos_s, sin_s = _rope_tables(past_len + jnp.arange(n_new))
    reps = CHUNK // n_new
    cos_s, sin_s = jnp.tile(cos_s, (reps, 1)), jnp.tile(sin_s, (reps, 1))
    xp = x_prompt.reshape(nb_p * seq, d)
    xs = x_sample.reshape(nb_s * n_new, d)
    gains = (gsgu, gq_t, gkc_t, gks_t, gkw_t)
    blk_of_key = (jnp.arange(seq) // L_SLC) % BIAS_ROWS
    onehot = jnp.pad(jax.nn.one_hot(blk_of_key, LANES - HEAD_DIM, dtype=F32), ((0, 0), (HEAD_DIM, 0)))
    q_scale = HEAD_DIM ** -0.5 * math.log2(math.e)
    (a_p, vn_p, qt, gt, kct, vct, kst, vst, kwt, vwt, ks_aug, vs_aug, kw_aug, vw_aug) = _proj_cols(
        xp, nb_p, seq, mods_p[0], mods_p[1], gmix, w_in_pad, cos_p, sin_p, onehot, *gains, wmix_p, bmix_p, gmat,
        PROMPT_ROWS, q_scale)
    (a_s, vn_s, q_s, kc_s, vc_s, ks_s, vs_s, kw_s, vw_s, gate_s) = _proj_rows(
        xs, mods_s[0], mods_s[1], gmix, w_in_pad, cos_s, sin_s, *gains, wmix_s, bmix_s, gmat, CHUNK)

    n_chunk_p = seq // STRIDE
    n_blk_p = seq // L_SLC
    wk_p = _compress_weights(pe_k_cmp[l], w_ck1[l], w_ck2[l], LANES)
    wv_p = _compress_weights(pe_v_cmp[l], w_cv1[l], w_cv2[l], LANES)
    pages_p = [pl.BlockSpec((None, kvw, LANES), functools.partial(lambda pg, i: (i, 0, pg), pg))
               for pg in range(seq // LANES)]
    kcc_p = _compress_call(kct, pages_p, LANES, (nb_p,), wk_p)
    vcc_p = _compress_call(vct, pages_p, LANES, (nb_p,), wv_p, transposed_out=True)
    ovt = _overlap(n_chunk_p, n_blk_p).T
    tri = jnp.tril(jnp.ones((n_blk_p, n_blk_p), BF16))
    q_lane = jnp.arange(2 * CHUNK)[None, :] % CHUNK
    dc = (jnp.arange(n_chunk_p)[:, None] * STRIDE + (L_CMP - 1) - q_lane).astype(jnp.int32)
    dd = (jnp.arange(TK)[:, None] - q_lane).astype(jnp.int32)
    jq = jnp.arange(N_WBLK * CHUNK)[:, None] - q_lane
    wb = jnp.where((jq > 0) & (jq <= WINDOW), 0.0, NEG).astype(F32)
    b_p = _attn_prompt(qt, gt, kcc_p, vcc_p, ks_aug, vs_aug, kw_aug, vw_aug, ovt, tri, dc, dd, wb)
    y_p = _finish(xp, a_p, b_p.reshape(nb_p * seq, N_HEADS * HEAD_DIM), mods_p[2:], False, seq, wout_b, gffn,
                  win_b, wo2_b, PROMPT_ROWS)

    fm = lambda a: a.transpose(0, 2, 3, 1).reshape(a.shape[0], kvw, a.shape[1])
    ck, cv, cks, cvs = fm(cache_k_cmp[l]), fm(cache_v_cmp[l]), fm(cache_k_slc[l]), fm(cache_v_slc[l])
    wk_s = _compress_weights(pe_k_cmp[l], w_ck1[l], w_ck2[l], HEAD_DIM)
    wv_s = _compress_weights(pe_v_cmp[l], w_cv1[l], w_cv2[l], HEAD_DIM)
    spb = (seq // LANES) // n_pages
    assert nb_s % spb == 0 and page == LANES

    def page_spec(j):
        return pl.BlockSpec((None, kvw, page), lambda i, pt: (pt[i * spb + j // n_pages, j % n_pages], 0, 0))

    pages_s = [page_spec(j) for j in range(spb * n_pages)]
    n_c_s = past_len // STRIDE
    n_item = 2
    kcc_s = _compress_call(ck, pages_s, page, (nb_s // spb,), wk_s, prefetch=page_table, n_item=n_item)
    vcc_s = _compress_call(cv, pages_s, page, (nb_s // spb,), wv_s, prefetch=page_table, n_item=n_item)
    kcc_s, vcc_s = kcc_s.reshape(nb_s, n_c_s, kvw), vcc_s.reshape(nb_s, n_c_s, kvw)

    q5 = (q_s * (HEAD_DIM ** -0.5)).reshape(nb_s, n_new, N_KV, 2, HEAD_DIM).transpose(0, 3, 2, 1, 4)
    q5 = jnp.pad(q5, ((0, 0), (0, 0), (0, 0), (0, NQ_PAD - n_new), (0, 0)))
    qbd = jnp.einsum("brgqd,gh->brgqhd", q5, jnp.eye(N_KV, dtype=F32))
    qbd = qbd.reshape(nb_s, 2 * N_KV * NQ_PAD, kvw).astype(BF16)
    g5 = gate_s[:, :3 * N_HEADS].reshape(nb_s, n_new, N_KV, 2, 3).transpose(0, 3, 2, 1, 4)
    g5 = jnp.pad(g5, ((0, 0), (0, 0), (0, 0), (0, NQ_PAD - n_new), (0, 5)))
    gsm = g5.reshape(nb_s, 2 * N_KV * NQ_PAD, 8)
    new = lambda a: a.reshape(nb_s, n_new, kvw)
    n_keys = (n_pages + 1) * page
    ov_s = _overlap(n_c_s, LANES).T
    tri_s = jnp.tril(jnp.ones((LANES, LANES), BF16))
    eexp = (jnp.arange(n_keys)[None, :] // L_SLC == jnp.arange(LANES)[:, None]).astype(BF16)
    os_ = _attn_sample(page_table, qbd, gsm, kcc_s, vcc_s, cks, cvs, _tail_page(new(ks_s), page),
                       _tail_page(new(vs_s), page), fm(state_k_win[l]), fm(state_v_win[l]),
                       _tail_page(new(kw_s), page), _tail_page(new(vw_s), page), ov_s, tri_s, eexp, past_len)
    b_s = os_[:, :, :n_new].reshape(nb_s, 2, n_new, N_KV, HEAD_DIM).transpose(0, 2, 3, 1, 4)
    b_s = b_s.reshape(nb_s * n_new, N_HEADS * HEAD_DIM).astype(BF16)

    y_s = _finish(xs, a_s, b_s, mods_s[2:], True, 0, wout_b, gffn, win_b, wo2_b, CHUNK)

    wb_p = min(WINDOW, seq)
    assert seq - ((seq - 1) // CHUNK) * CHUNK == CHUNK, "the prompt ends on a full chunk"
    from_cols = lambda t: t.reshape(1, nb_p, N_KV, HEAD_DIM, seq).transpose(0, 1, 4, 2, 3)
    outs_p = [from_cols(kct), from_cols(vct), from_cols(kst), from_cols(vst),
              from_cols(kwt)[:, :, seq - wb_p:], from_cols(vwt)[:, :, seq - wb_p:], vn_p[None]]
    kv5 = lambda a, nb, t: a.reshape(1, nb, t, N_KV, HEAD_DIM)
    kw_all = jnp.concatenate([state_k_win[l], kv5(kw_s, nb_s, n_new)[0]], axis=1)
    vw_all = jnp.concatenate([state_v_win[l], kv5(vw_s, nb_s, n_new)[0]], axis=1)
    outs_s = [kv5(kc_s, nb_s, n_new), kv5(vc_s, nb_s, n_new), kv5(ks_s, nb_s, n_new), kv5(vs_s, nb_s, n_new),
              kw_all[None, :, n_new:], vw_all[None, :, n_new:], vn_s.reshape(1, nb_s, n_new, -1)]
    return (y_p.reshape(nb_p, seq, d), y_s.reshape(nb_s, n_new, d), *outs_p, *outs_s)
```
